```python
import functools
import jax, jax.numpy as jnp
from jax import lax
import numpy as np

D_MODEL = 2048
BATCH = 1
SEQ = 8192
DEPTH = 1
DEC_BATCH = 128
DEC_SEQ = 4
PAST_LEN = 2048
PAGE_SIZE = 128

CONV_CH = D_MODEL // 2
CONV_WIDTH = 31
N_HEADS = 16
HEAD_DIM = (D_MODEL - CONV_CH) // N_HEADS
N_KV = 4
GQA = N_HEADS // N_KV
Q_WIDTH = N_HEADS * HEAD_DIM
KV_WIDTH = N_KV * HEAD_DIM
MIX_WIDTH = CONV_CH + Q_WIDTH
N_PROJ = 2 * CONV_CH + Q_WIDTH + 6 * KV_WIDTH + 3 * N_HEADS
COMP_BLOCK = 32
COMP_STRIDE = 16
CMP_RATIO = COMP_BLOCK // COMP_STRIDE
CMP_HIDDEN = 2 * HEAD_DIM
SEL_BLOCK = 64
N_SEL = 16
WINDOW = 512
Q_BLOCK = 128
N_GROUPS = 8
EXPERTS_PER_GROUP = 8
N_EXPERTS = N_GROUPS * EXPERTS_PER_GROUP
TOP_K = 2
MOE_D_FF = D_MODEL // 4
MAX_ROW_BLOCK = 128
LN_EPS = 1e-5
F32 = jnp.float32

kernel_name = 'hybrid_conv_nsa_hmoe_deepnorm_step'


def layer_norm(x, g, b):
    xf = x.astype(F32)
    mu = jnp.mean(xf, axis=-1, keepdims=True)
    var = jnp.mean(jnp.square(xf - mu), axis=-1, keepdims=True)
    return ((xf - mu) * lax.rsqrt(var + LN_EPS) * g + b).astype(x.dtype)


def masked_softmax(s, mask):
    s = jnp.where(mask, s, -jnp.inf)
    m = jnp.max(s, axis=-1, keepdims=True)
    m = jnp.where(jnp.isfinite(m), m, 0.0)
    e = jnp.exp(s - m)
    d = jnp.sum(e, axis=-1, keepdims=True)
    return e / jnp.where(d > 0, d, 1.0)


def alibi_slopes():
    return jnp.exp2(-8.0 * jnp.arange(1, N_HEADS + 1, dtype=F32) / N_HEADS).reshape(N_KV, GQA)


def split_proj(p):
    B, T, _ = p.shape
    o1 = 2 * CONV_CH
    o2 = o1 + Q_WIDTH
    o3 = o2 + 6 * KV_WIDTH
    glu = p[..., :CONV_CH] * jax.nn.sigmoid(p[..., CONV_CH:o1])
    q = p[..., o1:o2].reshape(B, T, N_HEADS, HEAD_DIM)
    kv = p[..., o2:o3].reshape(B, T, 3, 2, N_KV, HEAD_DIM)
    g = jax.nn.sigmoid(p[..., o3:].astype(F32)).reshape(B, T, N_HEADS, 3)
    return glu, q, kv[:, :, 0], kv[:, :, 1], kv[:, :, 2], g


def conv_module(a, prev, w, b, g, beta):
    xpad = jnp.concatenate([prev.astype(a.dtype), a], axis=1)
    y = lax.conv_general_dilated(xpad, w[:, None, :].astype(a.dtype), (1,), 'VALID',
                                 dimension_numbers=('NWC', 'WIO', 'NWC'),
                                 feature_group_count=CONV_CH) + b
    y = jax.nn.silu(layer_norm(y, g, beta))
    return y, xpad[:, -(CONV_WIDTH - 1):]


def compress(rows, pos, w1, w2):
    B, L = rows.shape[:2]
    nc = (L - COMP_BLOCK) // COMP_STRIDE + 1
    nchunk = nc + CMP_RATIO - 1
    ch = rows[:, :nchunk * COMP_STRIDE].reshape(B, nchunk, COMP_STRIDE, N_KV, HEAD_DIM)
    pe = pos.reshape(CMP_RATIO, COMP_STRIDE, 1, HEAD_DIM)
    w1r = w1.reshape(CMP_RATIO, COMP_STRIDE, HEAD_DIM, CMP_HIDDEN)
    hid = 0
    for r in range(CMP_RATIO):
        hid = hid + jnp.einsum('bcshd,sde->bche', ch[:, r:r + nc] + pe[r], w1r[r])
    return jnp.einsum('bche,ed->bchd', jax.nn.gelu(hid), w2)


def nsa_context(kv_c, kv_s, pos_k, w1_k, w2_k, pos_v, w1_v, w2_v):
    kc = compress(kv_c[:, :, 0], pos_k, w1_k, w2_k)
    vc = compress(kv_c[:, :, 1], pos_v, w1_v, w2_v)
    nc = kc.shape[1]
    c_start = jnp.arange(nc, dtype=jnp.int32) * COMP_STRIDE
    c_end = c_start + COMP_BLOCK - 1
    B, L = kv_s.shape[:2]
    ns = -(-L // SEL_BLOCK)
    s_start = jnp.arange(ns, dtype=jnp.int32) * SEL_BLOCK
    smap = ((c_start[:, None] <= s_start[None, :] + SEL_BLOCK - 1) &
            (c_end[:, None] >= s_start[None, :])).astype(F32)
    rows = jnp.pad(kv_s, ((0, 0), (0, ns * SEL_BLOCK - L), (0, 0), (0, 0), (0, 0)))
    blocks = rows.reshape(B, ns, SEL_BLOCK, 2, N_KV, HEAD_DIM)
    ks = blocks[:, :, :, 0].transpose(0, 3, 1, 2, 4)
    vs = blocks[:, :, :, 1].transpose(0, 3, 1, 2, 4)
    return kc, vc, c_end, smap, ks, vs


def nsa_core(q, g, q_pos, kc, vc, c_end, smap, ks, vs, kw, vw, w_pos):
    B, Tq = q.shape[:2]
    qh = q.reshape(B, Tq, N_KV, GQA, HEAD_DIM)
    sl = alibi_slopes()
    scale = HEAD_DIM ** -0.5
    s_c = jnp.einsum('bqkgd,bckd->bkgqc', qh, kc, preferred_element_type=F32) * scale
    d_c = q_pos[:, None] - c_end[None, :]
    s_c = s_c - sl[None, :, :, None, None] * d_c.astype(F32)
    p_c = masked_softmax(s_c, d_c >= 0)
    o_c = jnp.einsum('bkgqc,bckd->bqkgd', p_c.astype(vc.dtype), vc, preferred_element_type=F32)
    ns = smap.shape[1]
    k_sel = min(N_SEL, ns)
    imp = jnp.einsum('bkqc,cn->bkqn', p_c.sum(axis=2), smap)
    blk = jnp.arange(ns, dtype=jnp.int32)[None, :]
    cur = (q_pos // SEL_BLOCK)[:, None]
    valid = blk * SEL_BLOCK <= q_pos[:, None]
    forced = (blk == 0) | (blk == cur) | (blk == cur - 1)
    score = jnp.where(valid & forced, jnp.inf, jnp.where(valid, imp, -jnp.inf))
    top_v, top_i = lax.top_k(score, k_sel)
    b_ix = jnp.arange(B)[:, None, None, None]
    h_ix = jnp.arange(N_KV)[None, :, None, None]
    k_g = ks[b_ix, h_ix, top_i]
    v_g = vs[b_ix, h_ix, top_i]
    s_s = jnp.einsum('bqkgd,bkqnsd->bkgqns', qh, k_g, preferred_element_type=F32) * scale
    k_pos = top_i[..., None] * SEL_BLOCK + jnp.arange(SEL_BLOCK, dtype=jnp.int32)
    d_s = q_pos[None, None, :, None, None] - k_pos
    m_s = (top_v > -jnp.inf)[..., None] & (d_s >= 0)
    s_s = s_s - sl[None, :, :, None, None, None] * d_s[:, :, None].astype(F32)
    n_keys = k_sel * SEL_BLOCK
    p_s = masked_softmax(s_s.reshape(B, N_KV, GQA, Tq, n_keys),
                         m_s[:, :, None].reshape(B, N_KV, 1, Tq, n_keys)).reshape(s_s.shape)
    o_s = jnp.einsum('bkgqns,bkqnsd->bqkgd', p_s.astype(v_g.dtype), v_g, preferred_element_type=F32)
    s_w = jnp.einsum('bqkgd,bwkd->bkgqw', qh, kw, preferred_element_type=F32) * scale
    d_w = q_pos[:, None] - w_pos[None, :]
    m_w = (d_w >= 0) & (d_w <= WINDOW) & (w_pos[None, :] >= 0)
    s_w = s_w - sl[None, :, :, None, None] * d_w.astype(F32)
    p_w = masked_softmax(s_w, m_w)
    o_w = jnp.einsum('bkgqw,bwkd->bqkgd', p_w.astype(vw.dtype), vw, preferred_element_type=F32)
    gh = g.reshape(B, Tq, N_KV, GQA, 3)
    o = gh[..., 0:1] * o_c + gh[..., 1:2] * o_s + gh[..., 2:3] * o_w
    return o.reshape(B, Tq, Q_WIDTH).astype(q.dtype)


def nsa_prompt(q, g, kv_c, kv_s, kv_w, cw):
    B, S = q.shape[:2]
    kc, vc, c_end, smap, ks, vs = nsa_context(kv_c, kv_s, *cw)
    kw_pad = jnp.pad(kv_w, ((0, 0), (WINDOW, 0), (0, 0), (0, 0), (0, 0)))

    def query_block(i):
        q0 = i * Q_BLOCK
        qb = lax.dynamic_slice_in_dim(q, q0, Q_BLOCK, axis=1)
        gb = lax.dynamic_slice_in_dim(g, q0, Q_BLOCK, axis=1)
        wb = lax.dynamic_slice_in_dim(kw_pad, q0, Q_BLOCK + WINDOW, axis=1)
        q_pos = q0 + jnp.arange(Q_BLOCK, dtype=jnp.int32)
        w_pos = q0 - WINDOW + jnp.arange(Q_BLOCK + WINDOW, dtype=jnp.int32)
        return nsa_core(qb, gb, q_pos, kc, vc, c_end, smap, ks, vs, wb[:, :, 0], wb[:, :, 1], w_pos)

    out = lax.map(query_block, jnp.arange(S // Q_BLOCK, dtype=jnp.int32))
    out = out.transpose(1, 0, 2, 3).reshape(B, S, Q_WIDTH)
    return out, kv_w[:, S - min(WINDOW, S):]


def gather_pages(cache, page_table):
    pages = cache[page_table]
    return pages.reshape(page_table.shape[0], page_table.shape[1] * PAGE_SIZE, 2, N_KV, HEAD_DIM)


def nsa_sample(q, g, kv_c, kv_s, kv_w, cache_c, cache_s, win_prev, page_table, cw):
    T = q.shape[1]
    past = page_table.shape[1] * PAGE_SIZE
    full_c = jnp.concatenate([gather_pages(cache_c, page_table).astype(kv_c.dtype), kv_c], axis=1)
    full_s = jnp.concatenate([gather_pages(cache_s, page_table).astype(kv_s.dtype), kv_s], axis=1)
    kc, vc, c_end, smap, ks, vs = nsa_context(full_c, full_s, *cw)
    wb = win_prev.shape[1]
    kw_all = jnp.concatenate([win_prev.astype(kv_w.dtype), kv_w], axis=1)
    q_pos = past + jnp.arange(T, dtype=jnp.int32)
    w_pos = past - wb + jnp.arange(wb + T, dtype=jnp.int32)
    out = nsa_core(q, g, q_pos, kc, vc, c_end, smap, ks, vs, kw_all[:, :, 0], kw_all[:, :, 1], w_pos)
    return out, kw_all[:, T:]


def expert_rows(t, eid, wts, w_gate, w_up, w_down):
    N, D = t.shape
    A = N * TOP_K
    blk = 8
    while blk * 2 <= min(MAX_ROW_BLOCK, A // N_EXPERTS):
        blk *= 2
    n_blocks = -(-(A + N_EXPERTS * (blk - 1)) // blk)
    R = n_blocks * blk
    fe = eid.reshape(-1)
    ft = jnp.repeat(jnp.arange(N, dtype=jnp.int32), TOP_K)
    fw = wts.reshape(-1)
    order = jnp.argsort(fe)
    se, st, sw = fe[order], ft[order], fw[order]
    counts = jnp.bincount(fe, length=N_EXPERTS)
    padded = (counts + blk - 1) // blk * blk
    pad_end = jnp.cumsum(padded)
    pad_start = pad_end - padded
    start = jnp.cumsum(counts) - counts
    dest = pad_start[se] + jnp.arange(A, dtype=jnp.int32) - start[se]
    row_tok = jnp.zeros((R,), jnp.int32).at[dest].set(st)
    row_w = jnp.zeros((R,), F32).at[dest].set(sw)
    blk_e = jnp.minimum(jnp.searchsorted(pad_end, jnp.arange(n_blocks) * blk, side='right'), N_EXPERTS - 1)

    def run(args):
        tok, e = args
        xb = t[tok]
        return (jax.nn.silu(xb @ w_gate[e]) * (xb @ w_up[e])) @ w_down[e]

    yr = lax.map(run, (row_tok.reshape(n_blocks, blk), blk_e)).reshape(R, D)
    yr = yr * row_w[:, None].astype(t.dtype)
    return jnp.zeros_like(t).at[row_tok].add(yr)


def hier_moe(h, w_grp, b_grp, w_exp, b_exp, w_gate, w_up, w_down):
    B, T, D = h.shape
    t = h.reshape(B * T, D)
    N = B * T
    g_logit = jnp.einsum('nd,dg->ng', t, w_grp, preferred_element_type=F32) + b_grp
    g_top, g_idx = lax.top_k(g_logit, 1)
    g_p = jnp.exp(g_top[:, 0] - jax.nn.logsumexp(g_logit, axis=-1))
    e_logit = (jnp.einsum('nd,de->ne', t, w_exp, preferred_element_type=F32) + b_exp)
    e_in = e_logit.reshape(N, N_GROUPS, EXPERTS_PER_GROUP)[jnp.arange(N), g_idx[:, 0]]
    e_top, e_idx = lax.top_k(e_in, TOP_K)
    wts = jax.nn.softmax(e_top, axis=-1) * g_p[:, None]
    eid = g_idx * EXPERTS_PER_GROUP + e_idx
    return expert_rows(t, eid, wts, w_gate, w_up, w_down).reshape(B, T, D)


def decoder_layer(h, conv_prev, nsa_fn, w_in, conv_w, conv_b, conv_ln_g, conv_ln_b, w_out,
                  ln1_g, ln1_b, w_grp, b_grp, w_exp, b_exp, w_gate, w_up, w_down, ln2_g, ln2_b):
    alpha = (2.0 * DEPTH) ** 0.25
    a, q, kv_c, kv_s, kv_w, g = split_proj(h @ w_in)
    conv_out, conv_state = conv_module(a, conv_prev, conv_w, conv_b, conv_ln_g, conv_ln_b)
    attn_out, win_state = nsa_fn(q, g, kv_c, kv_s, kv_w)
    mix = jnp.concatenate([conv_out, attn_out.astype(conv_out.dtype)], axis=-1) @ w_out
    h = layer_norm(alpha * h + mix, ln1_g, ln1_b)
    h = layer_norm(alpha * h + hier_moe(h, w_grp, b_grp, w_exp, b_exp, w_gate, w_up, w_down), ln2_g, ln2_b)
    return h, (kv_c, kv_s, win_state, conv_state)


def setup_inputs(seed: int = 0) -> dict:
    key = jax.random.key(seed)
    k = jax.random.split(key, 32)
    n_pages = PAST_LEN // PAGE_SIZE
    n_used = DEC_BATCH * n_pages
    n_phys = n_used + max(1, n_used // 4)
    win_buf = min(WINDOW, PAST_LEN)
    beta = (8.0 * DEPTH) ** -0.25

    def nrm(i, shape, scale):
        return jax.random.normal(k[i], shape, F32) * scale

    page_table = jax.random.permutation(k[6], n_phys)[:n_used].reshape(DEC_BATCH, n_pages).astype(jnp.int32)
    return {
        'x_prompt': nrm(0, (BATCH, SEQ, D_MODEL), 1.0),
        'x_sample': nrm(1, (DEC_BATCH, DEC_SEQ, D_MODEL), 1.0),
        'cache_cmp_kv': nrm(2, (DEPTH, n_phys, PAGE_SIZE, 2, N_KV, HEAD_DIM), 1.0),
        'cache_slc_kv': nrm(3, (DEPTH, n_phys, PAGE_SIZE, 2, N_KV, HEAD_DIM), 1.0),
        'state_win_kv': nrm(4, (DEPTH, DEC_BATCH, win_buf, 2, N_KV, HEAD_DIM), 1.0),
        'state_conv': nrm(5, (DEPTH, DEC_BATCH, CONV_WIDTH - 1, CONV_CH), 0.5),
        'page_table': page_table,
        'w_in': nrm(7, (DEPTH, D_MODEL, N_PROJ), D_MODEL ** -0.5),
        'conv_w': nrm(8, (DEPTH, CONV_WIDTH, CONV_CH), CONV_WIDTH ** -0.5),
        'conv_b': nrm(9, (DEPTH, CONV_CH), 0.02),
        'conv_ln_g': 1.0 + nrm(10, (DEPTH, CONV_CH), 0.02),
        'conv_ln_b': nrm(11, (DEPTH, CONV_CH), 0.02),
        'cmp_pos_k': nrm(12, (DEPTH, COMP_BLOCK, HEAD_DIM), 0.1),
        'cmp_w1_k': nrm(13, (DEPTH, COMP_BLOCK, HEAD_DIM, CMP_HIDDEN), (COMP_BLOCK * HEAD_DIM) ** -0.5),
        'cmp_w2_k': nrm(14, (DEPTH, CMP_HIDDEN, HEAD_DIM), CMP_HIDDEN ** -0.5),
        'cmp_pos_v': nrm(15, (DEPTH, COMP_BLOCK, HEAD_DIM), 0.1),
        'cmp_w1_v': nrm(16, (DEPTH, COMP_BLOCK, HEAD_DIM, CMP_HIDDEN), (COMP_BLOCK * HEAD_DIM) ** -0.5),
        'cmp_w2_v': nrm(17, (DEPTH, CMP_HIDDEN, HEAD_DIM), CMP_HIDDEN ** -0.5),
        'w_out': nrm(18, (DEPTH, MIX_WIDTH, D_MODEL), beta * MIX_WIDTH ** -0.5),
        'ln1_g': 1.0 + nrm(19, (DEPTH, D_MODEL), 0.02),
        'ln1_b': nrm(20, (DEPTH, D_MODEL), 0.02),
        'w_grp': nrm(21, (DEPTH, D_MODEL, N_GROUPS), D_MODEL ** -0.5),
        'b_grp': nrm(22, (DEPTH, N_GROUPS), 0.01),
        'w_exp': nrm(23, (DEPTH, D_MODEL, N_EXPERTS), D_MODEL ** -0.5),
        'b_exp': nrm(24, (DEPTH, N_EXPERTS), 0.01),
        'w_gate': nrm(25, (DEPTH, N_EXPERTS, D_MODEL, MOE_D_FF), D_MODEL ** -0.5),
        'w_up': nrm(26, (DEPTH, N_EXPERTS, D_MODEL, MOE_D_FF), D_MODEL ** -0.5),
        'w_down': nrm(27, (DEPTH, N_EXPERTS, MOE_D_FF, D_MODEL), beta * MOE_D_FF ** -0.5),
        'ln2_g': 1.0 + nrm(28, (DEPTH, D_MODEL), 0.02),
        'ln2_b': nrm(29, (DEPTH, D_MODEL), 0.02),
    }


def reference(x_prompt, x_sample, cache_cmp_kv, cache_slc_kv, state_win_kv, state_conv, page_table,
              w_in, conv_w, conv_b, conv_ln_g, conv_ln_b, cmp_pos_k, cmp_w1_k, cmp_w2_k,
              cmp_pos_v, cmp_w1_v, cmp_w2_v, w_out, ln1_g, ln1_b, w_grp, b_grp, w_exp, b_exp,
              w_gate, w_up, w_down, ln2_g, ln2_b):
    hp, hs = x_prompt, x_sample
    prompt_states, sample_states = [], []
    for layer in range(DEPTH):
        cw = (cmp_pos_k[layer], cmp_w1_k[layer], cmp_w2_k[layer],
              cmp_pos_v[layer], cmp_w1_v[layer], cmp_w2_v[layer])
        lw = (w_in[layer], conv_w[layer], conv_b[layer], conv_ln_g[layer], conv_ln_b[layer], w_out[layer],
              ln1_g[layer], ln1_b[layer], w_grp[layer], b_grp[layer], w_exp[layer], b_exp[layer],
              w_gate[layer], w_up[layer], w_down[layer], ln2_g[layer], ln2_b[layer])
        conv_zero = jnp.zeros((hp.shape[0], CONV_WIDTH - 1, CONV_CH), hp.dtype)
        hp, st_p = decoder_layer(hp, conv_zero, functools.partial(nsa_prompt, cw=cw), *lw)
        nsa_s = functools.partial(nsa_sample, cache_c=cache_cmp_kv[layer], cache_s=cache_slc_kv[layer],
                                  win_prev=state_win_kv[layer], page_table=page_table, cw=cw)
        hs, st_s = decoder_layer(hs, state_conv[layer], nsa_s, *lw)
        prompt_states.append(st_p)
        sample_states.append(st_s)
    p_cmp, p_slc, p_win, p_conv = [jnp.stack(s, axis=0) for s in zip(*prompt_states)]
    s_cmp, s_slc, s_win, s_conv = [jnp.stack(s, axis=0) for s in zip(*sample_states)]
    return (hp, hs, p_cmp, p_slc, p_win, p_conv, s_cmp, s_slc, s_win, s_conv)
```

```python
import functools

import numpy as np
import jax
import jax.numpy as jnp
from jax import lax
from jax.experimental import pallas as pl
from jax.experimental.pallas import tpu as pltpu

F32 = jnp.float32
BF16 = jnp.bfloat16

D_MODEL = 2048
SEQ = 8192
DEC_BATCH = 128
DEC_SEQ = 4
PAST_LEN = 2048
PAGE_SIZE = 128
CONV_CH = 1024
CONV_WIDTH = 31
N_HEADS = 16
HEAD_DIM = 64
N_KV = 4
GQA = 4
Q_WIDTH = 1024
KV_WIDTH = 256
COMP_BLOCK = 32
COMP_STRIDE = 16
CMP_HIDDEN = 128
SEL_BLOCK = 64
N_SEL = 16
WINDOW = 512
N_GROUPS = 8
EXPERTS_PER_GROUP = 8
N_EXPERTS = 64
TOP_K = 2
MOE_D_FF = 512
LN_EPS = 1e-5
ALPHA = 2.0 ** 0.25

N_PROMPT = SEQ
N_SAMPLE = DEC_BATCH * DEC_SEQ
N_TOK = N_PROMPT + N_SAMPLE

LANES = 128
VMEM_LIMIT = 48 * 1024 * 1024


def _cparams(*sem):
    return pltpu.CompilerParams(dimension_semantics=sem, vmem_limit_bytes=VMEM_LIMIT)


PROJ_TM = 512
PROJ_TN = 512
PROJ_STEPS = 9


def _proj_kernel(x_ref, w_ref, wg_ref, glu_ref, q_ref, kvc_ref, kvs_ref, kvw_ref, g_ref, kvsb_ref, kvwb_ref,
                 xb_ref):
    j = pl.program_id(1)

    @pl.when(j == 0)
    def _():
        xb_ref[...] = x_ref[...].astype(BF16)

    acc = jnp.dot(xb_ref[...], w_ref[...], preferred_element_type=F32)

    @pl.when(j < 4)
    def _():
        glu_ref[...] = acc[:, :256] * jax.nn.sigmoid(acc[:, 256:])

    @pl.when((j == 4) | (j == 5))
    def _():
        q_ref[...] = (acc * (HEAD_DIM ** -0.5)).astype(BF16)

    @pl.when(j == 6)
    def _():
        kvc_ref[...] = acc

    @pl.when(j == 7)
    def _():
        kvs_ref[...] = acc
        kvsb_ref[...] = acc.astype(BF16)

    @pl.when(j == 8)
    def _():
        kvw_ref[...] = acc
        kvwb_ref[...] = acc.astype(BF16)
        g_ref[...] = jax.nn.sigmoid(jnp.dot(xb_ref[...], wg_ref[...], preferred_element_type=F32))


def _proj(x, w_in):
    n = x.shape[0]
    o1 = 2 * CONV_CH
    o2 = o1 + Q_WIDTH
    o3 = o2 + 6 * KV_WIDTH
    wa, wb = w_in[:, :CONV_CH], w_in[:, CONV_CH:o1]
    glu_cols = [jnp.concatenate([wa[:, 256 * s:256 * (s + 1)], wb[:, 256 * s:256 * (s + 1)]], axis=1)
                for s in range(4)]
    w_cat = jnp.concatenate(glu_cols + [w_in[:, o1:o3]], axis=1).astype(BF16)
    wg = w_in[:, o3:].reshape(D_MODEL, N_HEADS, 3).transpose(0, 2, 1).reshape(D_MODEL, 3 * N_HEADS)
    wg = jnp.pad(wg, ((0, 0), (0, LANES - 3 * N_HEADS))).astype(BF16)
    grid = (n // PROJ_TM, PROJ_STEPS)
    return pl.pallas_call(
        _proj_kernel,
        grid=grid,
        in_specs=[
            pl.BlockSpec((PROJ_TM, D_MODEL), lambda i, j: (i, 0)),
            pl.BlockSpec((D_MODEL, PROJ_TN), lambda i, j: (0, j)),
            pl.BlockSpec((D_MODEL, LANES), lambda i, j: (0, 0)),
        ],
        out_specs=[
            pl.BlockSpec((PROJ_TM, 256), lambda i, j: (i, jnp.minimum(j, 3))),
            pl.BlockSpec((PROJ_TM, 512), lambda i, j: (i, jnp.clip(j - 4, 0, 1))),
            pl.BlockSpec((PROJ_TM, 512), lambda i, j: (i, 0)),
            pl.BlockSpec((PROJ_TM, 512), lambda i, j: (i, 0)),
            pl.BlockSpec((PROJ_TM, 512), lambda i, j: (i, 0)),
            pl.BlockSpec((PROJ_TM, LANES), lambda i, j: (i, 0)),
            pl.BlockSpec((PROJ_TM, 512), lambda i, j: (i, 0)),
            pl.BlockSpec((PROJ_TM, 512), lambda i, j: (i, 0)),
        ],
        out_shape=[
            jax.ShapeDtypeStruct((n, CONV_CH), F32),
            jax.ShapeDtypeStruct((n, Q_WIDTH), BF16),
            jax.ShapeDtypeStruct((n, 2 * KV_WIDTH), F32),
            jax.ShapeDtypeStruct((n, 2 * KV_WIDTH), F32),
            jax.ShapeDtypeStruct((n, 2 * KV_WIDTH), F32),
            jax.ShapeDtypeStruct((n, LANES), F32),
            jax.ShapeDtypeStruct((n, 2 * KV_WIDTH), BF16),
            jax.ShapeDtypeStruct((n, 2 * KV_WIDTH), BF16),
        ],
        scratch_shapes=[pltpu.VMEM((PROJ_TM, D_MODEL), BF16)],
        compiler_params=_cparams("parallel", "arbitrary"),
        name="proj",
    )(x, w_cat, wg)


CONV_T = 256
CONV_HALO = 32
CONV_RC = 64
CONV_CC = 256


def _ln_silu(y, g, beta):
    mu = jnp.mean(y, axis=-1, keepdims=True)
    yc = y - mu
    var = jnp.mean(yc * yc, axis=-1, keepdims=True)
    z = yc * lax.rsqrt(var + LN_EPS) * g + beta
    return z * jax.nn.sigmoid(z)


def _conv_prompt_kernel(cur_ref, prev_ref, w_ref, b_ref, g_ref, beta_ref, o_ref, xp_ref, y_ref):
    i = pl.program_id(0)
    xp_ref[CONV_HALO:, :] = cur_ref[...]

    @pl.when(i == 0)
    def _():
        xp_ref[0:CONV_HALO, :] = jnp.zeros((CONV_HALO, CONV_CH), F32)

    @pl.when(i > 0)
    def _():
        xp_ref[0:CONV_HALO, :] = prev_ref[...]

    off = CONV_HALO - (CONV_WIDTH - 1)
    for c in range(CONV_CH // CONV_CC):
        cols = slice(c * CONV_CC, (c + 1) * CONV_CC)
        for r in range(CONV_T // CONV_RC):
            acc = jnp.zeros((CONV_RC, CONV_CC), F32)
            for k in range(CONV_WIDTH):
                r0 = r * CONV_RC + off + k
                acc = acc + w_ref[k:k + 1, cols] * xp_ref[r0:r0 + CONV_RC, cols]
            y_ref[r * CONV_RC:(r + 1) * CONV_RC, cols] = acc
    y = y_ref[...] + b_ref[...]
    o_ref[...] = _ln_silu(y, g_ref[...], beta_ref[...]).astype(BF16)


def _conv_prompt(a, conv_w, conv_b, ln_g, ln_b):
    wp = jnp.pad(conv_w, ((0, 1), (0, 0)))
    row = lambda v: v.reshape(1, CONV_CH)
    hb = CONV_T // CONV_HALO
    return pl.pallas_call(
        _conv_prompt_kernel,
        grid=(N_PROMPT // CONV_T,),
        in_specs=[
            pl.BlockSpec((CONV_T, CONV_CH), lambda i: (i, 0)),
            pl.BlockSpec((CONV_HALO, CONV_CH), lambda i: (jnp.maximum(i * hb - 1, 0), 0)),
            pl.BlockSpec((CONV_WIDTH + 1, CONV_CH), lambda i: (0, 0)),
            pl.BlockSpec((1, CONV_CH), lambda i: (0, 0)),
            pl.BlockSpec((1, CONV_CH), lambda i: (0, 0)),
            pl.BlockSpec((1, CONV_CH), lambda i: (0, 0)),
        ],
        out_specs=pl.BlockSpec((CONV_T, CONV_CH), lambda i: (i, 0)),
        out_shape=jax.ShapeDtypeStruct((N_PROMPT, CONV_CH), BF16),
        scratch_shapes=[pltpu.VMEM((CONV_T + CONV_HALO, CONV_CH), F32), pltpu.VMEM((CONV_T, CONV_CH), F32)],
        compiler_params=_cparams("parallel"),
        name="conv_prompt",
    )(a, a, wp, row(conv_b), row(ln_g), row(ln_b))


def _conv_sample_kernel(xt_ref, w_ref, b_ref, g_ref, beta_ref, o_ref):
    for t in range(DEC_SEQ):
        acc = jnp.zeros((DEC_BATCH, CONV_CH), F32)
        for k in range(CONV_WIDTH):
            acc = acc + w_ref[k:k + 1, :] * xt_ref[t + k]
        y = acc + b_ref[...]
        o_ref[t] = _ln_silu(y, g_ref[...], beta_ref[...]).astype(BF16)


def _conv_sample(xt, conv_w, conv_b, ln_g, ln_b):
    wp = jnp.pad(conv_w, ((0, 1), (0, 0)))
    row = lambda v: v.reshape(1, CONV_CH)
    npos = CONV_WIDTH - 1 + DEC_SEQ
    return pl.pallas_call(
        _conv_sample_kernel,
        grid=(1,),
        in_specs=[
            pl.BlockSpec((npos, DEC_BATCH, CONV_CH), lambda i: (0, 0, 0)),
            pl.BlockSpec((CONV_WIDTH + 1, CONV_CH), lambda i: (0, 0)),
            pl.BlockSpec((1, CONV_CH), lambda i: (0, 0)),
            pl.BlockSpec((1, CONV_CH), lambda i: (0, 0)),
            pl.BlockSpec((1, CONV_CH), lambda i: (0, 0)),
        ],
        out_specs=pl.BlockSpec((DEC_SEQ, DEC_BATCH, CONV_CH), lambda i: (0, 0, 0)),
        out_shape=jax.ShapeDtypeStruct((DEC_SEQ, DEC_BATCH, CONV_CH), BF16),
        compiler_params=_cparams("arbitrary"),
        name="conv_sample",
    )(xt, wp, row(conv_b), row(ln_g), row(ln_b))


HALF = HEAD_DIM
NEG_BIG = -1e30
CHUNKS_PER_PAGE = PAGE_SIZE // COMP_STRIDE
CHUNK_LANES = COMP_STRIDE * 2 * KV_WIDTH


def _alibi_slope(h):
    return float(2.0 ** (-8.0 * (h + 1) / N_HEADS))


def _gelu_tanh(x):
    c = np.float32(np.sqrt(2.0 / np.pi))
    return 0.5 * x * (1.0 + jnp.tanh(c * (x + 0.044715 * (x * x * x))))


def _dot_nt(a, b, precision=None):
    return lax.dot_general(a, b, (((1,), (1,)), ((), ())), preferred_element_type=F32, precision=precision)


def _half_mask(shape, half):
    lane = lax.broadcasted_iota(jnp.int32, shape, len(shape) - 1)
    return (lane >= HALF) if half else (lane < HALF)


def _swap_halves(x):
    return pltpu.roll(x, HALF, axis=x.ndim - 1)


def _compress_weights(pos, w1, w2):
    eye2 = jnp.eye(2, dtype=F32)
    base = w1.reshape(2, COMP_STRIDE, HEAD_DIM, CMP_HIDDEN)
    w1p = jnp.einsum('rsde,hg->rshdge', base, eye2).reshape(2, COMP_STRIDE * 2 * HEAD_DIM, 2 * CMP_HIDDEN)
    pe = jnp.broadcast_to(pos.reshape(2, COMP_STRIDE, 1, HEAD_DIM), (2, COMP_STRIDE, 2, HEAD_DIM))
    pe = pe.reshape(2, 1, COMP_STRIDE * 2 * HEAD_DIM)
    w2p = jnp.einsum('ed,hg->hegd', w2, eye2).reshape(2 * CMP_HIDDEN, 2 * HEAD_DIM)
    return w1p.astype(BF16), pe, w2p.astype(BF16)


def _compress_pair(x, pe_ref, w1_ref, w2_ref):
    n = x.shape[0]
    a0 = jnp.dot((x + pe_ref[0]).astype(BF16), w1_ref[0], preferred_element_type=F32)
    a1 = jnp.dot((x + pe_ref[1]).astype(BF16), w1_ref[1], preferred_element_type=F32)
    hid = a0 + pltpu.roll(a1, n - 1, axis=0)
    return jnp.dot(_gelu_tanh(hid).astype(BF16), w2_ref[...], preferred_element_type=F32)


def _overlap_map(nc, ns, nc_pad, ns_pad):
    i = np.arange(nc_pad)[None, :]
    j = np.arange(ns_pad)[:, None]
    c_start, s_start = i * COMP_STRIDE, j * SEL_BLOCK
    m = (c_start <= s_start + SEL_BLOCK - 1) & (c_start + COMP_BLOCK - 1 >= s_start) & (i < nc) & (j < ns)
    return jnp.asarray(m.astype(np.float32))


def _select_blocks(imp_t, q_pos, n_iter):
    nb = imp_t.shape[0]
    blk = lax.broadcasted_iota(jnp.int32, imp_t.shape, 0)
    blk_f = blk.astype(F32)
    cur = q_pos >> 6
    valid = blk * SEL_BLOCK <= q_pos
    forced = (blk == 0) | (blk == cur) | (blk == cur - 1)
    score = jnp.where(valid, jnp.where(forced, 3e38, imp_t), -1.0)
    sel = jnp.zeros(imp_t.shape, F32)
    for _ in range(n_iter):
        mx = jnp.max(score, axis=0, keepdims=True)
        first = jnp.min(jnp.where(score == mx, blk_f, float(nb)), axis=0, keepdims=True)
        pick = (blk_f == first) & (mx >= 0.0)
        sel = jnp.where(pick, 1.0, sel)
        score = jnp.where(pick, -2.0, score)
    return sel


P_CHUNKS = SEQ // COMP_STRIDE
P_NC = (SEQ - COMP_BLOCK) // COMP_STRIDE + 1
P_NS = SEQ // SEL_BLOCK


def _compress_prompt_kernel(*refs):
    c_refs = refs[:COMP_STRIDE]
    pe_ref, w1_ref, w2_ref, o_ref = refs[COMP_STRIDE:]
    x = jnp.concatenate([r[...] for r in c_refs], axis=1)
    o_ref[...] = _compress_pair(x, pe_ref, w1_ref, w2_ref).astype(BF16)


def _compress_prompt(kvc, cw):
    pos_k, w1_k, w2_k, pos_v, w1_v, w2_v = cw
    wk, wv = _compress_weights(pos_k, w1_k, w2_k), _compress_weights(pos_v, w1_v, w2_v)
    w1p, pe, w2p = (jnp.stack([a, b]) for a, b in zip(wk, wv))
    chunks = kvc.reshape(N_TOK // COMP_STRIDE, CHUNK_LANES)
    c_specs = [pl.BlockSpec((P_CHUNKS, LANES), lambda g, s=s: (0, s * 4 + g)) for s in range(COMP_STRIDE)]
    return pl.pallas_call(
        _compress_prompt_kernel,
        grid=(4,),
        in_specs=c_specs + [
            pl.BlockSpec((None, 2, 1, 2048), lambda g: (g // 2, 0, 0, 0)),
            pl.BlockSpec((None, 2, 2048, 256), lambda g: (g // 2, 0, 0, 0)),
            pl.BlockSpec((None, 256, LANES), lambda g: (g // 2, 0, 0)),
        ],
        out_specs=pl.BlockSpec((P_CHUNKS, LANES), lambda g: (0, g)),
        out_shape=jax.ShapeDtypeStruct((P_CHUNKS, 4 * LANES), BF16),
        compiler_params=_cparams("parallel"),
        name="compress_prompt",
    )(*([chunks] * COMP_STRIDE), pe, w1p, w2p)


QB = 256
TK = 256


def _stack_gqa(q_ref, k):
    qa = q_ref[:, 256 * k:256 * k + LANES]
    qb = q_ref[:, 256 * k + LANES:256 * (k + 1)]
    qas = _swap_halves(qa.astype(F32)).astype(BF16)
    qbs = _swap_halves(qb.astype(F32)).astype(BF16)
    order = [qa, qas, qb, qbs] if k % 2 == 0 else [qas, qa, qbs, qb]
    return jnp.concatenate(order, axis=0)


def _pair_up(o0, o1, half):
    return o0 + _swap_halves(o1) if half == 0 else _swap_halves(o0) + o1


def _cmp_prompt_kernel(q_ref, kcv_ref, smap_ref, oc_ref, sel_ref):
    i = pl.program_id(0)
    q0 = i * QB
    q_pos_col = q0 + lax.broadcasted_iota(jnp.int32, (QB, 1), 0)
    c_end = lax.broadcasted_iota(jnp.int32, (1, P_CHUNKS), 1) * COMP_STRIDE + (COMP_BLOCK - 1)
    d_c = q_pos_col - c_end
    ok = d_c >= 0
    d_cf = d_c.astype(F32)
    q_pos_row = q0 + lax.broadcasted_iota(jnp.int32, (1, QB), 1)
    for k in range(N_KV):
        pair, half = k // 2, k % 2
        keep = _half_mask((P_CHUNKS, LANES), half)
        kc = jnp.where(keep, kcv_ref[:, LANES * pair:LANES * (pair + 1)], 0)
        vc = jnp.where(keep, kcv_ref[:, 2 * LANES + LANES * pair:2 * LANES + LANES * (pair + 1)], 0)
        s4 = _dot_nt(_stack_gqa(q_ref, k), kc)
        psum = jnp.zeros((QB, P_CHUNKS), F32)
        outs = []
        for g in range(GQA):
            s = s4[g * QB:(g + 1) * QB] - _alibi_slope(4 * k + g) * d_cf
            s = jnp.where(ok, s, NEG_BIG)
            m = jnp.max(s, axis=-1, keepdims=True)
            e = jnp.where(ok, jnp.exp(s - m), 0.0)
            den = jnp.sum(e, axis=-1, keepdims=True)
            p = e / jnp.where(den > 0, den, 1.0)
            psum = psum + p
            outs.append(jnp.dot(p.astype(BF16), vc, preferred_element_type=F32))
        oc_ref[:, 256 * k:256 * k + LANES] = _pair_up(outs[0], outs[1], half)
        oc_ref[:, 256 * k + LANES:256 * (k + 1)] = _pair_up(outs[2], outs[3], half)
        imp_t = _dot_nt(smap_ref[...], psum, precision=lax.Precision.HIGHEST)
        sel_t = _select_blocks(imp_t, q_pos_row, N_SEL)
        sel_ref[:, LANES * k:LANES * (k + 1)] = sel_t.T.astype(BF16)


def _cmp_prompt(q, kcv):
    smap_t = _overlap_map(P_NC, P_NS, P_CHUNKS, P_NS)
    return pl.pallas_call(
        _cmp_prompt_kernel,
        grid=(N_PROMPT // QB,),
        in_specs=[
            pl.BlockSpec((QB, Q_WIDTH), lambda i: (i, 0)),
            pl.BlockSpec((P_CHUNKS, 4 * LANES), lambda i: (0, 0)),
            pl.BlockSpec((P_NS, P_CHUNKS), lambda i: (0, 0)),
        ],
        out_specs=[
            pl.BlockSpec((QB, Q_WIDTH), lambda i: (i, 0)),
            pl.BlockSpec((QB, N_KV * P_NS), lambda i: (i, 0)),
        ],
        out_shape=[
            jax.ShapeDtypeStruct((N_PROMPT, Q_WIDTH), F32),
            jax.ShapeDtypeStruct((N_PROMPT, N_KV * P_NS), BF16),
        ],
        compiler_params=_cparams("parallel"),
        name="nsa_cmp_prompt",
    )(q, kcv, smap_t)


def _softmax_tile(s4, bias_fn, k, m_ref, l_ref, acc_ref, v_tile, first):
    ps = []
    for g in range(GQA):
        rows = slice(g * QB, (g + 1) * QB)
        s = s4[rows] + bias_fn(g)
        m_tile = jnp.max(s, axis=-1, keepdims=True)
        if first:
            m_new = m_tile
        else:
            m_old = m_ref[k, rows]
            m_new = jnp.maximum(m_old, m_tile)
        p = jnp.exp(s - m_new)
        row_sum = jnp.sum(p, axis=-1, keepdims=True)
        if first:
            l_ref[k, rows] = row_sum
        else:
            scale_old = jnp.exp(m_old - m_new)
            l_ref[k, rows] = scale_old * l_ref[k, rows] + row_sum
            acc_ref[k, rows] = scale_old * acc_ref[k, rows]
        m_ref[k, rows] = m_new
        ps.append(p.astype(BF16))
    pv = jnp.dot(jnp.concatenate(ps, axis=0), v_tile, preferred_element_type=F32)
    if first:
        acc_ref[k] = pv
    else:
        acc_ref[k] = acc_ref[k] + pv


def _kv_tile(kv_ref, row0, k):
    pair, half = k // 2, k % 2
    keep = _half_mask((TK, LANES), half)
    kt = jnp.where(keep, kv_ref[pl.ds(row0, TK), LANES * pair:LANES * (pair + 1)], 0)
    vt = jnp.where(keep, kv_ref[pl.ds(row0, TK), 2 * LANES + LANES * pair:2 * LANES + LANES * (pair + 1)], 0)
    return kt, vt


def _sel_prompt_kernel(q_ref, g_ref, oc_ref, sel_ref, ks_ref, kw_ref, o_ref,
                       q4_ref, m_ref, l_ref, acc_ref, os_ref):
    i = pl.program_id(0)
    q0 = pl.multiple_of(i * QB, QB)
    for k in range(N_KV):
        q4_ref[k] = _stack_gqa(q_ref, k)
    q_pos = q0 + lax.broadcasted_iota(jnp.int32, (QB, 1), 0)
    lane_key = lax.broadcasted_iota(jnp.int32, (1, TK), 1)
    blk_row = lax.broadcasted_iota(jnp.int32, (P_NS, 1), 0)

    def dist(row0):
        return (q_pos - (row0 + lane_key)).astype(F32)

    def sel_mask(row0, k):
        expand = jnp.where(blk_row == (row0 >> 6) + (lane_key >> 6), 1.0, 0.0).astype(BF16)
        return jnp.dot(sel_ref[:, LANES * k:LANES * (k + 1)], expand, preferred_element_type=F32)

    def selected_tile(row0, first):
        d = dist(row0)
        causal_bias = jnp.where(d >= 0, 0.0, NEG_BIG) if first else None
        for k in range(N_KV):
            kt, vt = _kv_tile(ks_ref, row0, k)
            s4 = _dot_nt(q4_ref[k], kt)
            bias = (sel_mask(row0, k) - 1.0) * (-NEG_BIG)
            if first:
                bias = bias + causal_bias
            _softmax_tile(s4, lambda g: bias - _alibi_slope(4 * k + g) * d, k, m_ref, l_ref, acc_ref, vt, first)

    selected_tile(q0, True)

    def body(t, carry):
        selected_tile(pl.multiple_of(t * TK, TK), False)
        return carry

    lax.fori_loop(0, i, body, 0)
    for k in range(N_KV):
        os_ref[k] = acc_ref[k] / l_ref[k]

    def window_tile(row0, first, band):
        d = dist(row0)
        if first:
            mask_bias = jnp.where(d >= 0, 0.0, NEG_BIG)
        elif band:
            mask_bias = jnp.where(d <= WINDOW, 0.0, NEG_BIG)
        else:
            mask_bias = jnp.zeros_like(d)
        for k in range(N_KV):
            kt, vt = _kv_tile(kw_ref, row0, k)
            s4 = _dot_nt(q4_ref[k], kt)
            _softmax_tile(s4, lambda g: mask_bias - _alibi_slope(4 * k + g) * d, k, m_ref, l_ref, acc_ref, vt,
                          first)

    window_tile(q0, True, False)

    @pl.when(i >= 1)
    def _():
        window_tile(pl.multiple_of(q0 - TK, TK), False, False)

    @pl.when(i >= 2)
    def _():
        window_tile(pl.multiple_of(q0 - 2 * TK, TK), False, True)

    low = _half_mask((QB, LANES), 0)
    for k in range(N_KV):
        half = k % 2
        contrib = []
        for g in range(GQA):
            h = 4 * k + g
            rows = slice(g * QB, (g + 1) * QB)
            o_s = os_ref[k, rows] * g_ref[:, N_HEADS + h:N_HEADS + h + 1]
            o_w = (acc_ref[k, rows] / l_ref[k, rows]) * g_ref[:, 2 * N_HEADS + h:2 * N_HEADS + h + 1]
            contrib.append(o_s + o_w)
        for pr in range(2):
            h0 = 4 * k + 2 * pr
            gate_c = jnp.where(low, g_ref[:, h0:h0 + 1], g_ref[:, h0 + 1:h0 + 2])
            cols = slice(256 * k + LANES * pr, 256 * k + LANES * (pr + 1))
            o = gate_c * oc_ref[:, cols] + _pair_up(contrib[2 * pr], contrib[2 * pr + 1], half)
            o_ref[:, cols] = o.astype(BF16)


def _sel_prompt(q, gates, o_c, sel, kvs_b, kvw_b):
    resident = lambda shape: pl.BlockSpec(shape, lambda i: (0, 0), pipeline_mode=pl.Buffered(1))
    return pl.pallas_call(
        _sel_prompt_kernel,
        grid=(N_PROMPT // QB,),
        in_specs=[
            pl.BlockSpec((QB, Q_WIDTH), lambda i: (i, 0)),
            pl.BlockSpec((QB, LANES), lambda i: (i, 0)),
            pl.BlockSpec((QB, Q_WIDTH), lambda i: (i, 0)),
            pl.BlockSpec((QB, N_KV * P_NS), lambda i: (i, 0)),
            resident((N_PROMPT, 2 * KV_WIDTH)),
            resident((N_PROMPT, 2 * KV_WIDTH)),
        ],
        out_specs=pl.BlockSpec((QB, Q_WIDTH), lambda i: (i, 0)),
        out_shape=jax.ShapeDtypeStruct((N_PROMPT, Q_WIDTH), BF16),
        scratch_shapes=[
            pltpu.VMEM((N_KV, GQA * QB, LANES), BF16),
            pltpu.VMEM((N_KV, GQA * QB, 1), F32),
            pltpu.VMEM((N_KV, GQA * QB, 1), F32),
            pltpu.VMEM((N_KV, GQA * QB, LANES), F32),
            pltpu.VMEM((N_KV, GQA * QB, LANES), F32),
        ],
        compiler_params=_cparams("parallel"),
        name="nsa_sel_prompt",
    )(q, gates, o_c, sel, kvs_b, kvw_b)


N_PAGES = PAST_LEN // PAGE_SIZE
S_CHUNKS = PAST_LEN // COMP_STRIDE
S_NC = (PAST_LEN + DEC_SEQ - COMP_BLOCK) // COMP_STRIDE + 1
S_NS = -(-(PAST_LEN + DEC_SEQ) // SEL_BLOCK)
S_KEYS = 17 * LANES
S_NEW = 16
WIN_BUF = min(WINDOW, PAST_LEN)
S_WKEYS = 5 * LANES
ROWS16 = GQA * DEC_SEQ


def _row_softmax(s, ok):
    s = jnp.where(ok, s, NEG_BIG)
    m = jnp.max(s, axis=-1, keepdims=True)
    e = jnp.where(ok, jnp.exp(s - m), 0.0)
    den = jnp.sum(e, axis=-1, keepdims=True)
    return e / jnp.where(den > 0, den, 1.0)


def _nsa_sample_kernel(pt_ref, *refs):
    cp = refs[:N_PAGES]
    sp = refs[N_PAGES:2 * N_PAGES]
    (win_ref, news_ref, neww_ref, q_ref, g_ref, pe_ref, w1_ref, w2_ref, smap_ref, expand_ref,
     o_ref, ks_ref, kw_ref) = refs[2 * N_PAGES:]
    del pt_ref

    for p in range(N_PAGES):
        ks_ref[PAGE_SIZE * p:PAGE_SIZE * (p + 1), :] = sp[p][...].astype(BF16)
    ks_ref[PAST_LEN:PAST_LEN + S_NEW, :] = news_ref[...].astype(BF16)
    ks_ref[PAST_LEN + S_NEW:, :] = jnp.zeros((S_KEYS - PAST_LEN - S_NEW, 2 * KV_WIDTH), BF16)
    kw_ref[0:WIN_BUF, :] = win_ref[...].astype(BF16)
    kw_ref[WIN_BUF:WIN_BUF + S_NEW, :] = neww_ref[...].astype(BF16)
    kw_ref[WIN_BUF + S_NEW:, :] = jnp.zeros((S_WKEYS - WIN_BUF - S_NEW, 2 * KV_WIDTH), BF16)

    kcv = []
    for kv in range(2):
        for pair in range(2):
            base = kv * KV_WIDTH + pair * LANES
            x = jnp.concatenate(
                [jnp.concatenate([cp[p][:, s * 2 * KV_WIDTH + base:s * 2 * KV_WIDTH + base + LANES]
                                  for p in range(N_PAGES)], axis=0) for s in range(COMP_STRIDE)], axis=1)
            kcv.append(_compress_pair(x, pe_ref.at[kv], w1_ref.at[kv], w2_ref.at[kv]).astype(BF16))

    row = lax.broadcasted_iota(jnp.int32, (ROWS16, 1), 0)
    q_pos = PAST_LEN + (row & (DEC_SEQ - 1))
    grp = row >> 2

    def slopes(k):
        s = jnp.full((ROWS16, 1), _alibi_slope(4 * k), F32)
        for g in range(1, GQA):
            s = jnp.where(grp == g, _alibi_slope(4 * k + g), s)
        return s

    c_end = lax.broadcasted_iota(jnp.int32, (1, S_CHUNKS), 1) * COMP_STRIDE + (COMP_BLOCK - 1)
    d_c = q_pos - c_end
    o_cmp, psums = [], []
    for k in range(N_KV):
        s = _dot_nt(q_ref[k], kcv[k // 2]) - slopes(k) * d_c.astype(F32)
        p = _row_softmax(s, d_c >= 0)
        o_cmp.append(jnp.dot(p.astype(BF16), kcv[2 + k // 2], preferred_element_type=F32))
        ps = p
        for g in range(1, GQA):
            ps = ps + pltpu.roll(p, DEC_SEQ * g, axis=0)
        psums.append(ps)
    psum_all = jnp.concatenate(psums + [jnp.zeros((LANES - N_KV * ROWS16, S_CHUNKS), F32)], axis=0)
    imp_t = _dot_nt(smap_ref[...], psum_all, precision=lax.Precision.HIGHEST)
    col = lax.broadcasted_iota(jnp.int32, (1, LANES), 1)
    sel_t = _select_blocks(imp_t, PAST_LEN + (col & (DEC_SEQ - 1)), N_SEL)
    sel = sel_t.T.astype(BF16)

    key_s = lax.broadcasted_iota(jnp.int32, (1, S_KEYS), 1)
    d_s = q_pos - key_s
    idx_w = lax.broadcasted_iota(jnp.int32, (1, S_WKEYS), 1)
    d_w = q_pos - (PAST_LEN - WIN_BUF + idx_w)
    ok_w = (d_w >= 0) & (d_w <= WINDOW)
    for k in range(N_KV):
        pair = k // 2
        kcol = slice(LANES * pair, LANES * (pair + 1))
        vcol = slice(2 * LANES + LANES * pair, 2 * LANES + LANES * (pair + 1))
        chosen = jnp.dot(sel[ROWS16 * k:ROWS16 * (k + 1)], expand_ref[...], preferred_element_type=F32)
        s = _dot_nt(q_ref[k], ks_ref[:, kcol]) - slopes(k) * d_s.astype(F32)
        p = _row_softmax(s, (chosen > 0.5) & (d_s >= 0))
        o_sel = jnp.dot(p.astype(BF16), ks_ref[:, vcol], preferred_element_type=F32)
        s = _dot_nt(q_ref[k], kw_ref[:, kcol]) - slopes(k) * d_w.astype(F32)
        p = _row_softmax(s, ok_w)
        o_win = jnp.dot(p.astype(BF16), kw_ref[:, vcol], preferred_element_type=F32)
        o_ref[k] = (g_ref[k][:, 0:1] * o_cmp[k] + g_ref[k][:, 1:2] * o_sel + g_ref[k][:, 2:3] * o_win)


def _nsa_sample(page_table, cache_c, cache_s, win_prev, q_s, gates_s, kvs_new, kvw_new, cw):
    pos_k, w1_k, w2_k, pos_v, w1_v, w2_v = cw
    wk, wv = _compress_weights(pos_k, w1_k, w2_k), _compress_weights(pos_v, w1_v, w2_v)
    w1p, pe, w2p = (jnp.stack([a, b]) for a, b in zip(wk, wv))
    n_phys = cache_c.shape[0]
    pages_c = cache_c.reshape(n_phys, CHUNKS_PER_PAGE, CHUNK_LANES)
    pages_s = cache_s.reshape(n_phys, PAGE_SIZE, 2 * KV_WIDTH)
    win = win_prev.reshape(DEC_BATCH, WIN_BUF, 2 * KV_WIDTH)
    half_of_k = (np.arange(N_KV) % 2)[:, None] == np.arange(2)[None, :]
    half_sel = jnp.asarray(half_of_k.reshape(1, N_KV, 1, 2, 1))
    qh = q_s.reshape(DEC_SEQ, DEC_BATCH, N_KV, GQA, HEAD_DIM).transpose(1, 2, 3, 0, 4)
    qh = qh.reshape(DEC_BATCH, N_KV, ROWS16, 1, HEAD_DIM)
    q16 = jnp.where(half_sel, qh, jnp.zeros((), BF16)).reshape(DEC_BATCH, N_KV, ROWS16, LANES)
    gh = gates_s[:, :3 * N_HEADS].reshape(DEC_SEQ, DEC_BATCH, 3, N_KV, GQA).transpose(1, 3, 4, 0, 2)
    g16 = jnp.pad(gh.reshape(DEC_BATCH, N_KV, ROWS16, 3), ((0, 0), (0, 0), (0, 0), (0, LANES - 3)))

    def new_rows(a):
        a = a.reshape(DEC_SEQ, DEC_BATCH, 2 * KV_WIDTH).transpose(1, 0, 2)
        return jnp.pad(a, ((0, 0), (0, S_NEW - DEC_SEQ), (0, 0)))

    smap_t = _overlap_map(S_NC, S_NS, S_CHUNKS, LANES)
    expand = jnp.asarray((np.arange(LANES)[:, None] == (np.arange(S_KEYS)[None, :] // SEL_BLOCK)), BF16)
    pt = page_table.reshape(-1).astype(jnp.int32)
    c_specs = [pl.BlockSpec((None, CHUNKS_PER_PAGE, CHUNK_LANES), lambda b, pt, p=p: (pt[b * N_PAGES + p], 0, 0))
               for p in range(N_PAGES)]
    s_specs = [pl.BlockSpec((None, PAGE_SIZE, 2 * KV_WIDTH), lambda b, pt, p=p: (pt[b * N_PAGES + p], 0, 0))
               for p in range(N_PAGES)]
    per_b3 = lambda shape: pl.BlockSpec((None,) + shape, lambda b, pt: (b, 0, 0))
    per_b4 = lambda shape: pl.BlockSpec((None,) + shape, lambda b, pt: (b, 0, 0, 0))
    const = lambda shape: pl.BlockSpec(shape, lambda b, pt: (0,) * len(shape))
    o16 = pl.pallas_call(
        _nsa_sample_kernel,
        grid_spec=pltpu.PrefetchScalarGridSpec(
            num_scalar_prefetch=1,
            grid=(DEC_BATCH,),
            in_specs=c_specs + s_specs + [
                per_b3((WIN_BUF, 2 * KV_WIDTH)),
                per_b3((S_NEW, 2 * KV_WIDTH)),
                per_b3((S_NEW, 2 * KV_WIDTH)),
                per_b4((N_KV, ROWS16, LANES)),
                per_b4((N_KV, ROWS16, LANES)),
                const((2, 2, 1, 2048)),
                const((2, 2, 2048, 256)),
                const((2, 256, LANES)),
                const((LANES, S_CHUNKS)),
                const((LANES, S_KEYS)),
            ],
            out_specs=per_b4((N_KV, ROWS16, LANES)),
            scratch_shapes=[pltpu.VMEM((S_KEYS, 2 * KV_WIDTH), BF16), pltpu.VMEM((S_WKEYS, 2 * KV_WIDTH), BF16)],
        ),
        out_shape=jax.ShapeDtypeStruct((DEC_BATCH, N_KV, ROWS16, LANES), F32),
        compiler_params=_cparams("parallel"),
        name="nsa_sample",
    )(pt, *([pages_c] * N_PAGES), *([pages_s] * N_PAGES), win, new_rows(kvs_new), new_rows(kvw_new),
      q16, g16, pe, w1p, w2p, smap_t, expand)
    o = o16.reshape(DEC_BATCH, N_KV, GQA, DEC_SEQ, 2, HEAD_DIM)
    o = jnp.stack([o[:, k, :, :, k % 2, :] for k in range(N_KV)], axis=1)
    return o.transpose(3, 0, 1, 2, 4).reshape(N_SAMPLE, Q_WIDTH).astype(BF16)


OUT_TM = 512
ROUTE_E0, ROUTE_E1, ROUTE_W0, ROUTE_W1, ROUTE_R0, ROUTE_R1 = range(6)


def _layer_norm(x, g, b):
    mu = jnp.mean(x, axis=-1, keepdims=True)
    xc = x - mu
    var = jnp.mean(xc * xc, axis=-1, keepdims=True)
    return xc * lax.rsqrt(var + LN_EPS) * g + b


def _first_max(vals, lane_f):
    mx = jnp.max(vals, axis=-1, keepdims=True)
    idx = jnp.min(jnp.where(vals == mx, lane_f, float(LANES)), axis=-1, keepdims=True)
    return mx, idx


def _out_route_kernel(conv_ref, attn_ref, x_ref, wo_ref, g1_ref, b1_ref, wr_ref, br_ref,
                      h_ref, route_ref, counts_ref, run_ref):
    i = pl.program_id(0)

    @pl.when(i == 0)
    def _():
        run_ref[...] = jnp.zeros_like(run_ref)

    mix = jnp.dot(conv_ref[...], wo_ref[0:CONV_CH, :], preferred_element_type=F32)
    mix = mix + jnp.dot(attn_ref[...], wo_ref[CONV_CH:, :], preferred_element_type=F32)
    h = _layer_norm(ALPHA * x_ref[...] + mix, g1_ref[...], b1_ref[...])
    h_ref[...] = h

    logit = jnp.dot(h, wr_ref[...], preferred_element_type=F32, precision=lax.Precision.HIGHEST) + br_ref[...]
    lane = lax.broadcasted_iota(jnp.int32, logit.shape, 1)
    lane_f = lane.astype(F32)
    is_grp = (lane >= N_EXPERTS) & (lane < N_EXPERTS + N_GROUPS)
    g_logit = jnp.where(is_grp, logit, NEG_BIG)
    g_top, g_lane = _first_max(g_logit, lane_f)
    lse = g_top + jnp.log(jnp.sum(jnp.where(is_grp, jnp.exp(g_logit - g_top), 0.0), axis=-1, keepdims=True))
    g_p = jnp.exp(g_top - lse)
    g_idx = g_lane.astype(jnp.int32) - N_EXPERTS
    in_grp = (lane >> 3) == g_idx
    e_logit = jnp.where(in_grp, logit, NEG_BIG)
    e0, i0 = _first_max(e_logit, lane_f)
    e1, i1 = _first_max(jnp.where(lane_f == i0, NEG_BIG, e_logit), lane_f)
    z1 = jnp.exp(e1 - e0)
    w0 = g_p / (1.0 + z1)
    w1 = g_p * z1 / (1.0 + z1)

    hit = jnp.where((lane_f == i0) | (lane_f == i1), 1.0, 0.0)
    r = lax.broadcasted_iota(jnp.int32, (OUT_TM, OUT_TM), 0)
    c = lax.broadcasted_iota(jnp.int32, (OUT_TM, OUT_TM), 1)
    lower = jnp.where(c < r, 1.0, 0.0).astype(BF16)
    before = jnp.dot(lower, hit.astype(BF16), preferred_element_type=F32) + run_ref[...]
    r0 = jnp.sum(jnp.where(lane_f == i0, before, 0.0), axis=-1, keepdims=True)
    r1 = jnp.sum(jnp.where(lane_f == i1, before, 0.0), axis=-1, keepdims=True)
    run_ref[...] = run_ref[...] + jnp.sum(hit, axis=0, keepdims=True)
    counts_ref[...] = jnp.broadcast_to(run_ref[...], counts_ref.shape)

    rec = jnp.zeros(logit.shape, F32)
    for col_id, v in ((ROUTE_E0, i0), (ROUTE_E1, i1), (ROUTE_W0, w0), (ROUTE_W1, w1), (ROUTE_R0, r0), (ROUTE_R1, r1)):
        rec = jnp.where(lane == col_id, v, rec)
    route_ref[...] = rec


def _out_route(conv, attn, x, w_out, ln_g, ln_b, w_grp, b_grp, w_exp, b_exp):
    n = x.shape[0]
    wr = jnp.pad(jnp.concatenate([w_exp, w_grp], axis=1), ((0, 0), (0, LANES - N_EXPERTS - N_GROUPS)))
    br = jnp.pad(jnp.concatenate([b_exp, b_grp]), (0, LANES - N_EXPERTS - N_GROUPS)).reshape(1, LANES)
    const = lambda shape: pl.BlockSpec(shape, lambda i: (0, 0))
    return pl.pallas_call(
        _out_route_kernel,
        grid=(n // OUT_TM,),
        in_specs=[
            pl.BlockSpec((OUT_TM, CONV_CH), lambda i: (i, 0)),
            pl.BlockSpec((OUT_TM, Q_WIDTH), lambda i: (i, 0)),
            pl.BlockSpec((OUT_TM, D_MODEL), lambda i: (i, 0)),
            pl.BlockSpec((D_MODEL, D_MODEL), lambda i: (0, 0), pipeline_mode=pl.Buffered(1)),
            const((1, D_MODEL)),
            const((1, D_MODEL)),
            const((D_MODEL, LANES)),
            const((1, LANES)),
        ],
        out_specs=[
            pl.BlockSpec((OUT_TM, D_MODEL), lambda i: (i, 0)),
            pl.BlockSpec((OUT_TM, LANES), lambda i: (i, 0)),
            const((8, LANES)),
        ],
        out_shape=[
            jax.ShapeDtypeStruct((n, D_MODEL), F32),
            jax.ShapeDtypeStruct((n, LANES), F32),
            jax.ShapeDtypeStruct((8, LANES), F32),
        ],
        scratch_shapes=[pltpu.VMEM((1, LANES), F32)],
        compiler_params=_cparams("arbitrary"),
        name="out_ln1_route",
    )(conv, attn, x, w_out.astype(BF16), ln_g.reshape(1, D_MODEL), ln_b.reshape(1, D_MODEL), wr, br)


MOE_BLK = 128
N_ASSIGN = N_TOK * TOP_K
MOE_BLOCKS = -(-(N_ASSIGN + N_EXPERTS * (MOE_BLK - 1)) // MOE_BLK)
MOE_ROWS = MOE_BLOCKS * MOE_BLK
CMB_TM = 128


def _gather_rows(src_hbm, dst_ref, sem, idx_ref, base, n_rows):
    def body(r, carry):
        pltpu.make_async_copy(src_hbm.at[pl.ds(idx_ref[base + r], 1)], dst_ref.at[pl.ds(r, 1)], sem).start()
        return carry
    lax.fori_loop(0, n_rows, body, 0)


def _wait_rows(src_hbm, dst_ref, sem):
    pltpu.make_async_copy(src_hbm.at[pl.ds(0, dst_ref.shape[0])], dst_ref, sem).wait()


def _moe_kernel(blk_e_ref, row_tok_ref, nvalid_ref, h_hbm, wg_ref, wu_ref, wd_ref, y_ref,
                xbuf, sems, wg_b, wu_b, wd_b):
    i = pl.program_id(0)
    slot = i % 2
    nvalid = nvalid_ref[0]

    @pl.when(i == 0)
    def _():
        _gather_rows(h_hbm, xbuf.at[0], sems.at[0], row_tok_ref, 0, MOE_BLK)

    @pl.when(i + 1 < nvalid)
    def _():
        _gather_rows(h_hbm, xbuf.at[1 - slot], sems.at[1 - slot], row_tok_ref, (i + 1) * MOE_BLK, MOE_BLK)

    @pl.when((i == 0) | (blk_e_ref[i] != blk_e_ref[jnp.maximum(i - 1, 0)]))
    def _():
        wg_b[...] = wg_ref[...].astype(BF16)
        wu_b[...] = wu_ref[...].astype(BF16)
        wd_b[...] = wd_ref[...].astype(BF16)

    @pl.when(i < jnp.maximum(nvalid, 1))
    def _():
        _wait_rows(h_hbm, xbuf.at[slot], sems.at[slot])

    @pl.when(i < nvalid)
    def _():
        x = xbuf[slot].astype(BF16)
        hg = jnp.dot(x, wg_b[...], preferred_element_type=F32)
        hu = jnp.dot(x, wu_b[...], preferred_element_type=F32)
        act = (hg * jax.nn.sigmoid(hg) * hu).astype(BF16)
        y_ref[...] = jnp.dot(act, wd_b[...], preferred_element_type=F32)

    @pl.when(i >= nvalid)
    def _():
        y_ref[...] = jnp.zeros_like(y_ref)


def _moe(h, blk_e, row_tok, nvalid, w_gate, w_up, w_down):
    wspec = lambda shape: pl.BlockSpec((None,) + shape, lambda i, be, rt, nv: (be[i], 0, 0))
    return pl.pallas_call(
        _moe_kernel,
        grid_spec=pltpu.PrefetchScalarGridSpec(
            num_scalar_prefetch=3,
            grid=(MOE_BLOCKS,),
            in_specs=[
                pl.BlockSpec(memory_space=pl.ANY),
                wspec((D_MODEL, MOE_D_FF)),
                wspec((D_MODEL, MOE_D_FF)),
                wspec((MOE_D_FF, D_MODEL)),
            ],
            out_specs=pl.BlockSpec((MOE_BLK, D_MODEL), lambda i, be, rt, nv: (i, 0)),
            scratch_shapes=[
                pltpu.VMEM((2, MOE_BLK, D_MODEL), F32),
                pltpu.SemaphoreType.DMA((2,)),
                pltpu.VMEM((D_MODEL, MOE_D_FF), BF16),
                pltpu.VMEM((D_MODEL, MOE_D_FF), BF16),
                pltpu.VMEM((MOE_D_FF, D_MODEL), BF16),
            ],
        ),
        out_shape=jax.ShapeDtypeStruct((MOE_ROWS, D_MODEL), F32),
        compiler_params=_cparams("arbitrary"),
        name="moe_ffn",
    )(blk_e, row_tok, nvalid, h, w_gate, w_up, w_down)


def _combine_kernel(dest_ref, y_hbm, h_ref, route_ref, g2_ref, b2_ref, o_ref, ybuf, sems):
    i = pl.program_id(0)
    n_steps = pl.num_programs(0)
    slot = i % 2

    @pl.when(i == 0)
    def _():
        _gather_rows(y_hbm, ybuf.at[0], sems.at[0], dest_ref, 0, TOP_K * CMB_TM)

    @pl.when(i + 1 < n_steps)
    def _():
        _gather_rows(y_hbm, ybuf.at[1 - slot], sems.at[1 - slot], dest_ref, (i + 1) * TOP_K * CMB_TM, TOP_K * CMB_TM)

    _wait_rows(y_hbm, ybuf.at[slot], sems.at[slot])
    w0 = route_ref[:, ROUTE_W0:ROUTE_W0 + 1]
    w1 = route_ref[:, ROUTE_W1:ROUTE_W1 + 1]
    moe = ybuf[slot, 0:CMB_TM] * w0 + ybuf[slot, CMB_TM:2 * CMB_TM] * w1
    o_ref[...] = _layer_norm(ALPHA * h_ref[...] + moe, g2_ref[...], b2_ref[...])


def _combine(dest, y_rows, h, route, ln_g, ln_b):
    n = h.shape[0]
    return pl.pallas_call(
        _combine_kernel,
        grid_spec=pltpu.PrefetchScalarGridSpec(
            num_scalar_prefetch=1,
            grid=(n // CMB_TM,),
            in_specs=[
                pl.BlockSpec(memory_space=pl.ANY),
                pl.BlockSpec((CMB_TM, D_MODEL), lambda i, d: (i, 0)),
                pl.BlockSpec((CMB_TM, LANES), lambda i, d: (i, 0)),
                pl.BlockSpec((1, D_MODEL), lambda i, d: (0, 0)),
                pl.BlockSpec((1, D_MODEL), lambda i, d: (0, 0)),
            ],
            out_specs=pl.BlockSpec((CMB_TM, D_MODEL), lambda i, d: (i, 0)),
            scratch_shapes=[pltpu.VMEM((2, TOP_K * CMB_TM, D_MODEL), F32), pltpu.SemaphoreType.DMA((2,))],
        ),
        out_shape=jax.ShapeDtypeStruct((n, D_MODEL), F32),
        compiler_params=_cparams("arbitrary"),
        name="moe_combine_ln2",
    )(dest, y_rows, h, route, ln_g.reshape(1, D_MODEL), ln_b.reshape(1, D_MODEL))


def _dispatch_indices(route, counts):
    eid = route[:, ROUTE_E0:ROUTE_E1 + 1].astype(jnp.int32)
    rank = route[:, ROUTE_R0:ROUTE_R1 + 1].astype(jnp.int32)
    cnt = counts[0, :N_EXPERTS].astype(jnp.int32)
    padded = (cnt + MOE_BLK - 1) // MOE_BLK * MOE_BLK
    pad_end = jnp.cumsum(padded)
    pad_start = pad_end - padded
    dest = pad_start[eid] + rank
    tok = jnp.broadcast_to(jnp.arange(N_TOK, dtype=jnp.int32)[:, None], dest.shape)
    row_tok = jnp.zeros((MOE_ROWS,), jnp.int32).at[dest.reshape(-1)].set(tok.reshape(-1))
    blk_e = jnp.minimum(jnp.searchsorted(pad_end, jnp.arange(MOE_BLOCKS, dtype=jnp.int32) * MOE_BLK, side='right'),
                        N_EXPERTS - 1).astype(jnp.int32)
    nvalid = (pad_end[-1] // MOE_BLK).astype(jnp.int32).reshape(1)
    dest_tiles = dest.reshape(N_TOK // CMB_TM, CMB_TM, TOP_K).transpose(0, 2, 1).reshape(-1)
    return blk_e, row_tok, nvalid, dest_tiles


def kernel(x_prompt, x_sample, cache_cmp_kv, cache_slc_kv, state_win_kv, state_conv, page_table, w_in, conv_w, conv_b, conv_ln_g, conv_ln_b, cmp_pos_k, cmp_w1_k, cmp_w2_k, cmp_pos_v, cmp_w1_v, cmp_w2_v, w_out, ln1_g, ln1_b, w_grp, b_grp, w_exp, b_exp, w_gate, w_up, w_down, ln2_g, ln2_b):
    xs_t = x_sample.transpose(1, 0, 2).reshape(N_SAMPLE, D_MODEL)
    x = jnp.concatenate([x_prompt.reshape(N_PROMPT, D_MODEL), xs_t], axis=0)
    glu, q, kvc, kvs, kvw, gates, kvs_b, kvw_b = _proj(x, w_in[0])
    conv_p = _conv_prompt(glu, conv_w[0], conv_b[0], conv_ln_g[0], conv_ln_b[0])
    a_s = glu[N_PROMPT:].reshape(DEC_SEQ, DEC_BATCH, CONV_CH)
    xt = jnp.concatenate([state_conv[0].transpose(1, 0, 2), a_s], axis=0)
    conv_s = _conv_sample(xt, conv_w[0], conv_b[0], conv_ln_g[0], conv_ln_b[0])
    cw = (cmp_pos_k[0], cmp_w1_k[0], cmp_w2_k[0], cmp_pos_v[0], cmp_w1_v[0], cmp_w2_v[0])
    kcv_p = _compress_prompt(kvc, cw)
    o_c, sel = _cmp_prompt(q, kcv_p)
    attn_p = _sel_prompt(q, gates, o_c, sel, kvs_b, kvw_b)
    attn_s = _nsa_sample(page_table, cache_cmp_kv[0], cache_slc_kv[0], state_win_kv[0], q[N_PROMPT:],
                         gates[N_PROMPT:], kvs[N_PROMPT:], kvw[N_PROMPT:], cw)
    conv = jnp.concatenate([conv_p, conv_s.reshape(N_SAMPLE, CONV_CH)], axis=0)
    attn = jnp.concatenate([attn_p, attn_s], axis=0)
    h1, route, counts = _out_route(conv, attn, x, w_out[0], ln1_g[0], ln1_b[0], w_grp[0], b_grp[0],
                                   w_exp[0], b_exp[0])
    blk_e, row_tok, nvalid, dest_tiles = _dispatch_indices(route, counts)
    y_rows = _moe(h1, blk_e, row_tok, nvalid, w_gate[0], w_up[0], w_down[0])
    y = _combine(dest_tiles, y_rows, h1, route, ln2_g[0], ln2_b[0])

    def batch_major(a):
        return a[N_PROMPT:].reshape(DEC_SEQ, DEC_BATCH, -1).transpose(1, 0, 2)

    kv_shape = (2, N_KV, HEAD_DIM)
    win_len = min(WINDOW, SEQ)
    y_prompt = y[:N_PROMPT].reshape(1, SEQ, D_MODEL)
    y_sample = batch_major(y)
    p_cmp = kvc[:N_PROMPT].reshape(1, 1, SEQ, *kv_shape)
    p_slc = kvs[:N_PROMPT].reshape(1, 1, SEQ, *kv_shape)
    p_win = kvw[N_PROMPT - win_len:N_PROMPT].reshape(1, 1, win_len, *kv_shape)
    p_conv = glu[N_PROMPT - (CONV_WIDTH - 1):N_PROMPT].reshape(1, 1, CONV_WIDTH - 1, CONV_CH)
    s_cmp = batch_major(kvc).reshape(1, DEC_BATCH, DEC_SEQ, *kv_shape)
    s_slc = batch_major(kvs).reshape(1, DEC_BATCH, DEC_SEQ, *kv_shape)
    s_win = jnp.concatenate([state_win_kv[0][:, DEC_SEQ:],
                             batch_major(kvw).reshape(DEC_BATCH, DEC_SEQ, *kv_shape)], axis=1)[None]
    s_conv = jnp.concatenate([state_conv[0][:, DEC_SEQ:], batch_major(glu)], axis=1)[None]
    return (y_prompt, y_sample, p_cmp, p_slc, p_win, p_conv, s_cmp, s_slc, s_win, s_conv)
```

```python
import functools

import numpy as np
import jax
import jax.numpy as jnp
from jax import lax
from jax.experimental import pallas as pl
from jax.experimental.pallas import tpu as pltpu

F32 = jnp.float32
BF16 = jnp.bfloat16

D_MODEL = 2048
SEQ = 8192
DEC_BATCH = 128
DEC_SEQ = 4
PAST_LEN = 2048
PAGE_SIZE = 128
CONV_CH = 1024
CONV_WIDTH = 31
N_HEADS = 16
HEAD_DIM = 64
N_KV = 4
GQA = 4
Q_WIDTH = 1024
KV_WIDTH = 256
COMP_BLOCK = 32
COMP_STRIDE = 16
CMP_HIDDEN = 128
SEL_BLOCK = 64
N_SEL = 16
WINDOW = 512
N_GROUPS = 8
EXPERTS_PER_GROUP = 8
N_EXPERTS = 64
TOP_K = 2
MOE_D_FF = 512
LN_EPS = 1e-5
ALPHA = 2.0 ** 0.25

N_PROMPT = SEQ
N_SAMPLE = DEC_BATCH * DEC_SEQ
N_TOK = N_PROMPT + N_SAMPLE

LANES = 128
VMEM_LIMIT = 48 * 1024 * 1024


def _cparams(*sem):
    return pltpu.CompilerParams(dimension_semantics=sem, vmem_limit_bytes=VMEM_LIMIT)


PROJ_TM = 512
PROJ_TN = 512
PROJ_STEPS = 9


def _proj_kernel(x_ref, w_ref, wg_ref, glu_ref, q_ref, kvc_ref, kvs_ref, kvw_ref, g_ref,
                 ksk_ref, vst_ref, kwk_ref, vwt_ref, xb_ref):
    j = pl.program_id(1)

    @pl.when(j == 0)
    def _():
        xb_ref[...] = x_ref[...].astype(BF16)

    acc = jnp.dot(xb_ref[...], w_ref[...], preferred_element_type=F32)

    @pl.when(j < 4)
    def _():
        glu_ref[...] = acc[:, :256] * jax.nn.sigmoid(acc[:, 256:])

    @pl.when((j == 4) | (j == 5))
    def _():
        q_ref[...] = (acc * (HEAD_DIM ** -0.5 * LOG2E)).astype(BF16)

    @pl.when(j == 6)
    def _():
        kvc_ref[...] = acc

    @pl.when(j == 7)
    def _():
        kvs_ref[...] = acc
        ksk_ref[...] = acc[:, :KV_WIDTH].astype(BF16)
        vst_ref[...] = acc[:, KV_WIDTH:].T.astype(BF16)

    @pl.when(j == 8)
    def _():
        kvw_ref[...] = acc
        kwk_ref[...] = acc[:, :KV_WIDTH].astype(BF16)
        vwt_ref[...] = acc[:, KV_WIDTH:].T.astype(BF16)
        g_ref[...] = jax.nn.sigmoid(jnp.dot(xb_ref[...], wg_ref[...], preferred_element_type=F32))


def _proj(x, w_in):
    n = x.shape[0]
    o1 = 2 * CONV_CH
    o2 = o1 + Q_WIDTH
    o3 = o2 + 6 * KV_WIDTH
    wa, wb = w_in[:, :CONV_CH], w_in[:, CONV_CH:o1]
    glu_cols = [jnp.concatenate([wa[:, 256 * s:256 * (s + 1)], wb[:, 256 * s:256 * (s + 1)]], axis=1)
                for s in range(4)]
    w_cat = jnp.concatenate(glu_cols + [w_in[:, o1:o3]], axis=1).astype(BF16)
    wg = w_in[:, o3:].reshape(D_MODEL, N_HEADS, 3).transpose(0, 2, 1).reshape(D_MODEL, 3 * N_HEADS)
    wg = jnp.pad(wg, ((0, 0), (0, LANES - 3 * N_HEADS))).astype(BF16)
    grid = (n // PROJ_TM, PROJ_STEPS)
    return pl.pallas_call(
        _proj_kernel,
        grid=grid,
        in_specs=[
            pl.BlockSpec((PROJ_TM, D_MODEL), lambda i, j: (i, 0)),
            pl.BlockSpec((D_MODEL, PROJ_TN), lambda i, j: (0, j)),
            pl.BlockSpec((D_MODEL, LANES), lambda i, j: (0, 0)),
        ],
        out_specs=[
            pl.BlockSpec((PROJ_TM, 256), lambda i, j: (i, jnp.minimum(j, 3))),
            pl.BlockSpec((PROJ_TM, 512), lambda i, j: (i, jnp.clip(j - 4, 0, 1))),
            pl.BlockSpec((PROJ_TM, 512), lambda i, j: (i, 0)),
            pl.BlockSpec((PROJ_TM, 512), lambda i, j: (i, 0)),
            pl.BlockSpec((PROJ_TM, 512), lambda i, j: (i, 0)),
            pl.BlockSpec((PROJ_TM, LANES), lambda i, j: (i, 0)),
            pl.BlockSpec((PROJ_TM, KV_WIDTH), lambda i, j: (i, 0)),
            pl.BlockSpec((KV_WIDTH, PROJ_TM), lambda i, j: (0, i)),
            pl.BlockSpec((PROJ_TM, KV_WIDTH), lambda i, j: (i, 0)),
            pl.BlockSpec((KV_WIDTH, PROJ_TM), lambda i, j: (0, i)),
        ],
        out_shape=[
            jax.ShapeDtypeStruct((n, CONV_CH), F32),
            jax.ShapeDtypeStruct((n, Q_WIDTH), BF16),
            jax.ShapeDtypeStruct((n, 2 * KV_WIDTH), F32),
            jax.ShapeDtypeStruct((n, 2 * KV_WIDTH), F32),
            jax.ShapeDtypeStruct((n, 2 * KV_WIDTH), F32),
            jax.ShapeDtypeStruct((n, LANES), F32),
            jax.ShapeDtypeStruct((n, KV_WIDTH), BF16),
            jax.ShapeDtypeStruct((KV_WIDTH, n), BF16),
            jax.ShapeDtypeStruct((n, KV_WIDTH), BF16),
            jax.ShapeDtypeStruct((KV_WIDTH, n), BF16),
        ],
        scratch_shapes=[pltpu.VMEM((PROJ_TM, D_MODEL), BF16)],
        compiler_params=_cparams("parallel", "arbitrary"),
        name="proj",
    )(x, w_cat, wg)


CONV_T = 256
CONV_HALO = 32
CONV_RC = 64
CONV_CC = 256


def _ln_silu(y, g, beta):
    mu = jnp.mean(y, axis=-1, keepdims=True)
    yc = y - mu
    var = jnp.mean(yc * yc, axis=-1, keepdims=True)
    z = yc * lax.rsqrt(var + LN_EPS) * g + beta
    return z * jax.nn.sigmoid(z)


def _conv_prompt_kernel(cur_ref, prev_ref, w_ref, b_ref, g_ref, beta_ref, o_ref, xp_ref, y_ref):
    i = pl.program_id(0)
    xp_ref[CONV_HALO:, :] = cur_ref[...]

    @pl.when(i == 0)
    def _():
        xp_ref[0:CONV_HALO, :] = jnp.zeros((CONV_HALO, CONV_CH), F32)

    @pl.when(i > 0)
    def _():
        xp_ref[0:CONV_HALO, :] = prev_ref[...]

    off = CONV_HALO - (CONV_WIDTH - 1)
    for c in range(CONV_CH // CONV_CC):
        cols = slice(c * CONV_CC, (c + 1) * CONV_CC)
        for r in range(CONV_T // CONV_RC):
            acc = jnp.zeros((CONV_RC, CONV_CC), F32)
            for k in range(CONV_WIDTH):
                r0 = r * CONV_RC + off + k
                acc = acc + w_ref[k:k + 1, cols] * xp_ref[r0:r0 + CONV_RC, cols]
            y_ref[r * CONV_RC:(r + 1) * CONV_RC, cols] = acc
    y = y_ref[...] + b_ref[...]
    o_ref[...] = _ln_silu(y, g_ref[...], beta_ref[...]).astype(BF16)


def _conv_prompt(a, conv_w, conv_b, ln_g, ln_b):
    wp = jnp.pad(conv_w, ((0, 1), (0, 0)))
    row = lambda v: v.reshape(1, CONV_CH)
    hb = CONV_T // CONV_HALO
    return pl.pallas_call(
        _conv_prompt_kernel,
        grid=(N_PROMPT // CONV_T,),
        in_specs=[
            pl.BlockSpec((CONV_T, CONV_CH), lambda i: (i, 0)),
            pl.BlockSpec((CONV_HALO, CONV_CH), lambda i: (jnp.maximum(i * hb - 1, 0), 0)),
            pl.BlockSpec((CONV_WIDTH + 1, CONV_CH), lambda i: (0, 0)),
            pl.BlockSpec((1, CONV_CH), lambda i: (0, 0)),
            pl.BlockSpec((1, CONV_CH), lambda i: (0, 0)),
            pl.BlockSpec((1, CONV_CH), lambda i: (0, 0)),
        ],
        out_specs=pl.BlockSpec((CONV_T, CONV_CH), lambda i: (i, 0)),
        out_shape=jax.ShapeDtypeStruct((N_PROMPT, CONV_CH), BF16),
        scratch_shapes=[pltpu.VMEM((CONV_T + CONV_HALO, CONV_CH), F32), pltpu.VMEM((CONV_T, CONV_CH), F32)],
        compiler_params=_cparams("parallel"),
        name="conv_prompt",
    )(a, a, wp, row(conv_b), row(ln_g), row(ln_b))


def _conv_sample_kernel(xt_ref, w_ref, b_ref, g_ref, beta_ref, o_ref):
    for t in range(DEC_SEQ):
        acc = jnp.zeros((DEC_BATCH, CONV_CH), F32)
        for k in range(CONV_WIDTH):
            acc = acc + w_ref[k:k + 1, :] * xt_ref[t + k]
        y = acc + b_ref[...]
        o_ref[t] = _ln_silu(y, g_ref[...], beta_ref[...]).astype(BF16)


def _conv_sample(xt, conv_w, conv_b, ln_g, ln_b):
    wp = jnp.pad(conv_w, ((0, 1), (0, 0)))
    row = lambda v: v.reshape(1, CONV_CH)
    npos = CONV_WIDTH - 1 + DEC_SEQ
    return pl.pallas_call(
        _conv_sample_kernel,
        grid=(1,),
        in_specs=[
            pl.BlockSpec((npos, DEC_BATCH, CONV_CH), lambda i: (0, 0, 0)),
            pl.BlockSpec((CONV_WIDTH + 1, CONV_CH), lambda i: (0, 0)),
            pl.BlockSpec((1, CONV_CH), lambda i: (0, 0)),
            pl.BlockSpec((1, CONV_CH), lambda i: (0, 0)),
            pl.BlockSpec((1, CONV_CH), lambda i: (0, 0)),
        ],
        out_specs=pl.BlockSpec((DEC_SEQ, DEC_BATCH, CONV_CH), lambda i: (0, 0, 0)),
        out_shape=jax.ShapeDtypeStruct((DEC_SEQ, DEC_BATCH, CONV_CH), BF16),
        compiler_params=_cparams("arbitrary"),
        name="conv_sample",
    )(xt, wp, row(conv_b), row(ln_g), row(ln_b))


HALF = HEAD_DIM
NEG_BIG = -1e30
CHUNKS_PER_PAGE = PAGE_SIZE // COMP_STRIDE
CHUNK_LANES = COMP_STRIDE * 2 * KV_WIDTH


LOG2E = 1.4426950408889634


def _alibi_slope(h):
    return float(2.0 ** (-8.0 * (h + 1) / N_HEADS) * LOG2E)


def _gelu_tanh(x):
    c = np.float32(np.sqrt(2.0 / np.pi))
    return 0.5 * x * (1.0 + jnp.tanh(c * (x + 0.044715 * (x * x * x))))


def _dot_nt(a, b, precision=None):
    return lax.dot_general(a, b, (((1,), (1,)), ((), ())), preferred_element_type=F32, precision=precision)


def _half_mask(shape, half):
    lane = lax.broadcasted_iota(jnp.int32, shape, len(shape) - 1)
    return (lane >= HALF) if half else (lane < HALF)


def _swap_halves(x):
    return pltpu.roll(x, HALF, axis=x.ndim - 1)


def _compress_weights(pos, w1, w2):
    eye2 = jnp.eye(2, dtype=F32)
    base = w1.reshape(2, COMP_STRIDE, HEAD_DIM, CMP_HIDDEN)
    w1p = jnp.einsum('rsde,hg->rshdge', base, eye2).reshape(2, COMP_STRIDE * 2 * HEAD_DIM, 2 * CMP_HIDDEN)
    pe = jnp.broadcast_to(pos.reshape(2, COMP_STRIDE, 1, HEAD_DIM), (2, COMP_STRIDE, 2, HEAD_DIM))
    pe = pe.reshape(2, 1, COMP_STRIDE * 2 * HEAD_DIM)
    w2p = jnp.einsum('ed,hg->hegd', w2, eye2).reshape(2 * CMP_HIDDEN, 2 * HEAD_DIM)
    return w1p.astype(BF16), pe, w2p.astype(BF16)


def _compress_pair(x, pe_ref, w1_ref, w2_ref):
    n = x.shape[0]
    a0 = jnp.dot((x + pe_ref[0]).astype(BF16), w1_ref[0], preferred_element_type=F32)
    a1 = jnp.dot((x + pe_ref[1]).astype(BF16), w1_ref[1], preferred_element_type=F32)
    hid = a0 + pltpu.roll(a1, n - 1, axis=0)
    return jnp.dot(_gelu_tanh(hid).astype(BF16), w2_ref[...], preferred_element_type=F32)


def _overlap_map(nc, ns, nc_pad, ns_pad):
    i = np.arange(nc_pad)[None, :]
    j = np.arange(ns_pad)[:, None]
    c_start, s_start = i * COMP_STRIDE, j * SEL_BLOCK
    m = (c_start <= s_start + SEL_BLOCK - 1) & (c_start + COMP_BLOCK - 1 >= s_start) & (i < nc) & (j < ns)
    return jnp.asarray(m.astype(np.float32))


def _select_blocks(imp_t, q_pos, n_iter):
    nb = imp_t.shape[0]
    blk = lax.broadcasted_iota(jnp.int32, imp_t.shape, 0)
    blk_f = blk.astype(F32)
    cur = q_pos >> 6
    valid = blk * SEL_BLOCK <= q_pos
    forced = (blk == 0) | (blk == cur) | (blk == cur - 1)
    score = jnp.where(valid, jnp.where(forced, 3e38, imp_t), -1.0)
    sel = jnp.zeros(imp_t.shape, F32)
    for _ in range(n_iter):
        mx = jnp.max(score, axis=0, keepdims=True)
        first = jnp.min(jnp.where(score == mx, blk_f, float(nb)), axis=0, keepdims=True)
        pick = (blk_f == first) & (mx >= 0.0)
        sel = jnp.where(pick, 1.0, sel)
        score = jnp.where(pick, -2.0, score)
    return sel


P_CHUNKS = SEQ // COMP_STRIDE
P_NC = (SEQ - COMP_BLOCK) // COMP_STRIDE + 1
P_NS = SEQ // SEL_BLOCK


def _compress_prompt_kernel(*refs):
    c_refs = refs[:COMP_STRIDE]
    pe_ref, w1_ref, w2_ref, o_ref = refs[COMP_STRIDE:]
    x = jnp.concatenate([r[...] for r in c_refs], axis=1)
    o_ref[...] = _compress_pair(x, pe_ref, w1_ref, w2_ref).astype(BF16)


def _compress_prompt(kvc, cw):
    pos_k, w1_k, w2_k, pos_v, w1_v, w2_v = cw
    wk, wv = _compress_weights(pos_k, w1_k, w2_k), _compress_weights(pos_v, w1_v, w2_v)
    w1p, pe, w2p = (jnp.stack([a, b]) for a, b in zip(wk, wv))
    chunks = kvc.reshape(N_TOK // COMP_STRIDE, CHUNK_LANES)
    c_specs = [pl.BlockSpec((P_CHUNKS, LANES), lambda g, s=s: (0, s * 4 + g)) for s in range(COMP_STRIDE)]
    return pl.pallas_call(
        _compress_prompt_kernel,
        grid=(4,),
        in_specs=c_specs + [
            pl.BlockSpec((None, 2, 1, 2048), lambda g: (g // 2, 0, 0, 0)),
            pl.BlockSpec((None, 2, 2048, 256), lambda g: (g // 2, 0, 0, 0)),
            pl.BlockSpec((None, 256, LANES), lambda g: (g // 2, 0, 0)),
        ],
        out_specs=pl.BlockSpec((P_CHUNKS, LANES), lambda g: (0, g)),
        out_shape=jax.ShapeDtypeStruct((P_CHUNKS, 4 * LANES), BF16),
        compiler_params=_cparams("parallel"),
        name="compress_prompt",
    )(*([chunks] * COMP_STRIDE), pe, w1p, w2p)


QB = 256
TK = 256


def _stack_gqa(q_ref, k):
    qa = q_ref[:, 256 * k:256 * k + LANES]
    qb = q_ref[:, 256 * k + LANES:256 * (k + 1)]
    qas = _swap_halves(qa.astype(F32)).astype(BF16)
    qbs = _swap_halves(qb.astype(F32)).astype(BF16)
    order = [qa, qas, qb, qbs] if k % 2 == 0 else [qas, qa, qbs, qb]
    return jnp.concatenate(order, axis=0)


def _pair_up(o0, o1, half):
    return o0 + _swap_halves(o1) if half == 0 else _swap_halves(o0) + o1


def _cmp_prompt_kernel(q_ref, kcv_ref, smap_ref, oc_ref, sel_ref):
    i = pl.program_id(0)
    q0 = i * QB
    q_pos_col = q0 + lax.broadcasted_iota(jnp.int32, (QB, 1), 0)
    c_end = lax.broadcasted_iota(jnp.int32, (1, P_CHUNKS), 1) * COMP_STRIDE + (COMP_BLOCK - 1)
    d_c = q_pos_col - c_end
    ok = d_c >= 0
    d_cf = d_c.astype(F32)
    q_pos_row = q0 + lax.broadcasted_iota(jnp.int32, (1, QB), 1)
    for k in range(N_KV):
        pair, half = k // 2, k % 2
        keep = _half_mask((P_CHUNKS, LANES), half)
        kc = jnp.where(keep, kcv_ref[:, LANES * pair:LANES * (pair + 1)], 0)
        vc = jnp.where(keep, kcv_ref[:, 2 * LANES + LANES * pair:2 * LANES + LANES * (pair + 1)], 0)
        s4 = _dot_nt(_stack_gqa(q_ref, k), kc)
        psum = jnp.zeros((QB, P_CHUNKS), F32)
        outs = []
        for g in range(GQA):
            s = s4[g * QB:(g + 1) * QB] - _alibi_slope(4 * k + g) * d_cf
            s = jnp.where(ok, s, NEG_BIG)
            m = jnp.max(s, axis=-1, keepdims=True)
            e = jnp.where(ok, jnp.exp2(s - m), 0.0)
            den = jnp.sum(e, axis=-1, keepdims=True)
            p = e / jnp.where(den > 0, den, 1.0)
            psum = psum + p
            outs.append(jnp.dot(p.astype(BF16), vc, preferred_element_type=F32))
        oc_ref[:, 256 * k:256 * k + LANES] = _pair_up(outs[0], outs[1], half)
        oc_ref[:, 256 * k + LANES:256 * (k + 1)] = _pair_up(outs[2], outs[3], half)
        imp_t = _dot_nt(smap_ref[...], psum, precision=lax.Precision.HIGHEST)
        sel_t = _select_blocks(imp_t, q_pos_row, N_SEL)
        sel_ref[:, LANES * k:LANES * (k + 1)] = (1.0 - sel_t.T).astype(BF16)


def _cmp_prompt(q, kcv):
    smap_t = _overlap_map(P_NC, P_NS, P_CHUNKS, P_NS)
    return pl.pallas_call(
        _cmp_prompt_kernel,
        grid=(N_PROMPT // QB,),
        in_specs=[
            pl.BlockSpec((QB, Q_WIDTH), lambda i: (i, 0)),
            pl.BlockSpec((P_CHUNKS, 4 * LANES), lambda i: (0, 0)),
            pl.BlockSpec((P_NS, P_CHUNKS), lambda i: (0, 0)),
        ],
        out_specs=[
            pl.BlockSpec((QB, Q_WIDTH), lambda i: (i, 0)),
            pl.BlockSpec((QB, N_KV * P_NS), lambda i: (i, 0)),
        ],
        out_shape=[
            jax.ShapeDtypeStruct((N_PROMPT, Q_WIDTH), F32),
            jax.ShapeDtypeStruct((N_PROMPT, N_KV * P_NS), BF16),
        ],
        compiler_params=_cparams("parallel"),
        name="nsa_cmp_prompt",
    )(q, kcv, smap_t)


def _softmax_tile(s4, bias_fn, k, m_ref, l_ref, acc_ref, v_tile, first):
    ps = []
    for g in range(GQA):
        rows = slice(g * QB, (g + 1) * QB)
        s = s4[rows] + bias_fn(g)
        m_tile = jnp.max(s, axis=-1, keepdims=True)
        if first:
            m_new = m_tile
        else:
            m_old = m_ref[k, rows]
            m_new = jnp.maximum(m_old, m_tile)
        p = jnp.exp(s - m_new)
        row_sum = jnp.sum(p, axis=-1, keepdims=True)
        if first:
            l_ref[k, rows] = row_sum
        else:
            scale_old = jnp.exp(m_old - m_new)
            l_ref[k, rows] = scale_old * l_ref[k, rows] + row_sum
            acc_ref[k, rows] = scale_old * acc_ref[k, rows]
        m_ref[k, rows] = m_new
        ps.append(p.astype(BF16))
    pv = jnp.dot(jnp.concatenate(ps, axis=0), v_tile, preferred_element_type=F32)
    if first:
        acc_ref[k] = pv
    else:
        acc_ref[k] = acc_ref[k] + pv


def _kv_tile(kv_ref, row0, k):
    pair, half = k // 2, k % 2
    keep = _half_mask((TK, LANES), half)
    kt = jnp.where(keep, kv_ref[pl.ds(row0, TK), LANES * pair:LANES * (pair + 1)], 0)
    vt = jnp.where(keep, kv_ref[pl.ds(row0, TK), 2 * LANES + LANES * pair:2 * LANES + LANES * (pair + 1)], 0)
    return kt, vt


def _sel_prompt_kernel(q_ref, g_ref, oc_ref, sel_ref, ks_ref, kw_ref, o_ref,
                       q4_ref, m_ref, l_ref, acc_ref, os_ref):
    i = pl.program_id(0)
    q0 = pl.multiple_of(i * QB, QB)
    for k in range(N_KV):
        q4_ref[k] = _stack_gqa(q_ref, k)
    q_pos = q0 + lax.broadcasted_iota(jnp.int32, (QB, 1), 0)
    lane_key = lax.broadcasted_iota(jnp.int32, (1, TK), 1)
    blk_row = lax.broadcasted_iota(jnp.int32, (P_NS, 1), 0)

    def dist(row0):
        return (q_pos - (row0 + lane_key)).astype(F32)

    def sel_mask(row0, k):
        expand = jnp.where(blk_row == (row0 >> 6) + (lane_key >> 6), 1.0, 0.0).astype(BF16)
        return jnp.dot(sel_ref[:, LANES * k:LANES * (k + 1)], expand, preferred_element_type=F32)

    def selected_tile(row0, first):
        d = dist(row0)
        causal_bias = jnp.where(d >= 0, 0.0, NEG_BIG) if first else None
        for k in range(N_KV):
            kt, vt = _kv_tile(ks_ref, row0, k)
            s4 = _dot_nt(q4_ref[k], kt)
            bias = (sel_mask(row0, k) - 1.0) * (-NEG_BIG)
            if first:
                bias = bias + causal_bias
            _softmax_tile(s4, lambda g: bias - _alibi_slope(4 * k + g) * d, k, m_ref, l_ref, acc_ref, vt, first)

    selected_tile(q0, True)

    def body(t, carry):
        selected_tile(pl.multiple_of(t * TK, TK), False)
        return carry

    lax.fori_loop(0, i, body, 0)
    for k in range(N_KV):
        os_ref[k] = acc_ref[k] / l_ref[k]

    def window_tile(row0, first, band):
        d = dist(row0)
        if first:
            mask_bias = jnp.where(d >= 0, 0.0, NEG_BIG)
        elif band:
            mask_bias = jnp.where(d <= WINDOW, 0.0, NEG_BIG)
        else:
            mask_bias = jnp.zeros_like(d)
        for k in range(N_KV):
            kt, vt = _kv_tile(kw_ref, row0, k)
            s4 = _dot_nt(q4_ref[k], kt)
            _softmax_tile(s4, lambda g: mask_bias - _alibi_slope(4 * k + g) * d, k, m_ref, l_ref, acc_ref, vt,
                          first)

    window_tile(q0, True, False)

    @pl.when(i >= 1)
    def _():
        window_tile(pl.multiple_of(q0 - TK, TK), False, False)

    @pl.when(i >= 2)
    def _():
        window_tile(pl.multiple_of(q0 - 2 * TK, TK), False, True)

    low = _half_mask((QB, LANES), 0)
    for k in range(N_KV):
        half = k % 2
        contrib = []
        for g in range(GQA):
            h = 4 * k + g
            rows = slice(g * QB, (g + 1) * QB)
            o_s = os_ref[k, rows] * g_ref[:, N_HEADS + h:N_HEADS + h + 1]
            o_w = (acc_ref[k, rows] / l_ref[k, rows]) * g_ref[:, 2 * N_HEADS + h:2 * N_HEADS + h + 1]
            contrib.append(o_s + o_w)
        for pr in range(2):
            h0 = 4 * k + 2 * pr
            gate_c = jnp.where(low, g_ref[:, h0:h0 + 1], g_ref[:, h0 + 1:h0 + 2])
            cols = slice(256 * k + LANES * pr, 256 * k + LANES * (pr + 1))
            o = gate_c * oc_ref[:, cols] + _pair_up(contrib[2 * pr], contrib[2 * pr + 1], half)
            o_ref[:, cols] = o.astype(BF16)


def _sel_prompt(q, gates, o_c, sel, kvs_b, kvw_b):
    resident = lambda shape: pl.BlockSpec(shape, lambda i: (0, 0), pipeline_mode=pl.Buffered(1))
    return pl.pallas_call(
        _sel_prompt_kernel,
        grid=(N_PROMPT // QB,),
        in_specs=[
            pl.BlockSpec((QB, Q_WIDTH), lambda i: (i, 0)),
            pl.BlockSpec((QB, LANES), lambda i: (i, 0)),
            pl.BlockSpec((QB, Q_WIDTH), lambda i: (i, 0)),
            pl.BlockSpec((QB, N_KV * P_NS), lambda i: (i, 0)),
            resident((N_PROMPT, 2 * KV_WIDTH)),
            resident((N_PROMPT, 2 * KV_WIDTH)),
        ],
        out_specs=pl.BlockSpec((QB, Q_WIDTH), lambda i: (i, 0)),
        out_shape=jax.ShapeDtypeStruct((N_PROMPT, Q_WIDTH), BF16),
        scratch_shapes=[
            pltpu.VMEM((N_KV, GQA * QB, LANES), BF16),
            pltpu.VMEM((N_KV, GQA * QB, 1), F32),
            pltpu.VMEM((N_KV, GQA * QB, 1), F32),
            pltpu.VMEM((N_KV, GQA * QB, LANES), F32),
            pltpu.VMEM((N_KV, GQA * QB, LANES), F32),
        ],
        compiler_params=_cparams("parallel"),
        name="nsa_sel_prompt",
    )(q, gates, o_c, sel, kvs_b, kvw_b)


TKT = 256
POS_PIECES = 3


def _bf16_pieces(x, n):
    out = []
    for _ in range(n):
        p = float(np.float32(x).astype(jnp.bfloat16))
        out.append(p)
        x = x - p
    return out


def _position_lanes():
    pos = np.arange(SEQ)
    out = np.zeros((2, SEQ, LANES), np.float32)
    for half in range(2):
        base = HALF * (1 - half)
        for j in range(POS_PIECES):
            out[half, :, base + j] = (pos >> 6) * SEL_BLOCK
            out[half, :, base + POS_PIECES + j] = pos & (SEL_BLOCK - 1)
    return jnp.asarray(out, BF16)


def _block_drop_rows():
    m = np.where(np.arange(P_NS)[None, :] == (np.arange(SEQ)[:, None] // SEL_BLOCK), NEG_BIG, 0.0)
    return jnp.asarray(m.astype(np.float32), BF16)


def _selw_kernel(q_ref, g_ref, oc_ref, drop_ref, ksk_ref, vst_ref, kwk_ref, vwt_ref, pos_ref, e_ref, o_ref,
                 qa_ref, m_ref, acc_ref, os_ref):
    i = pl.program_id(0)
    q0 = pl.multiple_of(i * QB, QB)
    lane1 = lax.broadcasted_iota(jnp.int32, (1, LANES), 1)

    for k in range(N_KV):
        half = k % 2
        base = HALF * (1 - half)
        qa = q_ref[:, 256 * k:256 * k + LANES]
        qb = q_ref[:, 256 * k + LANES:256 * (k + 1)]
        qas = _swap_halves(qa.astype(F32)).astype(BF16)
        qbs = _swap_halves(qb.astype(F32)).astype(BF16)
        order = [qa, qas, qb, qbs] if half == 0 else [qas, qa, qbs, qb]
        own = _half_mask((QB, LANES), half)
        parts = []
        for g in range(GQA):
            sv = jnp.zeros((1, LANES), F32)
            pieces = _bf16_pieces(_alibi_slope(4 * k + g), POS_PIECES)
            for j, piece in enumerate(pieces + pieces):
                sv = jnp.where(lane1 == base + j, piece, sv)
            parts.append(jnp.where(own, order[g], sv.astype(BF16)))
        drop = drop_ref[:, LANES * k:LANES * (k + 1)]
        qa_ref[k] = jnp.concatenate([jnp.concatenate(parts, axis=0), jnp.concatenate([drop] * GQA, axis=0)], axis=1)

    key_row = lax.broadcasted_iota(jnp.int32, (TKT, 1), 0)
    q_lane = lax.broadcasted_iota(jnp.int32, (1, GQA * QB), 1) & (QB - 1)
    vrow = lax.broadcasted_iota(jnp.int32, (LANES, 1), 0)

    def tile(kk_ref, vt_ref, row0, selected, mode, first):
        if mode is not None:
            d = (q0 + q_lane) - (row0 + key_row)
            ok = (d >= 0) if mode == "causal" else (d <= WINDOW)
        for k in range(N_KV):
            pair, half = k // 2, k % 2
            own = _half_mask((TKT, LANES), half)
            ka = jnp.where(own, kk_ref[pl.ds(row0, TKT), LANES * pair:LANES * (pair + 1)],
                           pos_ref[half, pl.ds(row0, TKT), :])
            if selected:
                st = _dot_nt(jnp.concatenate([ka, e_ref[pl.ds(row0, TKT), :]], axis=1), qa_ref[k])
            else:
                st = _dot_nt(ka, qa_ref[k, :, 0:LANES])
            if mode is not None:
                st = jnp.where(ok, st, NEG_BIG)
            m_tile = jnp.max(st, axis=0, keepdims=True)
            own_rows = (vrow >= HALF) if half else (vrow < HALF)
            ones_row = jnp.where(vrow == HALF * (1 - half), 1.0, 0.0).astype(BF16)
            va = jnp.where(own_rows, vt_ref[LANES * pair:LANES * (pair + 1), pl.ds(row0, TKT)], ones_row)
            if first:
                m_new = m_tile
            else:
                m_old = m_ref[k]
                m_new = jnp.maximum(m_old, m_tile)
            p = jnp.exp2(st - m_new).astype(BF16)
            pv = jnp.dot(va, p, preferred_element_type=F32)
            if first:
                acc_ref[k] = pv
            else:
                acc_ref[k] = jnp.exp2(m_old - m_new) * acc_ref[k] + pv
            m_ref[k] = m_new

    def normalised(k):
        half = k % 2
        den_row = HALF * (1 - half)
        return acc_ref[k] / acc_ref[k, den_row:den_row + 1, :]

    tile(ksk_ref, vst_ref, q0, True, "causal", True)

    def body(t, carry):
        tile(ksk_ref, vst_ref, pl.multiple_of(t * TKT, TKT), True, None, False)
        return carry

    lax.fori_loop(0, i, body, 0)
    for k in range(N_KV):
        os_ref[k] = normalised(k)

    tile(kwk_ref, vwt_ref, q0, False, "causal", True)

    @pl.when(i >= 1)
    def _():
        tile(kwk_ref, vwt_ref, pl.multiple_of(q0 - TKT, TKT), False, None, False)

    @pl.when(i >= 2)
    def _():
        tile(kwk_ref, vwt_ref, pl.multiple_of(q0 - 2 * TKT, TKT), False, "band", False)

    g_t = g_ref[...].T
    low = _half_mask((QB, LANES), 0)
    for k in range(N_KV):
        half = k % 2
        o_w = normalised(k)
        contrib = []
        for g in range(GQA):
            h = 4 * k + g
            cols = slice(g * QB, (g + 1) * QB)
            c_t = (os_ref[k, :, cols] * g_t[N_HEADS + h:N_HEADS + h + 1, :]
                   + o_w[:, cols] * g_t[2 * N_HEADS + h:2 * N_HEADS + h + 1, :])
            contrib.append(c_t.T)
        for pr in range(2):
            h0 = 4 * k + 2 * pr
            c0, c1 = contrib[2 * pr], contrib[2 * pr + 1]
            both = jnp.where(low, c0, _swap_halves(c1)) if half == 0 else jnp.where(low, _swap_halves(c0), c1)
            gate_c = jnp.where(low, g_ref[:, h0:h0 + 1], g_ref[:, h0 + 1:h0 + 2])
            cols = slice(256 * k + LANES * pr, 256 * k + LANES * (pr + 1))
            o_ref[:, cols] = (gate_c * oc_ref[:, cols] + both).astype(BF16)


def _selw_prompt(q, gates, o_c, drop, ks_k, vs_t, kw_k, vw_t):
    resident = lambda shape: pl.BlockSpec(shape, lambda i: (0,) * len(shape), pipeline_mode=pl.Buffered(1))
    return pl.pallas_call(
        _selw_kernel,
        grid=(N_PROMPT // QB,),
        in_specs=[
            pl.BlockSpec((QB, Q_WIDTH), lambda i: (i, 0)),
            pl.BlockSpec((QB, LANES), lambda i: (i, 0)),
            pl.BlockSpec((QB, Q_WIDTH), lambda i: (i, 0)),
            pl.BlockSpec((QB, N_KV * P_NS), lambda i: (i, 0)),
            resident((N_PROMPT, KV_WIDTH)),
            resident((KV_WIDTH, N_PROMPT)),
            resident((N_PROMPT, KV_WIDTH)),
            resident((KV_WIDTH, N_PROMPT)),
            resident((2, SEQ, LANES)),
            resident((SEQ, LANES)),
        ],
        out_specs=pl.BlockSpec((QB, Q_WIDTH), lambda i: (i, 0)),
        out_shape=jax.ShapeDtypeStruct((N_PROMPT, Q_WIDTH), BF16),
        scratch_shapes=[
            pltpu.VMEM((N_KV, GQA * QB, 2 * LANES), BF16),
            pltpu.VMEM((N_KV, 1, GQA * QB), F32),
            pltpu.VMEM((N_KV, LANES, GQA * QB), F32),
            pltpu.VMEM((N_KV, LANES, GQA * QB), F32),
        ],
        compiler_params=_cparams("parallel"),
        name="nsa_selw_prompt",
    )(q, gates, o_c, drop, ks_k, vs_t, kw_k, vw_t, _position_lanes(), _block_drop_rows())


N_PAGES = PAST_LEN // PAGE_SIZE
S_CHUNKS = PAST_LEN // COMP_STRIDE
S_NC = (PAST_LEN + DEC_SEQ - COMP_BLOCK) // COMP_STRIDE + 1
S_NS = -(-(PAST_LEN + DEC_SEQ) // SEL_BLOCK)
S_KEYS = 17 * LANES
S_NEW = 16
WIN_BUF = min(WINDOW, PAST_LEN)
S_WKEYS = 5 * LANES
ROWS16 = GQA * DEC_SEQ


def _row_softmax(s, ok):
    s = jnp.where(ok, s, NEG_BIG)
    m = jnp.max(s, axis=-1, keepdims=True)
    e = jnp.where(ok, jnp.exp2(s - m), 0.0)
    den = jnp.sum(e, axis=-1, keepdims=True)
    return e / jnp.where(den > 0, den, 1.0)


def _nsa_sample_kernel(pt_ref, *refs):
    cp = refs[:N_PAGES]
    sp = refs[N_PAGES:2 * N_PAGES]
    (win_ref, news_ref, neww_ref, q_ref, g_ref, pe_ref, w1_ref, w2_ref, smap_ref, expand_ref,
     o_ref, ks_ref, kw_ref) = refs[2 * N_PAGES:]
    del pt_ref

    for p in range(N_PAGES):
        ks_ref[PAGE_SIZE * p:PAGE_SIZE * (p + 1), :] = sp[p][...].astype(BF16)
    ks_ref[PAST_LEN:PAST_LEN + S_NEW, :] = news_ref[...].astype(BF16)
    ks_ref[PAST_LEN + S_NEW:, :] = jnp.zeros((S_KEYS - PAST_LEN - S_NEW, 2 * KV_WIDTH), BF16)
    kw_ref[0:WIN_BUF, :] = win_ref[...].astype(BF16)
    kw_ref[WIN_BUF:WIN_BUF + S_NEW, :] = neww_ref[...].astype(BF16)
    kw_ref[WIN_BUF + S_NEW:, :] = jnp.zeros((S_WKEYS - WIN_BUF - S_NEW, 2 * KV_WIDTH), BF16)

    kcv = []
    for kv in range(2):
        for pair in range(2):
            base = kv * KV_WIDTH + pair * LANES
            x = jnp.concatenate(
                [jnp.concatenate([cp[p][:, s * 2 * KV_WIDTH + base:s * 2 * KV_WIDTH + base + LANES]
                                  for p in range(N_PAGES)], axis=0) for s in range(COMP_STRIDE)], axis=1)
            kcv.append(_compress_pair(x, pe_ref.at[kv], w1_ref.at[kv], w2_ref.at[kv]).astype(BF16))

    row = lax.broadcasted_iota(jnp.int32, (ROWS16, 1), 0)
    q_pos = PAST_LEN + (row & (DEC_SEQ - 1))
    grp = row >> 2

    def slopes(k):
        s = jnp.full((ROWS16, 1), _alibi_slope(4 * k), F32)
        for g in range(1, GQA):
            s = jnp.where(grp == g, _alibi_slope(4 * k + g), s)
        return s

    c_end = lax.broadcasted_iota(jnp.int32, (1, S_CHUNKS), 1) * COMP_STRIDE + (COMP_BLOCK - 1)
    d_c = q_pos - c_end
    o_cmp, psums = [], []
    for k in range(N_KV):
        s = _dot_nt(q_ref[k], kcv[k // 2]) - slopes(k) * d_c.astype(F32)
        p = _row_softmax(s, d_c >= 0)
        o_cmp.append(jnp.dot(p.astype(BF16), kcv[2 + k // 2], preferred_element_type=F32))
        ps = p
        for g in range(1, GQA):
            ps = ps + pltpu.roll(p, DEC_SEQ * g, axis=0)
        psums.append(ps)
    psum_all = jnp.concatenate(psums + [jnp.zeros((LANES - N_KV * ROWS16, S_CHUNKS), F32)], axis=0)
    imp_t = _dot_nt(smap_ref[...], psum_all, precision=lax.Precision.HIGHEST)
    col = lax.broadcasted_iota(jnp.int32, (1, LANES), 1)
    sel_t = _select_blocks(imp_t, PAST_LEN + (col & (DEC_SEQ - 1)), N_SEL)
    sel = sel_t.T.astype(BF16)

    key_s = lax.broadcasted_iota(jnp.int32, (1, S_KEYS), 1)
    d_s = q_pos - key_s
    idx_w = lax.broadcasted_iota(jnp.int32, (1, S_WKEYS), 1)
    d_w = q_pos - (PAST_LEN - WIN_BUF + idx_w)
    ok_w = (d_w >= 0) & (d_w <= WINDOW)
    for k in range(N_KV):
        pair = k // 2
        kcol = slice(LANES * pair, LANES * (pair + 1))
        vcol = slice(2 * LANES + LANES * pair, 2 * LANES + LANES * (pair + 1))
        chosen = jnp.dot(sel[ROWS16 * k:ROWS16 * (k + 1)], expand_ref[...], preferred_element_type=F32)
        s = _dot_nt(q_ref[k], ks_ref[:, kcol]) - slopes(k) * d_s.astype(F32)
        p = _row_softmax(s, (chosen > 0.5) & (d_s >= 0))
        o_sel = jnp.dot(p.astype(BF16), ks_ref[:, vcol], preferred_element_type=F32)
        s = _dot_nt(q_ref[k], kw_ref[:, kcol]) - slopes(k) * d_w.astype(F32)
        p = _row_softmax(s, ok_w)
        o_win = jnp.dot(p.astype(BF16), kw_ref[:, vcol], preferred_element_type=F32)
        o_ref[k] = (g_ref[k][:, 0:1] * o_cmp[k] + g_ref[k][:, 1:2] * o_sel + g_ref[k][:, 2:3] * o_win)


def _nsa_sample(page_table, cache_c, cache_s, win_prev, q_s, gates_s, kvs_new, kvw_new, cw):
    pos_k, w1_k, w2_k, pos_v, w1_v, w2_v = cw
    wk, wv = _compress_weights(pos_k, w1_k, w2_k), _compress_weights(pos_v, w1_v, w2_v)
    w1p, pe, w2p = (jnp.stack([a, b]) for a, b in zip(wk, wv))
    n_phys = cache_c.shape[0]
    pages_c = cache_c.reshape(n_phys, CHUNKS_PER_PAGE, CHUNK_LANES)
    pages_s = cache_s.reshape(n_phys, PAGE_SIZE, 2 * KV_WIDTH)
    win = win_prev.reshape(DEC_BATCH, WIN_BUF, 2 * KV_WIDTH)
    half_of_k = (np.arange(N_KV) % 2)[:, None] == np.arange(2)[None, :]
    half_sel = jnp.asarray(half_of_k.reshape(1, N_KV, 1, 2, 1))
    qh = q_s.reshape(DEC_SEQ, DEC_BATCH, N_KV, GQA, HEAD_DIM).transpose(1, 2, 3, 0, 4)
    qh = qh.reshape(DEC_BATCH, N_KV, ROWS16, 1, HEAD_DIM)
    q16 = jnp.where(half_sel, qh, jnp.zeros((), BF16)).reshape(DEC_BATCH, N_KV, ROWS16, LANES)
    gh = gates_s[:, :3 * N_HEADS].reshape(DEC_SEQ, DEC_BATCH, 3, N_KV, GQA).transpose(1, 3, 4, 0, 2)
    g16 = jnp.pad(gh.reshape(DEC_BATCH, N_KV, ROWS16, 3), ((0, 0), (0, 0), (0, 0), (0, LANES - 3)))

    def new_rows(a):
        a = a.reshape(DEC_SEQ, DEC_BATCH, 2 * KV_WIDTH).transpose(1, 0, 2)
        return jnp.pad(a, ((0, 0), (0, S_NEW - DEC_SEQ), (0, 0)))

    smap_t = _overlap_map(S_NC, S_NS, S_CHUNKS, LANES)
    expand = jnp.asarray((np.arange(LANES)[:, None] == (np.arange(S_KEYS)[None, :] // SEL_BLOCK)), BF16)
    pt = page_table.reshape(-1).astype(jnp.int32)
    c_specs = [pl.BlockSpec((None, CHUNKS_PER_PAGE, CHUNK_LANES), lambda b, pt, p=p: (pt[b * N_PAGES + p], 0, 0))
               for p in range(N_PAGES)]
    s_specs = [pl.BlockSpec((None, PAGE_SIZE, 2 * KV_WIDTH), lambda b, pt, p=p: (pt[b * N_PAGES + p], 0, 0))
               for p in range(N_PAGES)]
    per_b3 = lambda shape: pl.BlockSpec((None,) + shape, lambda b, pt: (b, 0, 0))
    per_b4 = lambda shape: pl.BlockSpec((None,) + shape, lambda b, pt: (b, 0, 0, 0))
    const = lambda shape: pl.BlockSpec(shape, lambda b, pt: (0,) * len(shape))
    o16 = pl.pallas_call(
        _nsa_sample_kernel,
        grid_spec=pltpu.PrefetchScalarGridSpec(
            num_scalar_prefetch=1,
            grid=(DEC_BATCH,),
            in_specs=c_specs + s_specs + [
                per_b3((WIN_BUF, 2 * KV_WIDTH)),
                per_b3((S_NEW, 2 * KV_WIDTH)),
                per_b3((S_NEW, 2 * KV_WIDTH)),
                per_b4((N_KV, ROWS16, LANES)),
                per_b4((N_KV, ROWS16, LANES)),
                const((2, 2, 1, 2048)),
                const((2, 2, 2048, 256)),
                const((2, 256, LANES)),
                const((LANES, S_CHUNKS)),
                const((LANES, S_KEYS)),
            ],
            out_specs=per_b4((N_KV, ROWS16, LANES)),
            scratch_shapes=[pltpu.VMEM((S_KEYS, 2 * KV_WIDTH), BF16), pltpu.VMEM((S_WKEYS, 2 * KV_WIDTH), BF16)],
        ),
        out_shape=jax.ShapeDtypeStruct((DEC_BATCH, N_KV, ROWS16, LANES), F32),
        compiler_params=_cparams("parallel"),
        name="nsa_sample",
    )(pt, *([pages_c] * N_PAGES), *([pages_s] * N_PAGES), win, new_rows(kvs_new), new_rows(kvw_new),
      q16, g16, pe, w1p, w2p, smap_t, expand)
    o = o16.reshape(DEC_BATCH, N_KV, GQA, DEC_SEQ, 2, HEAD_DIM)
    o = jnp.stack([o[:, k, :, :, k % 2, :] for k in range(N_KV)], axis=1)
    return o.transpose(3, 0, 1, 2, 4).reshape(N_SAMPLE, Q_WIDTH).astype(BF16)


HEADS8 = 2 * N_KV
PAGE_FLAT = PAGE_SIZE * HEADS8
S_VMEM_LIMIT = 58 * 1024 * 1024


def _slot_rows(ref, slot, n_rows):
    return ref.reshape(n_rows * HEADS8, HEAD_DIM)[pl.ds(slot, n_rows, stride=HEADS8), :]


def _row_tile(ref, r, n_rows):
    return ref.reshape(n_rows * HEADS8, HEAD_DIM)[HEADS8 * r:HEADS8 * (r + 1), :]


def _nsa_sample_native_kernel(pt_ref, *refs):
    cp = refs[:N_PAGES]
    sp = refs[N_PAGES:2 * N_PAGES]
    (win_ref, news_ref, neww_ref, q_ref, g_ref, pe_ref, w1_ref, w2_ref, smap_ref, expand_ref,
     o_ref, ks_ref, kw_ref) = refs[2 * N_PAGES:]
    del pt_ref

    for c in range(HEADS8):
        for p in range(N_PAGES):
            ks_ref[c, PAGE_SIZE * p:PAGE_SIZE * (p + 1), :] = _slot_rows(sp[p], c, PAGE_SIZE).astype(BF16)
        ks_ref[c, PAST_LEN:PAST_LEN + S_NEW, :] = news_ref[c].astype(BF16)
        ks_ref[c, PAST_LEN + S_NEW:, :] = jnp.zeros((S_KEYS - PAST_LEN - S_NEW, HEAD_DIM), BF16)
        kw_ref[c, 0:WIN_BUF, :] = _slot_rows(win_ref, c, WIN_BUF).astype(BF16)
        kw_ref[c, WIN_BUF:WIN_BUF + S_NEW, :] = neww_ref[c].astype(BF16)
        kw_ref[c, WIN_BUF + S_NEW:, :] = jnp.zeros((S_WKEYS - WIN_BUF - S_NEW, HEAD_DIM), BF16)

    cols = []
    for s in range(COMP_STRIDE):
        cols.append(jnp.concatenate(
            [_row_tile(cp[p], COMP_STRIDE * j + s, PAGE_SIZE)
             for p in range(N_PAGES) for j in range(CHUNKS_PER_PAGE)], axis=0))
    xb = jnp.concatenate(cols, axis=1).astype(BF16)
    is_k = (lax.broadcasted_iota(jnp.int32, (S_CHUNKS * HEADS8, 1), 0) & (HEADS8 - 1)) < N_KV
    acts = []
    for r in range(2):
        bias = jnp.dot(pe_ref[r].astype(BF16), w1_ref[r], preferred_element_type=F32)
        a = jnp.dot(xb, w1_ref[r], preferred_element_type=F32)
        acts.append(jnp.where(is_k, a[:, :CMP_HIDDEN] + bias[0:1, :CMP_HIDDEN],
                              a[:, CMP_HIDDEN:] + bias[1:2, CMP_HIDDEN:]))
    hid = acts[0] + jnp.concatenate([acts[1][HEADS8:], acts[1][:HEADS8]], axis=0)
    gl = _gelu_tanh(hid).astype(BF16)
    kc_all = jnp.where(is_k, jnp.dot(gl, w2_ref[0], preferred_element_type=F32),
                       jnp.dot(gl, w2_ref[1], preferred_element_type=F32)).astype(BF16)

    row = lax.broadcasted_iota(jnp.int32, (ROWS16, 1), 0)
    q_pos = PAST_LEN + (row & (DEC_SEQ - 1))
    grp = row >> 2

    def slopes(k):
        s = jnp.full((ROWS16, 1), _alibi_slope(4 * k), F32)
        for g in range(1, GQA):
            s = jnp.where(grp == g, _alibi_slope(4 * k + g), s)
        return s

    ccol = lax.broadcasted_iota(jnp.int32, (1, S_CHUNKS * HEADS8), 1)
    d_c = q_pos - ((ccol >> 3) * COMP_STRIDE + (COMP_BLOCK - 1))
    slot = ccol & (HEADS8 - 1)
    o_cmp, psums = [], []
    for k in range(N_KV):
        s = _dot_nt(q_ref[k], kc_all) - slopes(k) * d_c.astype(F32)
        p = _row_softmax(s, (d_c >= 0) & (slot == k))
        o_cmp.append(jnp.dot(pltpu.roll(p, N_KV, axis=1).astype(BF16), kc_all, preferred_element_type=F32))
        ps = p
        for g in range(1, GQA):
            ps = ps + pltpu.roll(p, DEC_SEQ * g, axis=0)
        psums.append(ps)
    psum_all = jnp.concatenate(psums + [jnp.zeros((LANES - N_KV * ROWS16, S_CHUNKS * HEADS8), F32)], axis=0)
    imp_t = _dot_nt(smap_ref[...], psum_all, precision=lax.Precision.HIGHEST)
    col = lax.broadcasted_iota(jnp.int32, (1, LANES), 1)
    sel_t = _select_blocks(imp_t, PAST_LEN + (col & (DEC_SEQ - 1)), N_SEL)
    sel = sel_t.T.astype(BF16)

    key_s = lax.broadcasted_iota(jnp.int32, (1, S_KEYS), 1)
    d_s = q_pos - key_s
    idx_w = lax.broadcasted_iota(jnp.int32, (1, S_WKEYS), 1)
    d_w = q_pos - (PAST_LEN - WIN_BUF + idx_w)
    ok_w = (d_w >= 0) & (d_w <= WINDOW)
    for k in range(N_KV):
        chosen = jnp.dot(sel[ROWS16 * k:ROWS16 * (k + 1)], expand_ref[...], preferred_element_type=F32)
        s = _dot_nt(q_ref[k], ks_ref[k]) - slopes(k) * d_s.astype(F32)
        p = _row_softmax(s, (chosen > 0.5) & (d_s >= 0))
        o_sel = jnp.dot(p.astype(BF16), ks_ref[N_KV + k], preferred_element_type=F32)
        s = _dot_nt(q_ref[k], kw_ref[k]) - slopes(k) * d_w.astype(F32)
        p = _row_softmax(s, ok_w)
        o_win = jnp.dot(p.astype(BF16), kw_ref[N_KV + k], preferred_element_type=F32)
        o_ref[k] = (g_ref[k][:, 0:1] * o_cmp[k] + g_ref[k][:, 1:2] * o_sel + g_ref[k][:, 2:3] * o_win)


def _nsa_sample_native(page_table, cache_c, cache_s, win_prev, q_s, gates_s, kvs_new, kvw_new, cw):
    pos_k, w1_k, w2_k, pos_v, w1_v, w2_v = cw
    flat = COMP_STRIDE * HEAD_DIM
    w1 = jnp.concatenate([w1_k.reshape(2, flat, CMP_HIDDEN), w1_v.reshape(2, flat, CMP_HIDDEN)], axis=2).astype(BF16)
    pe = jnp.pad(jnp.stack([pos_k.reshape(2, flat), pos_v.reshape(2, flat)], axis=1), ((0, 0), (0, 6), (0, 0)))
    w2 = jnp.stack([w2_k, w2_v]).astype(BF16)
    qh = q_s.reshape(DEC_SEQ, DEC_BATCH, N_KV, GQA, HEAD_DIM).transpose(1, 2, 3, 0, 4)
    q16 = qh.reshape(DEC_BATCH, N_KV, ROWS16, HEAD_DIM)
    gh = gates_s[:, :3 * N_HEADS].reshape(DEC_SEQ, DEC_BATCH, 3, N_KV, GQA).transpose(1, 3, 4, 0, 2)
    g16 = jnp.pad(gh.reshape(DEC_BATCH, N_KV, ROWS16, 3), ((0, 0), (0, 0), (0, 0), (0, LANES - 3)))

    def new_rows(a):
        a = a.reshape(DEC_SEQ, DEC_BATCH, HEADS8, HEAD_DIM).transpose(1, 2, 0, 3)
        return jnp.pad(a, ((0, 0), (0, 0), (0, S_NEW - DEC_SEQ), (0, 0)))

    smap_t = jnp.repeat(_overlap_map(S_NC, S_NS, S_CHUNKS, LANES), HEADS8, axis=1)
    expand = jnp.asarray((np.arange(LANES)[:, None] == (np.arange(S_KEYS)[None, :] // SEL_BLOCK)), BF16)
    pt = page_table.reshape(-1).astype(jnp.int32)
    page_shape = (None, None, PAGE_SIZE, 2, N_KV, HEAD_DIM)
    page_specs = [pl.BlockSpec(page_shape, lambda b, pt, p=p: (0, pt[b * N_PAGES + p], 0, 0, 0, 0))
                  for p in range(N_PAGES)]
    per_b4 = lambda shape: pl.BlockSpec((None,) + shape, lambda b, pt: (b, 0, 0, 0))
    const = lambda shape: pl.BlockSpec(shape, lambda b, pt: (0,) * len(shape))
    o16 = pl.pallas_call(
        _nsa_sample_native_kernel,
        grid_spec=pltpu.PrefetchScalarGridSpec(
            num_scalar_prefetch=1,
            grid=(DEC_BATCH,),
            in_specs=page_specs + page_specs + [
                pl.BlockSpec((None, None, WIN_BUF, 2, N_KV, HEAD_DIM), lambda b, pt: (0, b, 0, 0, 0, 0)),
                per_b4((HEADS8, S_NEW, HEAD_DIM)),
                per_b4((HEADS8, S_NEW, HEAD_DIM)),
                per_b4((N_KV, ROWS16, HEAD_DIM)),
                per_b4((N_KV, ROWS16, LANES)),
                const((2, 8, COMP_STRIDE * HEAD_DIM)),
                const((2, COMP_STRIDE * HEAD_DIM, 2 * CMP_HIDDEN)),
                const((2, CMP_HIDDEN, HEAD_DIM)),
                const((LANES, S_CHUNKS * HEADS8)),
                const((LANES, S_KEYS)),
            ],
            out_specs=per_b4((N_KV, ROWS16, HEAD_DIM)),
            scratch_shapes=[pltpu.VMEM((HEADS8, S_KEYS, HEAD_DIM), BF16), pltpu.VMEM((HEADS8, S_WKEYS, HEAD_DIM), BF16)],
        ),
        out_shape=jax.ShapeDtypeStruct((DEC_BATCH, N_KV, ROWS16, HEAD_DIM), F32),
        compiler_params=pltpu.CompilerParams(dimension_semantics=("parallel",), vmem_limit_bytes=S_VMEM_LIMIT),
        name="nsa_sample",
    )(pt, *([cache_c] * N_PAGES), *([cache_s] * N_PAGES), win_prev, new_rows(kvs_new), new_rows(kvw_new),
      q16, g16, pe, w1, w2, smap_t, expand)
    o = o16.reshape(DEC_BATCH, N_KV, GQA, DEC_SEQ, HEAD_DIM)
    return o.transpose(3, 0, 1, 2, 4).reshape(N_SAMPLE, Q_WIDTH).astype(BF16)


OUT_TM = 512
ROUTE_E0, ROUTE_E1, ROUTE_W0, ROUTE_W1, ROUTE_R0, ROUTE_R1 = range(6)


def _layer_norm(x, g, b):
    mu = jnp.mean(x, axis=-1, keepdims=True)
    xc = x - mu
    var = jnp.mean(xc * xc, axis=-1, keepdims=True)
    return xc * lax.rsqrt(var + LN_EPS) * g + b


def _first_max(vals, lane_f):
    mx = jnp.max(vals, axis=-1, keepdims=True)
    idx = jnp.min(jnp.where(vals == mx, lane_f, float(LANES)), axis=-1, keepdims=True)
    return mx, idx


def _out_route_kernel(conv_ref, attn_ref, x_ref, wo_ref, g1_ref, b1_ref, wr_ref, br_ref,
                      h_ref, route_ref, counts_ref, run_ref):
    i = pl.program_id(0)

    @pl.when(i == 0)
    def _():
        run_ref[...] = jnp.zeros_like(run_ref)

    mix = jnp.dot(conv_ref[...], wo_ref[0:CONV_CH, :], preferred_element_type=F32)
    mix = mix + jnp.dot(attn_ref[...], wo_ref[CONV_CH:, :], preferred_element_type=F32)
    h = _layer_norm(ALPHA * x_ref[...] + mix, g1_ref[...], b1_ref[...])
    h_ref[...] = h

    logit = jnp.dot(h, wr_ref[...], preferred_element_type=F32, precision=lax.Precision.HIGHEST) + br_ref[...]
    lane = lax.broadcasted_iota(jnp.int32, logit.shape, 1)
    lane_f = lane.astype(F32)
    is_grp = (lane >= N_EXPERTS) & (lane < N_EXPERTS + N_GROUPS)
    g_logit = jnp.where(is_grp, logit, NEG_BIG)
    g_top, g_lane = _first_max(g_logit, lane_f)
    lse = g_top + jnp.log(jnp.sum(jnp.where(is_grp, jnp.exp(g_logit - g_top), 0.0), axis=-1, keepdims=True))
    g_p = jnp.exp(g_top - lse)
    g_idx = g_lane.astype(jnp.int32) - N_EXPERTS
    in_grp = (lane >> 3) == g_idx
    e_logit = jnp.where(in_grp, logit, NEG_BIG)
    e0, i0 = _first_max(e_logit, lane_f)
    e1, i1 = _first_max(jnp.where(lane_f == i0, NEG_BIG, e_logit), lane_f)
    z1 = jnp.exp(e1 - e0)
    w0 = g_p / (1.0 + z1)
    w1 = g_p * z1 / (1.0 + z1)

    hit = jnp.where((lane_f == i0) | (lane_f == i1), 1.0, 0.0)
    r = lax.broadcasted_iota(jnp.int32, (OUT_TM, OUT_TM), 0)
    c = lax.broadcasted_iota(jnp.int32, (OUT_TM, OUT_TM), 1)
    lower = jnp.where(c < r, 1.0, 0.0).astype(BF16)
    before = jnp.dot(lower, hit.astype(BF16), preferred_element_type=F32) + run_ref[...]
    r0 = jnp.sum(jnp.where(lane_f == i0, before, 0.0), axis=-1, keepdims=True)
    r1 = jnp.sum(jnp.where(lane_f == i1, before, 0.0), axis=-1, keepdims=True)
    run_ref[...] = run_ref[...] + jnp.sum(hit, axis=0, keepdims=True)
    counts_ref[...] = jnp.broadcast_to(run_ref[...], counts_ref.shape)

    rec = jnp.zeros(logit.shape, F32)
    for col_id, v in ((ROUTE_E0, i0), (ROUTE_E1, i1), (ROUTE_W0, w0), (ROUTE_W1, w1), (ROUTE_R0, r0), (ROUTE_R1, r1)):
        rec = jnp.where(lane == col_id, v, rec)
    route_ref[...] = rec


def _out_route(conv, attn, x, w_out, ln_g, ln_b, w_grp, b_grp, w_exp, b_exp):
    n = x.shape[0]
    wr = jnp.pad(jnp.concatenate([w_exp, w_grp], axis=1), ((0, 0), (0, LANES - N_EXPERTS - N_GROUPS)))
    br = jnp.pad(jnp.concatenate([b_exp, b_grp]), (0, LANES - N_EXPERTS - N_GROUPS)).reshape(1, LANES)
    const = lambda shape: pl.BlockSpec(shape, lambda i: (0, 0))
    return pl.pallas_call(
        _out_route_kernel,
        grid=(n // OUT_TM,),
        in_specs=[
            pl.BlockSpec((OUT_TM, CONV_CH), lambda i: (i, 0)),
            pl.BlockSpec((OUT_TM, Q_WIDTH), lambda i: (i, 0)),
            pl.BlockSpec((OUT_TM, D_MODEL), lambda i: (i, 0)),
            pl.BlockSpec((D_MODEL, D_MODEL), lambda i: (0, 0), pipeline_mode=pl.Buffered(1)),
            const((1, D_MODEL)),
            const((1, D_MODEL)),
            const((D_MODEL, LANES)),
            const((1, LANES)),
        ],
        out_specs=[
            pl.BlockSpec((OUT_TM, D_MODEL), lambda i: (i, 0)),
            pl.BlockSpec((OUT_TM, LANES), lambda i: (i, 0)),
            const((8, LANES)),
        ],
        out_shape=[
            jax.ShapeDtypeStruct((n, D_MODEL), F32),
            jax.ShapeDtypeStruct((n, LANES), F32),
            jax.ShapeDtypeStruct((8, LANES), F32),
        ],
        scratch_shapes=[pltpu.VMEM((1, LANES), F32)],
        compiler_params=_cparams("arbitrary"),
        name="out_ln1_route",
    )(conv, attn, x, w_out.astype(BF16), ln_g.reshape(1, D_MODEL), ln_b.reshape(1, D_MODEL), wr, br)


MOE_BLK = 128
N_ASSIGN = N_TOK * TOP_K
MOE_BLOCKS = -(-(N_ASSIGN + N_EXPERTS * (MOE_BLK - 1)) // MOE_BLK)
MOE_ROWS = MOE_BLOCKS * MOE_BLK
CMB_TM = 128


def _gather_rows(src_hbm, dst_ref, sem, idx_ref, base, n_rows):
    def body(r, carry):
        pltpu.make_async_copy(src_hbm.at[pl.ds(idx_ref[base + r], 1)], dst_ref.at[pl.ds(r, 1)], sem).start()
        return carry
    lax.fori_loop(0, n_rows, body, 0)


def _wait_rows(src_hbm, dst_ref, sem):
    pltpu.make_async_copy(src_hbm.at[pl.ds(0, dst_ref.shape[0])], dst_ref, sem).wait()


def _moe_kernel(blk_e_ref, row_tok_ref, nvalid_ref, h_hbm, wg_ref, wu_ref, wd_ref, y_ref,
                xbuf, sems, wg_b, wu_b, wd_b):
    i = pl.program_id(0)
    slot = i % 2
    nvalid = nvalid_ref[0]

    @pl.when(i == 0)
    def _():
        _gather_rows(h_hbm, xbuf.at[0], sems.at[0], row_tok_ref, 0, MOE_BLK)

    @pl.when(i + 1 < nvalid)
    def _():
        _gather_rows(h_hbm, xbuf.at[1 - slot], sems.at[1 - slot], row_tok_ref, (i + 1) * MOE_BLK, MOE_BLK)

    @pl.when((i == 0) | (blk_e_ref[i] != blk_e_ref[jnp.maximum(i - 1, 0)]))
    def _():
        wg_b[...] = wg_ref[...].astype(BF16)
        wu_b[...] = wu_ref[...].astype(BF16)
        wd_b[...] = wd_ref[...].astype(BF16)

    @pl.when(i < jnp.maximum(nvalid, 1))
    def _():
        _wait_rows(h_hbm, xbuf.at[slot], sems.at[slot])

    @pl.when(i < nvalid)
    def _():
        x = xbuf[slot].astype(BF16)
        hg = jnp.dot(x, wg_b[...], preferred_element_type=F32)
        hu = jnp.dot(x, wu_b[...], preferred_element_type=F32)
        act = (hg * jax.nn.sigmoid(hg) * hu).astype(BF16)
        y_ref[...] = jnp.dot(act, wd_b[...], preferred_element_type=F32)

    @pl.when(i >= nvalid)
    def _():
        y_ref[...] = jnp.zeros_like(y_ref)


def _moe(h, blk_e, row_tok, nvalid, w_gate, w_up, w_down):
    wspec = lambda shape: pl.BlockSpec((None,) + shape, lambda i, be, rt, nv: (be[i], 0, 0))
    return pl.pallas_call(
        _moe_kernel,
        grid_spec=pltpu.PrefetchScalarGridSpec(
            num_scalar_prefetch=3,
            grid=(MOE_BLOCKS,),
            in_specs=[
                pl.BlockSpec(memory_space=pl.ANY),
                wspec((D_MODEL, MOE_D_FF)),
                wspec((D_MODEL, MOE_D_FF)),
                wspec((MOE_D_FF, D_MODEL)),
            ],
            out_specs=pl.BlockSpec((MOE_BLK, D_MODEL), lambda i, be, rt, nv: (i, 0)),
            scratch_shapes=[
                pltpu.VMEM((2, MOE_BLK, D_MODEL), F32),
                pltpu.SemaphoreType.DMA((2,)),
                pltpu.VMEM((D_MODEL, MOE_D_FF), BF16),
                pltpu.VMEM((D_MODEL, MOE_D_FF), BF16),
                pltpu.VMEM((MOE_D_FF, D_MODEL), BF16),
            ],
        ),
        out_shape=jax.ShapeDtypeStruct((MOE_ROWS, D_MODEL), F32),
        compiler_params=_cparams("arbitrary"),
        name="moe_ffn",
    )(blk_e, row_tok, nvalid, h, w_gate, w_up, w_down)


def _combine_kernel(dest_ref, y_hbm, h_ref, route_ref, g2_ref, b2_ref, o_ref, ybuf, sems):
    i = pl.program_id(0)
    n_steps = pl.num_programs(0)
    slot = i % 2

    @pl.when(i == 0)
    def _():
        _gather_rows(y_hbm, ybuf.at[0], sems.at[0], dest_ref, 0, TOP_K * CMB_TM)

    @pl.when(i + 1 < n_steps)
    def _():
        _gather_rows(y_hbm, ybuf.at[1 - slot], sems.at[1 - slot], dest_ref, (i + 1) * TOP_K * CMB_TM, TOP_K * CMB_TM)

    _wait_rows(y_hbm, ybuf.at[slot], sems.at[slot])
    w0 = route_ref[:, ROUTE_W0:ROUTE_W0 + 1]
    w1 = route_ref[:, ROUTE_W1:ROUTE_W1 + 1]
    moe = ybuf[slot, 0:CMB_TM] * w0 + ybuf[slot, CMB_TM:2 * CMB_TM] * w1
    o_ref[...] = _layer_norm(ALPHA * h_ref[...] + moe, g2_ref[...], b2_ref[...])


def _combine(dest, y_rows, h, route, ln_g, ln_b):
    n = h.shape[0]
    return pl.pallas_call(
        _combine_kernel,
        grid_spec=pltpu.PrefetchScalarGridSpec(
            num_scalar_prefetch=1,
            grid=(n // CMB_TM,),
            in_specs=[
                pl.BlockSpec(memory_space=pl.ANY),
                pl.BlockSpec((CMB_TM, D_MODEL), lambda i, d: (i, 0)),
                pl.BlockSpec((CMB_TM, LANES), lambda i, d: (i, 0)),
                pl.BlockSpec((1, D_MODEL), lambda i, d: (0, 0)),
                pl.BlockSpec((1, D_MODEL), lambda i, d: (0, 0)),
            ],
            out_specs=pl.BlockSpec((CMB_TM, D_MODEL), lambda i, d: (i, 0)),
            scratch_shapes=[pltpu.VMEM((2, TOP_K * CMB_TM, D_MODEL), F32), pltpu.SemaphoreType.DMA((2,))],
        ),
        out_shape=jax.ShapeDtypeStruct((n, D_MODEL), F32),
        compiler_params=_cparams("arbitrary"),
        name="moe_combine_ln2",
    )(dest, y_rows, h, route, ln_g.reshape(1, D_MODEL), ln_b.reshape(1, D_MODEL))


def _dispatch_indices(route, counts):
    eid = route[:, ROUTE_E0:ROUTE_E1 + 1].astype(jnp.int32)
    rank = route[:, ROUTE_R0:ROUTE_R1 + 1].astype(jnp.int32)
    cnt = counts[0, :N_EXPERTS].astype(jnp.int32)
    padded = (cnt + MOE_BLK - 1) // MOE_BLK * MOE_BLK
    pad_end = jnp.cumsum(padded)
    pad_start = pad_end - padded
    dest = pad_start[eid] + rank
    tok = jnp.broadcast_to(jnp.arange(N_TOK, dtype=jnp.int32)[:, None], dest.shape)
    row_tok = jnp.zeros((MOE_ROWS,), jnp.int32).at[dest.reshape(-1)].set(tok.reshape(-1))
    blk_start = jnp.arange(MOE_BLOCKS, dtype=jnp.int32) * MOE_BLK
    blk_e = jnp.minimum(jnp.sum(pad_end[None, :] <= blk_start[:, None], axis=1), N_EXPERTS - 1).astype(jnp.int32)
    nvalid = (pad_end[-1] // MOE_BLK).astype(jnp.int32).reshape(1)
    dest_tiles = dest.reshape(N_TOK // CMB_TM, CMB_TM, TOP_K).transpose(0, 2, 1).reshape(-1)
    return blk_e, row_tok, nvalid, dest_tiles


def kernel(x_prompt, x_sample, cache_cmp_kv, cache_slc_kv, state_win_kv, state_conv, page_table, w_in, conv_w, conv_b, conv_ln_g, conv_ln_b, cmp_pos_k, cmp_w1_k, cmp_w2_k, cmp_pos_v, cmp_w1_v, cmp_w2_v, w_out, ln1_g, ln1_b, w_grp, b_grp, w_exp, b_exp, w_gate, w_up, w_down, ln2_g, ln2_b):
    xs_t = x_sample.transpose(1, 0, 2).reshape(N_SAMPLE, D_MODEL)
    x = jnp.concatenate([x_prompt.reshape(N_PROMPT, D_MODEL), xs_t], axis=0)
    glu, q, kvc, kvs, kvw, gates, ks_k, vs_t, kw_k, vw_t = _proj(x, w_in[0])
    conv_p = _conv_prompt(glu, conv_w[0], conv_b[0], conv_ln_g[0], conv_ln_b[0])
    a_s = glu[N_PROMPT:].reshape(DEC_SEQ, DEC_BATCH, CONV_CH)
    xt = jnp.concatenate([state_conv[0].transpose(1, 0, 2), a_s], axis=0)
    conv_s = _conv_sample(xt, conv_w[0], conv_b[0], conv_ln_g[0], conv_ln_b[0])
    cw = (cmp_pos_k[0], cmp_w1_k[0], cmp_w2_k[0], cmp_pos_v[0], cmp_w1_v[0], cmp_w2_v[0])
    kcv_p = _compress_prompt(kvc, cw)
    o_c, drop = _cmp_prompt(q, kcv_p)
    attn_p = _selw_prompt(q, gates, o_c, drop, ks_k, vs_t, kw_k, vw_t)
    attn_s = _nsa_sample_native(page_table, cache_cmp_kv, cache_slc_kv, state_win_kv, q[N_PROMPT:],
                                gates[N_PROMPT:], kvs[N_PROMPT:], kvw[N_PROMPT:], cw)
    conv = jnp.concatenate([conv_p, conv_s.reshape(N_SAMPLE, CONV_CH)], axis=0)
    attn = jnp.concatenate([attn_p, attn_s], axis=0)
    h1, route, counts = _out_route(conv, attn, x, w_out[0], ln1_g[0], ln1_b[0], w_grp[0], b_grp[0],
                                   w_exp[0], b_exp[0])
    blk_e, row_tok, nvalid, dest_tiles = _dispatch_indices(route, counts)
    y_rows = _moe(h1, blk_e, row_tok, nvalid, w_gate[0], w_up[0], w_down[0])
    y = _combine(dest_tiles, y_rows, h1, route, ln2_g[0], ln2_b[0])

    def batch_major(a):
        return a[N_PROMPT:].reshape(DEC_SEQ, DEC_BATCH, -1).transpose(1, 0, 2)

    kv_shape = (2, N_KV, HEAD_DIM)
    win_len = min(WINDOW, SEQ)
    y_prompt = y[:N_PROMPT].reshape(1, SEQ, D_MODEL)
    y_sample = batch_major(y)
    p_cmp = kvc[:N_PROMPT].reshape(1, 1, SEQ, *kv_shape)
    p_slc = kvs[:N_PROMPT].reshape(1, 1, SEQ, *kv_shape)
    p_win = kvw[N_PROMPT - win_len:N_PROMPT].reshape(1, 1, win_len, *kv_shape)
    p_conv = glu[N_PROMPT - (CONV_WIDTH - 1):N_PROMPT].reshape(1, 1, CONV_WIDTH - 1, CONV_CH)
    s_cmp = batch_major(kvc).reshape(1, DEC_BATCH, DEC_SEQ, *kv_shape)
    s_slc = batch_major(kvs).reshape(1, DEC_BATCH, DEC_SEQ, *kv_shape)
    s_win = jnp.concatenate([state_win_kv[0][:, DEC_SEQ:],
                             batch_major(kvw).reshape(DEC_BATCH, DEC_SEQ, *kv_shape)], axis=1)[None]
    s_conv = jnp.concatenate([state_conv[0][:, DEC_SEQ:], batch_major(glu)], axis=1)[None]
    return (y_prompt, y_sample, p_cmp, p_slc, p_win, p_conv, s_cmp, s_slc, s_win, s_conv)
```

```python
import functools

import numpy as np
import jax
import jax.numpy as jnp
from jax import lax
from jax.experimental import pallas as pl
from jax.experimental.pallas import tpu as pltpu

F32 = jnp.float32
BF16 = jnp.bfloat16

D_MODEL = 2048
SEQ = 8192
DEC_BATCH = 128
DEC_SEQ = 4
PAST_LEN = 2048
PAGE_SIZE = 128
CONV_CH = 1024
CONV_WIDTH = 31
N_HEADS = 16
HEAD_DIM = 64
N_KV = 4
GQA = 4
Q_WIDTH = 1024
KV_WIDTH = 256
COMP_BLOCK = 32
COMP_STRIDE = 16
CMP_HIDDEN = 128
SEL_BLOCK = 64
N_SEL = 16
WINDOW = 512
N_GROUPS = 8
EXPERTS_PER_GROUP = 8
N_EXPERTS = 64
TOP_K = 2
MOE_D_FF = 512
LN_EPS = 1e-5
ALPHA = 2.0 ** 0.25

N_PROMPT = SEQ
N_SAMPLE = DEC_BATCH * DEC_SEQ
N_TOK = N_PROMPT + N_SAMPLE

LANES = 128
VMEM_LIMIT = 48 * 1024 * 1024


def _cparams(*sem):
    return pltpu.CompilerParams(dimension_semantics=sem, vmem_limit_bytes=VMEM_LIMIT)


PROJ_TM = 512
PROJ_TN = 512
PROJ_STEPS = 9


def _proj_kernel(x_ref, w_ref, wg_ref, glu_ref, q_ref, kvc_ref, kvs_ref, kvw_ref, g_ref,
                 ksk_ref, vst_ref, kwk_ref, vwt_ref, xb_ref):
    j = pl.program_id(1)

    @pl.when(j == 0)
    def _():
        xb_ref[...] = x_ref[...].astype(BF16)

    acc = jnp.dot(xb_ref[...], w_ref[...], preferred_element_type=F32)

    @pl.when(j < 4)
    def _():
        glu_ref[...] = acc[:, :256] * jax.nn.sigmoid(acc[:, 256:])

    @pl.when((j == 4) | (j == 5))
    def _():
        q_ref[...] = (acc * (HEAD_DIM ** -0.5 * LOG2E)).astype(BF16)

    @pl.when(j == 6)
    def _():
        kvc_ref[...] = acc

    @pl.when(j == 7)
    def _():
        kvs_ref[...] = acc
        ksk_ref[...] = acc[:, :KV_WIDTH].astype(BF16)
        vst_ref[...] = acc[:, KV_WIDTH:].T.astype(BF16)

    @pl.when(j == 8)
    def _():
        kvw_ref[...] = acc
        kwk_ref[...] = acc[:, :KV_WIDTH].astype(BF16)
        vwt_ref[...] = acc[:, KV_WIDTH:].T.astype(BF16)
        g_ref[...] = jax.nn.sigmoid(jnp.dot(xb_ref[...], wg_ref[...], preferred_element_type=F32))


def _proj(x, w_in):
    n = x.shape[0]
    o1 = 2 * CONV_CH
    o2 = o1 + Q_WIDTH
    o3 = o2 + 6 * KV_WIDTH
    wa, wb = w_in[:, :CONV_CH], w_in[:, CONV_CH:o1]
    glu_cols = [jnp.concatenate([wa[:, 256 * s:256 * (s + 1)], wb[:, 256 * s:256 * (s + 1)]], axis=1)
                for s in range(4)]
    w_cat = jnp.concatenate(glu_cols + [w_in[:, o1:o3]], axis=1).astype(BF16)
    wg = w_in[:, o3:].reshape(D_MODEL, N_HEADS, 3).transpose(0, 2, 1).reshape(D_MODEL, 3 * N_HEADS)
    wg = jnp.pad(wg, ((0, 0), (0, LANES - 3 * N_HEADS))).astype(BF16)
    grid = (n // PROJ_TM, PROJ_STEPS)
    return pl.pallas_call(
        _proj_kernel,
        grid=grid,
        in_specs=[
            pl.BlockSpec((PROJ_TM, D_MODEL), lambda i, j: (i, 0)),
            pl.BlockSpec((D_MODEL, PROJ_TN), lambda i, j: (0, j)),
            pl.BlockSpec((D_MODEL, LANES), lambda i, j: (0, 0)),
        ],
        out_specs=[
            pl.BlockSpec((PROJ_TM, 256), lambda i, j: (i, jnp.minimum(j, 3))),
            pl.BlockSpec((PROJ_TM, 512), lambda i, j: (i, jnp.clip(j - 4, 0, 1))),
            pl.BlockSpec((PROJ_TM, 512), lambda i, j: (i, 0)),
            pl.BlockSpec((PROJ_TM, 512), lambda i, j: (i, 0)),
            pl.BlockSpec((PROJ_TM, 512), lambda i, j: (i, 0)),
            pl.BlockSpec((PROJ_TM, LANES), lambda i, j: (i, 0)),
            pl.BlockSpec((PROJ_TM, KV_WIDTH), lambda i, j: (i, 0)),
            pl.BlockSpec((KV_WIDTH, PROJ_TM), lambda i, j: (0, i)),
            pl.BlockSpec((PROJ_TM, KV_WIDTH), lambda i, j: (i, 0)),
            pl.BlockSpec((KV_WIDTH, PROJ_TM), lambda i, j: (0, i)),
        ],
        out_shape=[
            jax.ShapeDtypeStruct((n, CONV_CH), F32),
            jax.ShapeDtypeStruct((n, Q_WIDTH), BF16),
            jax.ShapeDtypeStruct((n, 2 * KV_WIDTH), F32),
            jax.ShapeDtypeStruct((n, 2 * KV_WIDTH), F32),
            jax.ShapeDtypeStruct((n, 2 * KV_WIDTH), F32),
            jax.ShapeDtypeStruct((n, LANES), F32),
            jax.ShapeDtypeStruct((n, KV_WIDTH), BF16),
            jax.ShapeDtypeStruct((KV_WIDTH, n), BF16),
            jax.ShapeDtypeStruct((n, KV_WIDTH), BF16),
            jax.ShapeDtypeStruct((KV_WIDTH, n), BF16),
        ],
        scratch_shapes=[pltpu.VMEM((PROJ_TM, D_MODEL), BF16)],
        compiler_params=_cparams("parallel", "arbitrary"),
        name="proj",
    )(x, w_cat, wg)


CONV_T = 256
CONV_HALO = 32
CONV_RC = 64
CONV_CC = 256


def _ln_silu(y, g, beta):
    mu = jnp.mean(y, axis=-1, keepdims=True)
    yc = y - mu
    var = jnp.mean(yc * yc, axis=-1, keepdims=True)
    z = yc * lax.rsqrt(var + LN_EPS) * g + beta
    return z * jax.nn.sigmoid(z)


def _conv_prompt_kernel(cur_ref, prev_ref, w_ref, b_ref, g_ref, beta_ref, o_ref, xp_ref, y_ref):
    i = pl.program_id(0)
    xp_ref[CONV_HALO:, :] = cur_ref[...]

    @pl.when(i == 0)
    def _():
        xp_ref[0:CONV_HALO, :] = jnp.zeros((CONV_HALO, CONV_CH), F32)

    @pl.when(i > 0)
    def _():
        xp_ref[0:CONV_HALO, :] = prev_ref[...]

    off = CONV_HALO - (CONV_WIDTH - 1)
    for c in range(CONV_CH // CONV_CC):
        cols = slice(c * CONV_CC, (c + 1) * CONV_CC)
        for r in range(CONV_T // CONV_RC):
            acc = jnp.zeros((CONV_RC, CONV_CC), F32)
            for k in range(CONV_WIDTH):
                r0 = r * CONV_RC + off + k
                acc = acc + w_ref[k:k + 1, cols] * xp_ref[r0:r0 + CONV_RC, cols]
            y_ref[r * CONV_RC:(r + 1) * CONV_RC, cols] = acc
    y = y_ref[...] + b_ref[...]
    o_ref[...] = _ln_silu(y, g_ref[...], beta_ref[...]).astype(BF16)


def _conv_prompt(a, conv_w, conv_b, ln_g, ln_b):
    wp = jnp.pad(conv_w, ((0, 1), (0, 0)))
    row = lambda v: v.reshape(1, CONV_CH)
    hb = CONV_T // CONV_HALO
    return pl.pallas_call(
        _conv_prompt_kernel,
        grid=(N_PROMPT // CONV_T,),
        in_specs=[
            pl.BlockSpec((CONV_T, CONV_CH), lambda i: (i, 0)),
            pl.BlockSpec((CONV_HALO, CONV_CH), lambda i: (jnp.maximum(i * hb - 1, 0), 0)),
            pl.BlockSpec((CONV_WIDTH + 1, CONV_CH), lambda i: (0, 0)),
            pl.BlockSpec((1, CONV_CH), lambda i: (0, 0)),
            pl.BlockSpec((1, CONV_CH), lambda i: (0, 0)),
            pl.BlockSpec((1, CONV_CH), lambda i: (0, 0)),
        ],
        out_specs=pl.BlockSpec((CONV_T, CONV_CH), lambda i: (i, 0)),
        out_shape=jax.ShapeDtypeStruct((N_PROMPT, CONV_CH), BF16),
        scratch_shapes=[pltpu.VMEM((CONV_T + CONV_HALO, CONV_CH), F32), pltpu.VMEM((CONV_T, CONV_CH), F32)],
        compiler_params=_cparams("parallel"),
        name="conv_prompt",
    )(a, a, wp, row(conv_b), row(ln_g), row(ln_b))


def _conv_sample_kernel(xt_ref, w_ref, b_ref, g_ref, beta_ref, o_ref):
    for t in range(DEC_SEQ):
        acc = jnp.zeros((DEC_BATCH, CONV_CH), F32)
        for k in range(CONV_WIDTH):
            acc = acc + w_ref[k:k + 1, :] * xt_ref[t + k]
        y = acc + b_ref[...]
        o_ref[t] = _ln_silu(y, g_ref[...], beta_ref[...]).astype(BF16)


def _conv_sample(xt, conv_w, conv_b, ln_g, ln_b):
    wp = jnp.pad(conv_w, ((0, 1), (0, 0)))
    row = lambda v: v.reshape(1, CONV_CH)
    npos = CONV_WIDTH - 1 + DEC_SEQ
    return pl.pallas_call(
        _conv_sample_kernel,
        grid=(1,),
        in_specs=[
            pl.BlockSpec((npos, DEC_BATCH, CONV_CH), lambda i: (0, 0, 0)),
            pl.BlockSpec((CONV_WIDTH + 1, CONV_CH), lambda i: (0, 0)),
            pl.BlockSpec((1, CONV_CH), lambda i: (0, 0)),
            pl.BlockSpec((1, CONV_CH), lambda i: (0, 0)),
            pl.BlockSpec((1, CONV_CH), lambda i: (0, 0)),
        ],
        out_specs=pl.BlockSpec((DEC_SEQ, DEC_BATCH, CONV_CH), lambda i: (0, 0, 0)),
        out_shape=jax.ShapeDtypeStruct((DEC_SEQ, DEC_BATCH, CONV_CH), BF16),
        compiler_params=_cparams("arbitrary"),
        name="conv_sample",
    )(xt, wp, row(conv_b), row(ln_g), row(ln_b))


HALF = HEAD_DIM
NEG_BIG = -1e30
CHUNKS_PER_PAGE = PAGE_SIZE // COMP_STRIDE
CHUNK_LANES = COMP_STRIDE * 2 * KV_WIDTH


LOG2E = 1.4426950408889634


def _alibi_slope(h):
    return float(2.0 ** (-8.0 * (h + 1) / N_HEADS) * LOG2E)


def _gelu_tanh(x):
    c = np.float32(np.sqrt(2.0 / np.pi))
    return 0.5 * x * (1.0 + jnp.tanh(c * (x + 0.044715 * (x * x * x))))


def _dot_nt(a, b, precision=None):
    return lax.dot_general(a, b, (((1,), (1,)), ((), ())), preferred_element_type=F32, precision=precision)


def _half_mask(shape, half):
    lane = lax.broadcasted_iota(jnp.int32, shape, len(shape) - 1)
    return (lane >= HALF) if half else (lane < HALF)


def _swap_halves(x):
    return pltpu.roll(x, HALF, axis=x.ndim - 1)


def _compress_weights(pos, w1, w2):
    eye2 = jnp.eye(2, dtype=F32)
    base = w1.reshape(2, COMP_STRIDE, HEAD_DIM, CMP_HIDDEN)
    w1p = jnp.einsum('rsde,hg->rshdge', base, eye2).reshape(2, COMP_STRIDE * 2 * HEAD_DIM, 2 * CMP_HIDDEN)
    pe = jnp.broadcast_to(pos.reshape(2, COMP_STRIDE, 1, HEAD_DIM), (2, COMP_STRIDE, 2, HEAD_DIM))
    pe = pe.reshape(2, 1, COMP_STRIDE * 2 * HEAD_DIM)
    w2p = jnp.einsum('ed,hg->hegd', w2, eye2).reshape(2 * CMP_HIDDEN, 2 * HEAD_DIM)
    return w1p.astype(BF16), pe, w2p.astype(BF16)


def _compress_pair(x, pe_ref, w1_ref, w2_ref):
    n = x.shape[0]
    a0 = jnp.dot((x + pe_ref[0]).astype(BF16), w1_ref[0], preferred_element_type=F32)
    a1 = jnp.dot((x + pe_ref[1]).astype(BF16), w1_ref[1], preferred_element_type=F32)
    hid = a0 + pltpu.roll(a1, n - 1, axis=0)
    return jnp.dot(_gelu_tanh(hid).astype(BF16), w2_ref[...], preferred_element_type=F32)


def _overlap_map(nc, ns, nc_pad, ns_pad):
    i = np.arange(nc_pad)[None, :]
    j = np.arange(ns_pad)[:, None]
    c_start, s_start = i * COMP_STRIDE, j * SEL_BLOCK
    m = (c_start <= s_start + SEL_BLOCK - 1) & (c_start + COMP_BLOCK - 1 >= s_start) & (i < nc) & (j < ns)
    return jnp.asarray(m.astype(np.float32))


def _select_blocks(imp_t, q_pos, n_iter):
    nb = imp_t.shape[0]
    blk = lax.broadcasted_iota(jnp.int32, imp_t.shape, 0)
    blk_f = blk.astype(F32)
    cur = q_pos >> 6
    valid = blk * SEL_BLOCK <= q_pos
    forced = (blk == 0) | (blk == cur) | (blk == cur - 1)
    score = jnp.where(valid, jnp.where(forced, 3e38, imp_t), -1.0)
    sel = jnp.zeros(imp_t.shape, F32)
    for _ in range(n_iter):
        mx = jnp.max(score, axis=0, keepdims=True)
        first = jnp.min(jnp.where(score == mx, blk_f, float(nb)), axis=0, keepdims=True)
        pick = (blk_f == first) & (mx >= 0.0)
        sel = jnp.where(pick, 1.0, sel)
        score = jnp.where(pick, -2.0, score)
    return sel


P_CHUNKS = SEQ // COMP_STRIDE
P_NC = (SEQ - COMP_BLOCK) // COMP_STRIDE + 1
P_NS = SEQ // SEL_BLOCK


def _compress_prompt_kernel(*refs):
    c_refs = refs[:COMP_STRIDE]
    pe_ref, w1_ref, w2_ref, o_ref = refs[COMP_STRIDE:]
    x = jnp.concatenate([r[...] for r in c_refs], axis=1)
    o_ref[...] = _compress_pair(x, pe_ref, w1_ref, w2_ref).astype(BF16)


def _compress_prompt(kvc, cw):
    pos_k, w1_k, w2_k, pos_v, w1_v, w2_v = cw
    wk, wv = _compress_weights(pos_k, w1_k, w2_k), _compress_weights(pos_v, w1_v, w2_v)
    w1p, pe, w2p = (jnp.stack([a, b]) for a, b in zip(wk, wv))
    chunks = kvc.reshape(N_TOK // COMP_STRIDE, CHUNK_LANES)
    c_specs = [pl.BlockSpec((P_CHUNKS, LANES), lambda g, s=s: (0, s * 4 + g)) for s in range(COMP_STRIDE)]
    return pl.pallas_call(
        _compress_prompt_kernel,
        grid=(4,),
        in_specs=c_specs + [
            pl.BlockSpec((None, 2, 1, 2048), lambda g: (g // 2, 0, 0, 0)),
            pl.BlockSpec((None, 2, 2048, 256), lambda g: (g // 2, 0, 0, 0)),
            pl.BlockSpec((None, 256, LANES), lambda g: (g // 2, 0, 0)),
        ],
        out_specs=pl.BlockSpec((P_CHUNKS, LANES), lambda g: (0, g)),
        out_shape=jax.ShapeDtypeStruct((P_CHUNKS, 4 * LANES), BF16),
        compiler_params=_cparams("parallel"),
        name="compress_prompt",
    )(*([chunks] * COMP_STRIDE), pe, w1p, w2p)


QB = 256
TK = 256


def _stack_gqa(q_ref, k):
    qa = q_ref[:, 256 * k:256 * k + LANES]
    qb = q_ref[:, 256 * k + LANES:256 * (k + 1)]
    qas = _swap_halves(qa.astype(F32)).astype(BF16)
    qbs = _swap_halves(qb.astype(F32)).astype(BF16)
    order = [qa, qas, qb, qbs] if k % 2 == 0 else [qas, qa, qbs, qb]
    return jnp.concatenate(order, axis=0)


def _pair_up(o0, o1, half):
    return o0 + _swap_halves(o1) if half == 0 else _swap_halves(o0) + o1


def _cmp_prompt_kernel(q_ref, kcv_ref, smap_ref, oc_ref, sel_ref):
    i = pl.program_id(0)
    q0 = i * QB
    q_pos_col = q0 + lax.broadcasted_iota(jnp.int32, (QB, 1), 0)
    c_end = lax.broadcasted_iota(jnp.int32, (1, P_CHUNKS), 1) * COMP_STRIDE + (COMP_BLOCK - 1)
    d_c = q_pos_col - c_end
    ok = d_c >= 0
    d_cf = d_c.astype(F32)
    q_pos_row = q0 + lax.broadcasted_iota(jnp.int32, (1, QB), 1)
    for k in range(N_KV):
        pair, half = k // 2, k % 2
        keep = _half_mask((P_CHUNKS, LANES), half)
        kc = jnp.where(keep, kcv_ref[:, LANES * pair:LANES * (pair + 1)], 0)
        vc = jnp.where(keep, kcv_ref[:, 2 * LANES + LANES * pair:2 * LANES + LANES * (pair + 1)], 0)
        s4 = _dot_nt(_stack_gqa(q_ref, k), kc)
        psum = jnp.zeros((QB, P_CHUNKS), F32)
        outs = []
        for g in range(GQA):
            s = s4[g * QB:(g + 1) * QB] - _alibi_slope(4 * k + g) * d_cf
            s = jnp.where(ok, s, NEG_BIG)
            m = jnp.max(s, axis=-1, keepdims=True)
            e = jnp.where(ok, jnp.exp2(s - m), 0.0)
            den = jnp.sum(e, axis=-1, keepdims=True)
            p = e / jnp.where(den > 0, den, 1.0)
            psum = psum + p
            outs.append(jnp.dot(p.astype(BF16), vc, preferred_element_type=F32))
        oc_ref[:, 256 * k:256 * k + LANES] = _pair_up(outs[0], outs[1], half)
        oc_ref[:, 256 * k + LANES:256 * (k + 1)] = _pair_up(outs[2], outs[3], half)
        imp_t = _dot_nt(smap_ref[...], psum, precision=lax.Precision.HIGHEST)
        sel_t = _select_blocks(imp_t, q_pos_row, N_SEL)
        sel_ref[:, LANES * k:LANES * (k + 1)] = (1.0 - sel_t.T).astype(BF16)


def _cmp_prompt(q, kcv):
    smap_t = _overlap_map(P_NC, P_NS, P_CHUNKS, P_NS)
    return pl.pallas_call(
        _cmp_prompt_kernel,
        grid=(N_PROMPT // QB,),
        in_specs=[
            pl.BlockSpec((QB, Q_WIDTH), lambda i: (i, 0)),
            pl.BlockSpec((P_CHUNKS, 4 * LANES), lambda i: (0, 0)),
            pl.BlockSpec((P_NS, P_CHUNKS), lambda i: (0, 0)),
        ],
        out_specs=[
            pl.BlockSpec((QB, Q_WIDTH), lambda i: (i, 0)),
            pl.BlockSpec((QB, N_KV * P_NS), lambda i: (i, 0)),
        ],
        out_shape=[
            jax.ShapeDtypeStruct((N_PROMPT, Q_WIDTH), F32),
            jax.ShapeDtypeStruct((N_PROMPT, N_KV * P_NS), BF16),
        ],
        compiler_params=_cparams("parallel"),
        name="nsa_cmp_prompt",
    )(q, kcv, smap_t)


def _softmax_tile(s4, bias_fn, k, m_ref, l_ref, acc_ref, v_tile, first):
    ps = []
    for g in range(GQA):
        rows = slice(g * QB, (g + 1) * QB)
        s = s4[rows] + bias_fn(g)
        m_tile = jnp.max(s, axis=-1, keepdims=True)
        if first:
            m_new = m_tile
        else:
            m_old = m_ref[k, rows]
            m_new = jnp.maximum(m_old, m_tile)
        p = jnp.exp(s - m_new)
        row_sum = jnp.sum(p, axis=-1, keepdims=True)
        if first:
            l_ref[k, rows] = row_sum
        else:
            scale_old = jnp.exp(m_old - m_new)
            l_ref[k, rows] = scale_old * l_ref[k, rows] + row_sum
            acc_ref[k, rows] = scale_old * acc_ref[k, rows]
        m_ref[k, rows] = m_new
        ps.append(p.astype(BF16))
    pv = jnp.dot(jnp.concatenate(ps, axis=0), v_tile, preferred_element_type=F32)
    if first:
        acc_ref[k] = pv
    else:
        acc_ref[k] = acc_ref[k] + pv


def _kv_tile(kv_ref, row0, k):
    pair, half = k // 2, k % 2
    keep = _half_mask((TK, LANES), half)
    kt = jnp.where(keep, kv_ref[pl.ds(row0, TK), LANES * pair:LANES * (pair + 1)], 0)
    vt = jnp.where(keep, kv_ref[pl.ds(row0, TK), 2 * LANES + LANES * pair:2 * LANES + LANES * (pair + 1)], 0)
    return kt, vt


def _sel_prompt_kernel(q_ref, g_ref, oc_ref, sel_ref, ks_ref, kw_ref, o_ref,
                       q4_ref, m_ref, l_ref, acc_ref, os_ref):
    i = pl.program_id(0)
    q0 = pl.multiple_of(i * QB, QB)
    for k in range(N_KV):
        q4_ref[k] = _stack_gqa(q_ref, k)
    q_pos = q0 + lax.broadcasted_iota(jnp.int32, (QB, 1), 0)
    lane_key = lax.broadcasted_iota(jnp.int32, (1, TK), 1)
    blk_row = lax.broadcasted_iota(jnp.int32, (P_NS, 1), 0)

    def dist(row0):
        return (q_pos - (row0 + lane_key)).astype(F32)

    def sel_mask(row0, k):
        expand = jnp.where(blk_row == (row0 >> 6) + (lane_key >> 6), 1.0, 0.0).astype(BF16)
        return jnp.dot(sel_ref[:, LANES * k:LANES * (k + 1)], expand, preferred_element_type=F32)

    def selected_tile(row0, first):
        d = dist(row0)
        causal_bias = jnp.where(d >= 0, 0.0, NEG_BIG) if first else None
        for k in range(N_KV):
            kt, vt = _kv_tile(ks_ref, row0, k)
            s4 = _dot_nt(q4_ref[k], kt)
            bias = (sel_mask(row0, k) - 1.0) * (-NEG_BIG)
            if first:
                bias = bias + causal_bias
            _softmax_tile(s4, lambda g: bias - _alibi_slope(4 * k + g) * d, k, m_ref, l_ref, acc_ref, vt, first)

    selected_tile(q0, True)

    def body(t, carry):
        selected_tile(pl.multiple_of(t * TK, TK), False)
        return carry

    lax.fori_loop(0, i, body, 0)
    for k in range(N_KV):
        os_ref[k] = acc_ref[k] / l_ref[k]

    def window_tile(row0, first, band):
        d = dist(row0)
        if first:
            mask_bias = jnp.where(d >= 0, 0.0, NEG_BIG)
        elif band:
            mask_bias = jnp.where(d <= WINDOW, 0.0, NEG_BIG)
        else:
            mask_bias = jnp.zeros_like(d)
        for k in range(N_KV):
            kt, vt = _kv_tile(kw_ref, row0, k)
            s4 = _dot_nt(q4_ref[k], kt)
            _softmax_tile(s4, lambda g: mask_bias - _alibi_slope(4 * k + g) * d, k, m_ref, l_ref, acc_ref, vt,
                          first)

    window_tile(q0, True, False)

    @pl.when(i >= 1)
    def _():
        window_tile(pl.multiple_of(q0 - TK, TK), False, False)

    @pl.when(i >= 2)
    def _():
        window_tile(pl.multiple_of(q0 - 2 * TK, TK), False, True)

    low = _half_mask((QB, LANES), 0)
    for k in range(N_KV):
        half = k % 2
        contrib = []
        for g in range(GQA):
            h = 4 * k + g
            rows = slice(g * QB, (g + 1) * QB)
            o_s = os_ref[k, rows] * g_ref[:, N_HEADS + h:N_HEADS + h + 1]
            o_w = (acc_ref[k, rows] / l_ref[k, rows]) * g_ref[:, 2 * N_HEADS + h:2 * N_HEADS + h + 1]
            contrib.append(o_s + o_w)
        for pr in range(2):
            h0 = 4 * k + 2 * pr
            gate_c = jnp.where(low, g_ref[:, h0:h0 + 1], g_ref[:, h0 + 1:h0 + 2])
            cols = slice(256 * k + LANES * pr, 256 * k + LANES * (pr + 1))
            o = gate_c * oc_ref[:, cols] + _pair_up(contrib[2 * pr], contrib[2 * pr + 1], half)
            o_ref[:, cols] = o.astype(BF16)


def _sel_prompt(q, gates, o_c, sel, kvs_b, kvw_b):
    resident = lambda shape: pl.BlockSpec(shape, lambda i: (0, 0), pipeline_mode=pl.Buffered(1))
    return pl.pallas_call(
        _sel_prompt_kernel,
        grid=(N_PROMPT // QB,),
        in_specs=[
            pl.BlockSpec((QB, Q_WIDTH), lambda i: (i, 0)),
            pl.BlockSpec((QB, LANES), lambda i: (i, 0)),
            pl.BlockSpec((QB, Q_WIDTH), lambda i: (i, 0)),
            pl.BlockSpec((QB, N_KV * P_NS), lambda i: (i, 0)),
            resident((N_PROMPT, 2 * KV_WIDTH)),
            resident((N_PROMPT, 2 * KV_WIDTH)),
        ],
        out_specs=pl.BlockSpec((QB, Q_WIDTH), lambda i: (i, 0)),
        out_shape=jax.ShapeDtypeStruct((N_PROMPT, Q_WIDTH), BF16),
        scratch_shapes=[
            pltpu.VMEM((N_KV, GQA * QB, LANES), BF16),
            pltpu.VMEM((N_KV, GQA * QB, 1), F32),
            pltpu.VMEM((N_KV, GQA * QB, 1), F32),
            pltpu.VMEM((N_KV, GQA * QB, LANES), F32),
            pltpu.VMEM((N_KV, GQA * QB, LANES), F32),
        ],
        compiler_params=_cparams("parallel"),
        name="nsa_sel_prompt",
    )(q, gates, o_c, sel, kvs_b, kvw_b)


TKT = 256
POS_PIECES = 3


def _bf16_pieces(x, n):
    out = []
    for _ in range(n):
        p = float(np.float32(x).astype(jnp.bfloat16))
        out.append(p)
        x = x - p
    return out


def _position_lanes():
    pos = np.arange(SEQ)
    out = np.zeros((2, SEQ, LANES), np.float32)
    for half in range(2):
        base = HALF * (1 - half)
        for j in range(POS_PIECES):
            out[half, :, base + j] = (pos >> 6) * SEL_BLOCK
            out[half, :, base + POS_PIECES + j] = pos & (SEL_BLOCK - 1)
    return jnp.asarray(out, BF16)


def _block_drop_rows():
    m = np.where(np.arange(P_NS)[None, :] == (np.arange(SEQ)[:, None] // SEL_BLOCK), NEG_BIG, 0.0)
    return jnp.asarray(m.astype(np.float32), BF16)


def _selw_kernel(q_ref, g_ref, oc_ref, drop_ref, ksk_ref, vst_ref, kwk_ref, vwt_ref, pos_ref, e_ref, o_ref,
                 qa_ref, m_ref, acc_ref, os_ref):
    i = pl.program_id(0)
    q0 = pl.multiple_of(i * QB, QB)
    lane1 = lax.broadcasted_iota(jnp.int32, (1, LANES), 1)

    for k in range(N_KV):
        half = k % 2
        base = HALF * (1 - half)
        qa = q_ref[:, 256 * k:256 * k + LANES]
        qb = q_ref[:, 256 * k + LANES:256 * (k + 1)]
        qas = _swap_halves(qa.astype(F32)).astype(BF16)
        qbs = _swap_halves(qb.astype(F32)).astype(BF16)
        order = [qa, qas, qb, qbs] if half == 0 else [qas, qa, qbs, qb]
        own = _half_mask((QB, LANES), half)
        parts = []
        for g in range(GQA):
            sv = jnp.zeros((1, LANES), F32)
            pieces = _bf16_pieces(_alibi_slope(4 * k + g), POS_PIECES)
            for j, piece in enumerate(pieces + pieces):
                sv = jnp.where(lane1 == base + j, piece, sv)
            parts.append(jnp.where(own, order[g], sv.astype(BF16)))
        drop = drop_ref[:, LANES * k:LANES * (k + 1)]
        qa_ref[k] = jnp.concatenate([jnp.concatenate(parts, axis=0), jnp.concatenate([drop] * GQA, axis=0)], axis=1)

    key_row = lax.broadcasted_iota(jnp.int32, (TKT, 1), 0)
    q_lane = lax.broadcasted_iota(jnp.int32, (1, GQA * QB), 1) & (QB - 1)
    vrow = lax.broadcasted_iota(jnp.int32, (LANES, 1), 0)

    def tile(kk_ref, vt_ref, row0, selected, mode, first):
        if mode is not None:
            d = (q0 + q_lane) - (row0 + key_row)
            ok = (d >= 0) if mode == "causal" else (d <= WINDOW)
        for k in range(N_KV):
            pair, half = k // 2, k % 2
            own = _half_mask((TKT, LANES), half)
            ka = jnp.where(own, kk_ref[pl.ds(row0, TKT), LANES * pair:LANES * (pair + 1)],
                           pos_ref[half, pl.ds(row0, TKT), :])
            if selected:
                st = _dot_nt(jnp.concatenate([ka, e_ref[pl.ds(row0, TKT), :]], axis=1), qa_ref[k])
            else:
                st = _dot_nt(ka, qa_ref[k, :, 0:LANES])
            if mode is not None:
                st = jnp.where(ok, st, NEG_BIG)
            m_tile = jnp.max(st, axis=0, keepdims=True)
            own_rows = (vrow >= HALF) if half else (vrow < HALF)
            ones_row = jnp.where(vrow == HALF * (1 - half), 1.0, 0.0).astype(BF16)
            va = jnp.where(own_rows, vt_ref[LANES * pair:LANES * (pair + 1), pl.ds(row0, TKT)], ones_row)
            if first:
                m_new = m_tile
            else:
                m_old = m_ref[k]
                m_new = jnp.maximum(m_old, m_tile)
            p = jnp.exp2(st - m_new).astype(BF16)
            pv = jnp.dot(va, p, preferred_element_type=F32)
            if first:
                acc_ref[k] = pv
            else:
                acc_ref[k] = jnp.exp2(m_old - m_new) * acc_ref[k] + pv
            m_ref[k] = m_new

    def normalised(k):
        half = k % 2
        den_row = HALF * (1 - half)
        return acc_ref[k] / acc_ref[k, den_row:den_row + 1, :]

    tile(ksk_ref, vst_ref, q0, True, "causal", True)

    def body(t, carry):
        tile(ksk_ref, vst_ref, pl.multiple_of(t * TKT, TKT), True, None, False)
        return carry

    lax.fori_loop(0, i, body, 0)
    for k in range(N_KV):
        os_ref[k] = normalised(k)

    tile(kwk_ref, vwt_ref, q0, False, "causal", True)

    @pl.when(i >= 1)
    def _():
        tile(kwk_ref, vwt_ref, pl.multiple_of(q0 - TKT, TKT), False, None, False)

    @pl.when(i >= 2)
    def _():
        tile(kwk_ref, vwt_ref, pl.multiple_of(q0 - 2 * TKT, TKT), False, "band", False)

    g_t = g_ref[...].T
    low = _half_mask((QB, LANES), 0)
    for k in range(N_KV):
        half = k % 2
        o_w = normalised(k)
        contrib = []
        for g in range(GQA):
            h = 4 * k + g
            cols = slice(g * QB, (g + 1) * QB)
            c_t = (os_ref[k, :, cols] * g_t[N_HEADS + h:N_HEADS + h + 1, :]
                   + o_w[:, cols] * g_t[2 * N_HEADS + h:2 * N_HEADS + h + 1, :])
            contrib.append(c_t.T)
        for pr in range(2):
            h0 = 4 * k + 2 * pr
            c0, c1 = contrib[2 * pr], contrib[2 * pr + 1]
            both = jnp.where(low, c0, _swap_halves(c1)) if half == 0 else jnp.where(low, _swap_halves(c0), c1)
            gate_c = jnp.where(low, g_ref[:, h0:h0 + 1], g_ref[:, h0 + 1:h0 + 2])
            cols = slice(256 * k + LANES * pr, 256 * k + LANES * (pr + 1))
            o_ref[:, cols] = (gate_c * oc_ref[:, cols] + both).astype(BF16)


def _selw_prompt(q, gates, o_c, drop, ks_k, vs_t, kw_k, vw_t):
    resident = lambda shape: pl.BlockSpec(shape, lambda i: (0,) * len(shape), pipeline_mode=pl.Buffered(1))
    return pl.pallas_call(
        _selw_kernel,
        grid=(N_PROMPT // QB,),
        in_specs=[
            pl.BlockSpec((QB, Q_WIDTH), lambda i: (i, 0)),
            pl.BlockSpec((QB, LANES), lambda i: (i, 0)),
            pl.BlockSpec((QB, Q_WIDTH), lambda i: (i, 0)),
            pl.BlockSpec((QB, N_KV * P_NS), lambda i: (i, 0)),
            resident((N_PROMPT, KV_WIDTH)),
            resident((KV_WIDTH, N_PROMPT)),
            resident((N_PROMPT, KV_WIDTH)),
            resident((KV_WIDTH, N_PROMPT)),
            resident((2, SEQ, LANES)),
            resident((SEQ, LANES)),
        ],
        out_specs=pl.BlockSpec((QB, Q_WIDTH), lambda i: (i, 0)),
        out_shape=jax.ShapeDtypeStruct((N_PROMPT, Q_WIDTH), BF16),
        scratch_shapes=[
            pltpu.VMEM((N_KV, GQA * QB, 2 * LANES), BF16),
            pltpu.VMEM((N_KV, 1, GQA * QB), F32),
            pltpu.VMEM((N_KV, LANES, GQA * QB), F32),
            pltpu.VMEM((N_KV, LANES, GQA * QB), F32),
        ],
        compiler_params=_cparams("parallel"),
        name="nsa_selw_prompt",
    )(q, gates, o_c, drop, ks_k, vs_t, kw_k, vw_t, _position_lanes(), _block_drop_rows())


N_PAGES = PAST_LEN // PAGE_SIZE
S_CHUNKS = PAST_LEN // COMP_STRIDE
S_NC = (PAST_LEN + DEC_SEQ - COMP_BLOCK) // COMP_STRIDE + 1
S_NS = -(-(PAST_LEN + DEC_SEQ) // SEL_BLOCK)
S_KEYS = 17 * LANES
S_NEW = 16
WIN_BUF = min(WINDOW, PAST_LEN)
S_WKEYS = 5 * LANES
ROWS16 = GQA * DEC_SEQ


def _row_softmax(s, ok):
    s = jnp.where(ok, s, NEG_BIG)
    m = jnp.max(s, axis=-1, keepdims=True)
    e = jnp.where(ok, jnp.exp2(s - m), 0.0)
    den = jnp.sum(e, axis=-1, keepdims=True)
    return e / jnp.where(den > 0, den, 1.0)


def _nsa_sample_kernel(pt_ref, *refs):
    cp = refs[:N_PAGES]
    sp = refs[N_PAGES:2 * N_PAGES]
    (win_ref, news_ref, neww_ref, q_ref, g_ref, pe_ref, w1_ref, w2_ref, smap_ref, expand_ref,
     o_ref, ks_ref, kw_ref) = refs[2 * N_PAGES:]
    del pt_ref

    for p in range(N_PAGES):
        ks_ref[PAGE_SIZE * p:PAGE_SIZE * (p + 1), :] = sp[p][...].astype(BF16)
    ks_ref[PAST_LEN:PAST_LEN + S_NEW, :] = news_ref[...].astype(BF16)
    ks_ref[PAST_LEN + S_NEW:, :] = jnp.zeros((S_KEYS - PAST_LEN - S_NEW, 2 * KV_WIDTH), BF16)
    kw_ref[0:WIN_BUF, :] = win_ref[...].astype(BF16)
    kw_ref[WIN_BUF:WIN_BUF + S_NEW, :] = neww_ref[...].astype(BF16)
    kw_ref[WIN_BUF + S_NEW:, :] = jnp.zeros((S_WKEYS - WIN_BUF - S_NEW, 2 * KV_WIDTH), BF16)

    kcv = []
    for kv in range(2):
        for pair in range(2):
            base = kv * KV_WIDTH + pair * LANES
            x = jnp.concatenate(
                [jnp.concatenate([cp[p][:, s * 2 * KV_WIDTH + base:s * 2 * KV_WIDTH + base + LANES]
                                  for p in range(N_PAGES)], axis=0) for s in range(COMP_STRIDE)], axis=1)
            kcv.append(_compress_pair(x, pe_ref.at[kv], w1_ref.at[kv], w2_ref.at[kv]).astype(BF16))

    row = lax.broadcasted_iota(jnp.int32, (ROWS16, 1), 0)
    q_pos = PAST_LEN + (row & (DEC_SEQ - 1))
    grp = row >> 2

    def slopes(k):
        s = jnp.full((ROWS16, 1), _alibi_slope(4 * k), F32)
        for g in range(1, GQA):
            s = jnp.where(grp == g, _alibi_slope(4 * k + g), s)
        return s

    c_end = lax.broadcasted_iota(jnp.int32, (1, S_CHUNKS), 1) * COMP_STRIDE + (COMP_BLOCK - 1)
    d_c = q_pos - c_end
    o_cmp, psums = [], []
    for k in range(N_KV):
        s = _dot_nt(q_ref[k], kcv[k // 2]) - slopes(k) * d_c.astype(F32)
        p = _row_softmax(s, d_c >= 0)
        o_cmp.append(jnp.dot(p.astype(BF16), kcv[2 + k // 2], preferred_element_type=F32))
        ps = p
        for g in range(1, GQA):
            ps = ps + pltpu.roll(p, DEC_SEQ * g, axis=0)
        psums.append(ps)
    psum_all = jnp.concatenate(psums + [jnp.zeros((LANES - N_KV * ROWS16, S_CHUNKS), F32)], axis=0)
    imp_t = _dot_nt(smap_ref[...], psum_all, precision=lax.Precision.HIGHEST)
    col = lax.broadcasted_iota(jnp.int32, (1, LANES), 1)
    sel_t = _select_blocks(imp_t, PAST_LEN + (col & (DEC_SEQ - 1)), N_SEL)
    sel = sel_t.T.astype(BF16)

    key_s = lax.broadcasted_iota(jnp.int32, (1, S_KEYS), 1)
    d_s = q_pos - key_s
    idx_w = lax.broadcasted_iota(jnp.int32, (1, S_WKEYS), 1)
    d_w = q_pos - (PAST_LEN - WIN_BUF + idx_w)
    ok_w = (d_w >= 0) & (d_w <= WINDOW)
    for k in range(N_KV):
        pair = k // 2
        kcol = slice(LANES * pair, LANES * (pair + 1))
        vcol = slice(2 * LANES + LANES * pair, 2 * LANES + LANES * (pair + 1))
        chosen = jnp.dot(sel[ROWS16 * k:ROWS16 * (k + 1)], expand_ref[...], preferred_element_type=F32)
        s = _dot_nt(q_ref[k], ks_ref[:, kcol]) - slopes(k) * d_s.astype(F32)
        p = _row_softmax(s, (chosen > 0.5) & (d_s >= 0))
        o_sel = jnp.dot(p.astype(BF16), ks_ref[:, vcol], preferred_element_type=F32)
        s = _dot_nt(q_ref[k], kw_ref[:, kcol]) - slopes(k) * d_w.astype(F32)
        p = _row_softmax(s, ok_w)
        o_win = jnp.dot(p.astype(BF16), kw_ref[:, vcol], preferred_element_type=F32)
        o_ref[k] = (g_ref[k][:, 0:1] * o_cmp[k] + g_ref[k][:, 1:2] * o_sel + g_ref[k][:, 2:3] * o_win)


def _nsa_sample(page_table, cache_c, cache_s, win_prev, q_s, gates_s, kvs_new, kvw_new, cw):
    pos_k, w1_k, w2_k, pos_v, w1_v, w2_v = cw
    wk, wv = _compress_weights(pos_k, w1_k, w2_k), _compress_weights(pos_v, w1_v, w2_v)
    w1p, pe, w2p = (jnp.stack([a, b]) for a, b in zip(wk, wv))
    n_phys = cache_c.shape[0]
    pages_c = cache_c.reshape(n_phys, CHUNKS_PER_PAGE, CHUNK_LANES)
    pages_s = cache_s.reshape(n_phys, PAGE_SIZE, 2 * KV_WIDTH)
    win = win_prev.reshape(DEC_BATCH, WIN_BUF, 2 * KV_WIDTH)
    half_of_k = (np.arange(N_KV) % 2)[:, None] == np.arange(2)[None, :]
    half_sel = jnp.asarray(half_of_k.reshape(1, N_KV, 1, 2, 1))
    qh = q_s.reshape(DEC_SEQ, DEC_BATCH, N_KV, GQA, HEAD_DIM).transpose(1, 2, 3, 0, 4)
    qh = qh.reshape(DEC_BATCH, N_KV, ROWS16, 1, HEAD_DIM)
    q16 = jnp.where(half_sel, qh, jnp.zeros((), BF16)).reshape(DEC_BATCH, N_KV, ROWS16, LANES)
    gh = gates_s[:, :3 * N_HEADS].reshape(DEC_SEQ, DEC_BATCH, 3, N_KV, GQA).transpose(1, 3, 4, 0, 2)
    g16 = jnp.pad(gh.reshape(DEC_BATCH, N_KV, ROWS16, 3), ((0, 0), (0, 0), (0, 0), (0, LANES - 3)))

    def new_rows(a):
        a = a.reshape(DEC_SEQ, DEC_BATCH, 2 * KV_WIDTH).transpose(1, 0, 2)
        return jnp.pad(a, ((0, 0), (0, S_NEW - DEC_SEQ), (0, 0)))

    smap_t = _overlap_map(S_NC, S_NS, S_CHUNKS, LANES)
    expand = jnp.asarray((np.arange(LANES)[:, None] == (np.arange(S_KEYS)[None, :] // SEL_BLOCK)), BF16)
    pt = page_table.reshape(-1).astype(jnp.int32)
    c_specs = [pl.BlockSpec((None, CHUNKS_PER_PAGE, CHUNK_LANES), lambda b, pt, p=p: (pt[b * N_PAGES + p], 0, 0))
               for p in range(N_PAGES)]
    s_specs = [pl.BlockSpec((None, PAGE_SIZE, 2 * KV_WIDTH), lambda b, pt, p=p: (pt[b * N_PAGES + p], 0, 0))
               for p in range(N_PAGES)]
    per_b3 = lambda shape: pl.BlockSpec((None,) + shape, lambda b, pt: (b, 0, 0))
    per_b4 = lambda shape: pl.BlockSpec((None,) + shape, lambda b, pt: (b, 0, 0, 0))
    const = lambda shape: pl.BlockSpec(shape, lambda b, pt: (0,) * len(shape))
    o16 = pl.pallas_call(
        _nsa_sample_kernel,
        grid_spec=pltpu.PrefetchScalarGridSpec(
            num_scalar_prefetch=1,
            grid=(DEC_BATCH,),
            in_specs=c_specs + s_specs + [
                per_b3((WIN_BUF, 2 * KV_WIDTH)),
                per_b3((S_NEW, 2 * KV_WIDTH)),
                per_b3((S_NEW, 2 * KV_WIDTH)),
                per_b4((N_KV, ROWS16, LANES)),
                per_b4((N_KV, ROWS16, LANES)),
                const((2, 2, 1, 2048)),
                const((2, 2, 2048, 256)),
                const((2, 256, LANES)),
                const((LANES, S_CHUNKS)),
                const((LANES, S_KEYS)),
            ],
            out_specs=per_b4((N_KV, ROWS16, LANES)),
            scratch_shapes=[pltpu.VMEM((S_KEYS, 2 * KV_WIDTH), BF16), pltpu.VMEM((S_WKEYS, 2 * KV_WIDTH), BF16)],
        ),
        out_shape=jax.ShapeDtypeStruct((DEC_BATCH, N_KV, ROWS16, LANES), F32),
        compiler_params=_cparams("parallel"),
        name="nsa_sample",
    )(pt, *([pages_c] * N_PAGES), *([pages_s] * N_PAGES), win, new_rows(kvs_new), new_rows(kvw_new),
      q16, g16, pe, w1p, w2p, smap_t, expand)
    o = o16.reshape(DEC_BATCH, N_KV, GQA, DEC_SEQ, 2, HEAD_DIM)
    o = jnp.stack([o[:, k, :, :, k % 2, :] for k in range(N_KV)], axis=1)
    return o.transpose(3, 0, 1, 2, 4).reshape(N_SAMPLE, Q_WIDTH).astype(BF16)


HEADS8 = 2 * N_KV
PAGE_FLAT = PAGE_SIZE * HEADS8
S_VMEM_LIMIT = 58 * 1024 * 1024


def _slot_rows(ref, slot, n_rows):
    return ref.reshape(n_rows * HEADS8, HEAD_DIM)[pl.ds(slot, n_rows, stride=HEADS8), :]


def _row_tile(ref, r, n_rows):
    return ref.reshape(n_rows * HEADS8, HEAD_DIM)[HEADS8 * r:HEADS8 * (r + 1), :]


def _nsa_sample_native_kernel(pt_ref, *refs):
    cp = refs[:N_PAGES]
    sp = refs[N_PAGES:2 * N_PAGES]
    (win_ref, news_ref, neww_ref, q_ref, g_ref, pe_ref, w1_ref, w2_ref, smap_ref, expand_ref,
     o_ref, ks_ref, kw_ref) = refs[2 * N_PAGES:]
    del pt_ref

    for c in range(HEADS8):
        for p in range(N_PAGES):
            ks_ref[c, PAGE_SIZE * p:PAGE_SIZE * (p + 1), :] = _slot_rows(sp[p], c, PAGE_SIZE).astype(BF16)
        ks_ref[c, PAST_LEN:PAST_LEN + S_NEW, :] = news_ref[c].astype(BF16)
        ks_ref[c, PAST_LEN + S_NEW:, :] = jnp.zeros((S_KEYS - PAST_LEN - S_NEW, HEAD_DIM), BF16)
        kw_ref[c, 0:WIN_BUF, :] = _slot_rows(win_ref, c, WIN_BUF).astype(BF16)
        kw_ref[c, WIN_BUF:WIN_BUF + S_NEW, :] = neww_ref[c].astype(BF16)
        kw_ref[c, WIN_BUF + S_NEW:, :] = jnp.zeros((S_WKEYS - WIN_BUF - S_NEW, HEAD_DIM), BF16)

    cols = []
    for s in range(COMP_STRIDE):
        cols.append(jnp.concatenate(
            [_row_tile(cp[p], COMP_STRIDE * j + s, PAGE_SIZE)
             for p in range(N_PAGES) for j in range(CHUNKS_PER_PAGE)], axis=0))
    xb = jnp.concatenate(cols, axis=1).astype(BF16)
    is_k = (lax.broadcasted_iota(jnp.int32, (S_CHUNKS * HEADS8, 1), 0) & (HEADS8 - 1)) < N_KV
    acts = []
    for r in range(2):
        bias = jnp.dot(pe_ref[r].astype(BF16), w1_ref[r], preferred_element_type=F32)
        a = jnp.dot(xb, w1_ref[r], preferred_element_type=F32)
        acts.append(jnp.where(is_k, a[:, :CMP_HIDDEN] + bias[0:1, :CMP_HIDDEN],
                              a[:, CMP_HIDDEN:] + bias[1:2, CMP_HIDDEN:]))
    hid = acts[0] + jnp.concatenate([acts[1][HEADS8:], acts[1][:HEADS8]], axis=0)
    gl = _gelu_tanh(hid).astype(BF16)
    kc_all = jnp.where(is_k, jnp.dot(gl, w2_ref[0], preferred_element_type=F32),
                       jnp.dot(gl, w2_ref[1], preferred_element_type=F32)).astype(BF16)

    row = lax.broadcasted_iota(jnp.int32, (ROWS16, 1), 0)
    q_pos = PAST_LEN + (row & (DEC_SEQ - 1))
    grp = row >> 2

    def slopes(k):
        s = jnp.full((ROWS16, 1), _alibi_slope(4 * k), F32)
        for g in range(1, GQA):
            s = jnp.where(grp == g, _alibi_slope(4 * k + g), s)
        return s

    ccol = lax.broadcasted_iota(jnp.int32, (1, S_CHUNKS * HEADS8), 1)
    d_c = q_pos - ((ccol >> 3) * COMP_STRIDE + (COMP_BLOCK - 1))
    slot = ccol & (HEADS8 - 1)
    o_cmp, psums = [], []
    for k in range(N_KV):
        s = _dot_nt(q_ref[k], kc_all) - slopes(k) * d_c.astype(F32)
        p = _row_softmax(s, (d_c >= 0) & (slot == k))
        o_cmp.append(jnp.dot(pltpu.roll(p, N_KV, axis=1).astype(BF16), kc_all, preferred_element_type=F32))
        ps = p
        for g in range(1, GQA):
            ps = ps + pltpu.roll(p, DEC_SEQ * g, axis=0)
        psums.append(ps)
    psum_all = jnp.concatenate(psums + [jnp.zeros((LANES - N_KV * ROWS16, S_CHUNKS * HEADS8), F32)], axis=0)
    imp_t = _dot_nt(smap_ref[...], psum_all, precision=lax.Precision.HIGHEST)
    col = lax.broadcasted_iota(jnp.int32, (1, LANES), 1)
    sel_t = _select_blocks(imp_t, PAST_LEN + (col & (DEC_SEQ - 1)), N_SEL)
    sel = sel_t.T.astype(BF16)

    key_s = lax.broadcasted_iota(jnp.int32, (1, S_KEYS), 1)
    d_s = q_pos - key_s
    idx_w = lax.broadcasted_iota(jnp.int32, (1, S_WKEYS), 1)
    d_w = q_pos - (PAST_LEN - WIN_BUF + idx_w)
    ok_w = (d_w >= 0) & (d_w <= WINDOW)
    for k in range(N_KV):
        chosen = jnp.dot(sel[ROWS16 * k:ROWS16 * (k + 1)], expand_ref[...], preferred_element_type=F32)
        s = _dot_nt(q_ref[k], ks_ref[k]) - slopes(k) * d_s.astype(F32)
        p = _row_softmax(s, (chosen > 0.5) & (d_s >= 0))
        o_sel = jnp.dot(p.astype(BF16), ks_ref[N_KV + k], preferred_element_type=F32)
        s = _dot_nt(q_ref[k], kw_ref[k]) - slopes(k) * d_w.astype(F32)
        p = _row_softmax(s, ok_w)
        o_win = jnp.dot(p.astype(BF16), kw_ref[N_KV + k], preferred_element_type=F32)
        o_ref[k] = (g_ref[k][:, 0:1] * o_cmp[k] + g_ref[k][:, 1:2] * o_sel + g_ref[k][:, 2:3] * o_win)


def _nsa_sample_native(page_table, cache_c, cache_s, win_prev, q_s, gates_s, kvs_new, kvw_new, cw):
    pos_k, w1_k, w2_k, pos_v, w1_v, w2_v = cw
    flat = COMP_STRIDE * HEAD_DIM
    w1 = jnp.concatenate([w1_k.reshape(2, flat, CMP_HIDDEN), w1_v.reshape(2, flat, CMP_HIDDEN)], axis=2).astype(BF16)
    pe = jnp.pad(jnp.stack([pos_k.reshape(2, flat), pos_v.reshape(2, flat)], axis=1), ((0, 0), (0, 6), (0, 0)))
    w2 = jnp.stack([w2_k, w2_v]).astype(BF16)
    qh = q_s.reshape(DEC_SEQ, DEC_BATCH, N_KV, GQA, HEAD_DIM).transpose(1, 2, 3, 0, 4)
    q16 = qh.reshape(DEC_BATCH, N_KV, ROWS16, HEAD_DIM)
    gh = gates_s[:, :3 * N_HEADS].reshape(DEC_SEQ, DEC_BATCH, 3, N_KV, GQA).transpose(1, 3, 4, 0, 2)
    g16 = jnp.pad(gh.reshape(DEC_BATCH, N_KV, ROWS16, 3), ((0, 0), (0, 0), (0, 0), (0, LANES - 3)))

    def new_rows(a):
        a = a.reshape(DEC_SEQ, DEC_BATCH, HEADS8, HEAD_DIM).transpose(1, 2, 0, 3)
        return jnp.pad(a, ((0, 0), (0, 0), (0, S_NEW - DEC_SEQ), (0, 0)))

    smap_t = jnp.repeat(_overlap_map(S_NC, S_NS, S_CHUNKS, LANES), HEADS8, axis=1)
    expand = jnp.asarray((np.arange(LANES)[:, None] == (np.arange(S_KEYS)[None, :] // SEL_BLOCK)), BF16)
    pt = page_table.reshape(-1).astype(jnp.int32)
    page_shape = (None, None, PAGE_SIZE, 2, N_KV, HEAD_DIM)
    page_specs = [pl.BlockSpec(page_shape, lambda b, pt, p=p: (0, pt[b * N_PAGES + p], 0, 0, 0, 0))
                  for p in range(N_PAGES)]
    per_b4 = lambda shape: pl.BlockSpec((None,) + shape, lambda b, pt: (b, 0, 0, 0))
    const = lambda shape: pl.BlockSpec(shape, lambda b, pt: (0,) * len(shape))
    o16 = pl.pallas_call(
        _nsa_sample_native_kernel,
        grid_spec=pltpu.PrefetchScalarGridSpec(
            num_scalar_prefetch=1,
            grid=(DEC_BATCH,),
            in_specs=page_specs + page_specs + [
                pl.BlockSpec((None, None, WIN_BUF, 2, N_KV, HEAD_DIM), lambda b, pt: (0, b, 0, 0, 0, 0)),
                per_b4((HEADS8, S_NEW, HEAD_DIM)),
                per_b4((HEADS8, S_NEW, HEAD_DIM)),
                per_b4((N_KV, ROWS16, HEAD_DIM)),
                per_b4((N_KV, ROWS16, LANES)),
                const((2, 8, COMP_STRIDE * HEAD_DIM)),
                const((2, COMP_STRIDE * HEAD_DIM, 2 * CMP_HIDDEN)),
                const((2, CMP_HIDDEN, HEAD_DIM)),
                const((LANES, S_CHUNKS * HEADS8)),
                const((LANES, S_KEYS)),
            ],
            out_specs=per_b4((N_KV, ROWS16, HEAD_DIM)),
            scratch_shapes=[pltpu.VMEM((HEADS8, S_KEYS, HEAD_DIM), BF16), pltpu.VMEM((HEADS8, S_WKEYS, HEAD_DIM), BF16)],
        ),
        out_shape=jax.ShapeDtypeStruct((DEC_BATCH, N_KV, ROWS16, HEAD_DIM), F32),
        compiler_params=pltpu.CompilerParams(dimension_semantics=("parallel",), vmem_limit_bytes=S_VMEM_LIMIT),
        name="nsa_sample",
    )(pt, *([cache_c] * N_PAGES), *([cache_s] * N_PAGES), win_prev, new_rows(kvs_new), new_rows(kvw_new),
      q16, g16, pe, w1, w2, smap_t, expand)
    o = o16.reshape(DEC_BATCH, N_KV, GQA, DEC_SEQ, HEAD_DIM)
    return o.transpose(3, 0, 1, 2, 4).reshape(N_SAMPLE, Q_WIDTH).astype(BF16)


def _nsa_sample_t_kernel(pt_ref, *refs):
    cp = refs[:N_PAGES]
    sp = refs[N_PAGES:2 * N_PAGES]
    (win_ref, news_ref, neww_ref, q_ref, g_ref, pe_ref, w1_ref, w2_ref, smap_ref, expand_ref,
     o_ref, ks_ref, kw_ref) = refs[2 * N_PAGES:]
    del pt_ref

    for c in range(HEADS8):
        kv, h = c // N_KV, c % N_KV
        for p in range(N_PAGES):
            ks_ref[c, :, PAGE_SIZE * p:PAGE_SIZE * (p + 1)] = sp[p][kv, h].astype(BF16)
        ks_ref[c, :, PAST_LEN:] = news_ref[c]
        kw_ref[c, :, 0:WIN_BUF] = win_ref[kv, h].astype(BF16)
        kw_ref[c, :, WIN_BUF:] = neww_ref[c]

    kcv = []
    for c in range(HEADS8):
        kv = c // N_KV
        x = jnp.concatenate([cp[p][c] for p in range(N_PAGES)], axis=0)
        a0 = jnp.dot((x + pe_ref[kv, 0]).astype(BF16), w1_ref[kv, 0], preferred_element_type=F32)
        a1 = jnp.dot((x + pe_ref[kv, 1]).astype(BF16), w1_ref[kv, 1], preferred_element_type=F32)
        hid = a0 + pltpu.roll(a1, S_CHUNKS - 1, axis=0)
        kcv.append(jnp.dot(_gelu_tanh(hid).astype(BF16), w2_ref[kv], preferred_element_type=F32).astype(BF16))

    row = lax.broadcasted_iota(jnp.int32, (ROWS16, 1), 0)
    q_pos = PAST_LEN + (row & (DEC_SEQ - 1))
    grp = row >> 2

    def slopes(k):
        s = jnp.full((ROWS16, 1), _alibi_slope(4 * k), F32)
        for g in range(1, GQA):
            s = jnp.where(grp == g, _alibi_slope(4 * k + g), s)
        return s

    c_end = lax.broadcasted_iota(jnp.int32, (1, S_CHUNKS), 1) * COMP_STRIDE + (COMP_BLOCK - 1)
    d_c = q_pos - c_end
    o_cmp, psums = [], []
    for k in range(N_KV):
        s = _dot_nt(q_ref[k], kcv[k]) - slopes(k) * d_c.astype(F32)
        p = _row_softmax(s, d_c >= 0)
        o_cmp.append(jnp.dot(p.astype(BF16), kcv[N_KV + k], preferred_element_type=F32))
        ps = p
        for g in range(1, GQA):
            ps = ps + pltpu.roll(p, DEC_SEQ * g, axis=0)
        psums.append(ps)
    psum_all = jnp.concatenate(psums + [jnp.zeros((LANES - N_KV * ROWS16, S_CHUNKS), F32)], axis=0)
    imp_t = _dot_nt(smap_ref[...], psum_all, precision=lax.Precision.HIGHEST)
    col = lax.broadcasted_iota(jnp.int32, (1, LANES), 1)
    sel_t = _select_blocks(imp_t, PAST_LEN + (col & (DEC_SEQ - 1)), N_SEL)
    sel = sel_t.T.astype(BF16)

    key_s = lax.broadcasted_iota(jnp.int32, (1, S_KEYS), 1)
    d_s = q_pos - key_s
    idx_w = lax.broadcasted_iota(jnp.int32, (1, S_WKEYS), 1)
    d_w = q_pos - (PAST_LEN - WIN_BUF + idx_w)
    ok_w = (d_w >= 0) & (d_w <= WINDOW)
    for k in range(N_KV):
        chosen = jnp.dot(sel[ROWS16 * k:ROWS16 * (k + 1)], expand_ref[...], preferred_element_type=F32)
        s = jnp.dot(q_ref[k], ks_ref[k], preferred_element_type=F32) - slopes(k) * d_s.astype(F32)
        p = _row_softmax(s, (chosen > 0.5) & (d_s >= 0))
        o_sel = _dot_nt(p.astype(BF16), ks_ref[N_KV + k])
        s = jnp.dot(q_ref[k], kw_ref[k], preferred_element_type=F32) - slopes(k) * d_w.astype(F32)
        p = _row_softmax(s, ok_w)
        o_win = _dot_nt(p.astype(BF16), kw_ref[N_KV + k])
        o_ref[k] = (g_ref[k][:, 0:1] * o_cmp[k] + g_ref[k][:, 1:2] * o_sel + g_ref[k][:, 2:3] * o_win)


def _nsa_sample_t(page_table, cache_c, cache_s, win_prev, q_s, gates_s, kvs_new, kvw_new, cw):
    pos_k, w1_k, w2_k, pos_v, w1_v, w2_v = cw
    flat = COMP_STRIDE * HEAD_DIM
    n_phys = cache_c.shape[1]
    w1 = jnp.stack([w1_k, w1_v]).reshape(2, 2, flat, CMP_HIDDEN).astype(BF16)
    pe = jnp.stack([pos_k, pos_v]).reshape(2, 2, 1, flat)
    w2 = jnp.stack([w2_k, w2_v]).astype(BF16)
    chunks = cache_c.reshape(n_phys, CHUNKS_PER_PAGE, COMP_STRIDE, HEADS8, HEAD_DIM)
    chunks = chunks.transpose(0, 3, 1, 2, 4).reshape(n_phys, HEADS8, CHUNKS_PER_PAGE, flat)
    slc_t = cache_s.transpose(0, 1, 3, 4, 5, 2)
    win_t = win_prev.transpose(0, 1, 3, 4, 5, 2)
    qh = q_s.reshape(DEC_SEQ, DEC_BATCH, N_KV, GQA, HEAD_DIM).transpose(1, 2, 3, 0, 4)
    q16 = qh.reshape(DEC_BATCH, N_KV, ROWS16, HEAD_DIM)
    gh = gates_s[:, :3 * N_HEADS].reshape(DEC_SEQ, DEC_BATCH, 3, N_KV, GQA).transpose(1, 3, 4, 0, 2)
    g16 = jnp.pad(gh.reshape(DEC_BATCH, N_KV, ROWS16, 3), ((0, 0), (0, 0), (0, 0), (0, LANES - 3)))

    def new_rows_t(a, width):
        a = a.reshape(DEC_SEQ, DEC_BATCH, HEADS8, HEAD_DIM).transpose(1, 2, 3, 0).astype(BF16)
        return jnp.pad(a, ((0, 0), (0, 0), (0, 0), (0, width - DEC_SEQ)))

    smap_t = _overlap_map(S_NC, S_NS, S_CHUNKS, LANES)
    expand = jnp.asarray((np.arange(LANES)[:, None] == (np.arange(S_KEYS)[None, :] // SEL_BLOCK)), BF16)
    pt = page_table.reshape(-1).astype(jnp.int32)
    c_specs = [pl.BlockSpec((None, HEADS8, CHUNKS_PER_PAGE, flat), lambda b, pt, p=p: (pt[b * N_PAGES + p], 0, 0, 0))
               for p in range(N_PAGES)]
    s_specs = [pl.BlockSpec((None, None, 2, N_KV, HEAD_DIM, PAGE_SIZE),
                            lambda b, pt, p=p: (0, pt[b * N_PAGES + p], 0, 0, 0, 0)) for p in range(N_PAGES)]
    per_b4 = lambda shape: pl.BlockSpec((None,) + shape, lambda b, pt: (b, 0, 0, 0))
    const = lambda shape: pl.BlockSpec(shape, lambda b, pt: (0,) * len(shape))
    o16 = pl.pallas_call(
        _nsa_sample_t_kernel,
        grid_spec=pltpu.PrefetchScalarGridSpec(
            num_scalar_prefetch=1,
            grid=(DEC_BATCH,),
            in_specs=c_specs + s_specs + [
                pl.BlockSpec((None, None, 2, N_KV, HEAD_DIM, WIN_BUF), lambda b, pt: (0, b, 0, 0, 0, 0)),
                per_b4((HEADS8, HEAD_DIM, S_KEYS - PAST_LEN)),
                per_b4((HEADS8, HEAD_DIM, S_WKEYS - WIN_BUF)),
                per_b4((N_KV, ROWS16, HEAD_DIM)),
                per_b4((N_KV, ROWS16, LANES)),
                const((2, 2, 1, flat)),
                const((2, 2, flat, CMP_HIDDEN)),
                const((2, CMP_HIDDEN, HEAD_DIM)),
                const((LANES, S_CHUNKS)),
                const((LANES, S_KEYS)),
            ],
            out_specs=per_b4((N_KV, ROWS16, HEAD_DIM)),
            scratch_shapes=[pltpu.VMEM((HEADS8, HEAD_DIM, S_KEYS), BF16), pltpu.VMEM((HEADS8, HEAD_DIM, S_WKEYS), BF16)],
        ),
        out_shape=jax.ShapeDtypeStruct((DEC_BATCH, N_KV, ROWS16, HEAD_DIM), F32),
        compiler_params=_cparams("parallel"),
        name="nsa_sample",
    )(pt, *([chunks] * N_PAGES), *([slc_t] * N_PAGES), win_t,
      new_rows_t(kvs_new, S_KEYS - PAST_LEN), new_rows_t(kvw_new, S_WKEYS - WIN_BUF),
      q16, g16, pe, w1, w2, smap_t, expand)
    o = o16.reshape(DEC_BATCH, N_KV, GQA, DEC_SEQ, HEAD_DIM)
    return o.transpose(3, 0, 1, 2, 4).reshape(N_SAMPLE, Q_WIDTH).astype(BF16)


OUT_TM = 512
ROUTE_E0, ROUTE_E1, ROUTE_W0, ROUTE_W1, ROUTE_R0, ROUTE_R1 = range(6)


def _layer_norm(x, g, b):
    mu = jnp.mean(x, axis=-1, keepdims=True)
    xc = x - mu
    var = jnp.mean(xc * xc, axis=-1, keepdims=True)
    return xc * lax.rsqrt(var + LN_EPS) * g + b


def _first_max(vals, lane_f):
    mx = jnp.max(vals, axis=-1, keepdims=True)
    idx = jnp.min(jnp.where(vals == mx, lane_f, float(LANES)), axis=-1, keepdims=True)
    return mx, idx


def _out_route_kernel(conv_ref, attn_ref, x_ref, wo_ref, g1_ref, b1_ref, wr_ref, br_ref,
                      h_ref, route_ref, counts_ref, run_ref):
    i = pl.program_id(0)

    @pl.when(i == 0)
    def _():
        run_ref[...] = jnp.zeros_like(run_ref)

    mix = jnp.dot(conv_ref[...], wo_ref[0:CONV_CH, :], preferred_element_type=F32)
    mix = mix + jnp.dot(attn_ref[...], wo_ref[CONV_CH:, :], preferred_element_type=F32)
    h = _layer_norm(ALPHA * x_ref[...] + mix, g1_ref[...], b1_ref[...])
    h_ref[...] = h

    logit = jnp.dot(h, wr_ref[...], preferred_element_type=F32, precision=lax.Precision.HIGHEST) + br_ref[...]
    lane = lax.broadcasted_iota(jnp.int32, logit.shape, 1)
    lane_f = lane.astype(F32)
    is_grp = (lane >= N_EXPERTS) & (lane < N_EXPERTS + N_GROUPS)
    g_logit = jnp.where(is_grp, logit, NEG_BIG)
    g_top, g_lane = _first_max(g_logit, lane_f)
    lse = g_top + jnp.log(jnp.sum(jnp.where(is_grp, jnp.exp(g_logit - g_top), 0.0), axis=-1, keepdims=True))
    g_p = jnp.exp(g_top - lse)
    g_idx = g_lane.astype(jnp.int32) - N_EXPERTS
    in_grp = (lane >> 3) == g_idx
    e_logit = jnp.where(in_grp, logit, NEG_BIG)
    e0, i0 = _first_max(e_logit, lane_f)
    e1, i1 = _first_max(jnp.where(lane_f == i0, NEG_BIG, e_logit), lane_f)
    z1 = jnp.exp(e1 - e0)
    w0 = g_p / (1.0 + z1)
    w1 = g_p * z1 / (1.0 + z1)

    hit = jnp.where((lane_f == i0) | (lane_f == i1), 1.0, 0.0)
    r = lax.broadcasted_iota(jnp.int32, (OUT_TM, OUT_TM), 0)
    c = lax.broadcasted_iota(jnp.int32, (OUT_TM, OUT_TM), 1)
    lower = jnp.where(c < r, 1.0, 0.0).astype(BF16)
    before = jnp.dot(lower, hit.astype(BF16), preferred_element_type=F32) + run_ref[...]
    r0 = jnp.sum(jnp.where(lane_f == i0, before, 0.0), axis=-1, keepdims=True)
    r1 = jnp.sum(jnp.where(lane_f == i1, before, 0.0), axis=-1, keepdims=True)
    run_ref[...] = run_ref[...] + jnp.sum(hit, axis=0, keepdims=True)
    counts_ref[...] = jnp.broadcast_to(run_ref[...], counts_ref.shape)

    rec = jnp.zeros(logit.shape, F32)
    for col_id, v in ((ROUTE_E0, i0), (ROUTE_E1, i1), (ROUTE_W0, w0), (ROUTE_W1, w1), (ROUTE_R0, r0), (ROUTE_R1, r1)):
        rec = jnp.where(lane == col_id, v, rec)
    route_ref[...] = rec


def _out_route(conv, attn, x, w_out, ln_g, ln_b, w_grp, b_grp, w_exp, b_exp):
    n = x.shape[0]
    wr = jnp.pad(jnp.concatenate([w_exp, w_grp], axis=1), ((0, 0), (0, LANES - N_EXPERTS - N_GROUPS)))
    br = jnp.pad(jnp.concatenate([b_exp, b_grp]), (0, LANES - N_EXPERTS - N_GROUPS)).reshape(1, LANES)
    const = lambda shape: pl.BlockSpec(shape, lambda i: (0, 0))
    return pl.pallas_call(
        _out_route_kernel,
        grid=(n // OUT_TM,),
        in_specs=[
            pl.BlockSpec((OUT_TM, CONV_CH), lambda i: (i, 0)),
            pl.BlockSpec((OUT_TM, Q_WIDTH), lambda i: (i, 0)),
            pl.BlockSpec((OUT_TM, D_MODEL), lambda i: (i, 0)),
            pl.BlockSpec((D_MODEL, D_MODEL), lambda i: (0, 0), pipeline_mode=pl.Buffered(1)),
            const((1, D_MODEL)),
            const((1, D_MODEL)),
            const((D_MODEL, LANES)),
            const((1, LANES)),
        ],
        out_specs=[
            pl.BlockSpec((OUT_TM, D_MODEL), lambda i: (i, 0)),
            pl.BlockSpec((OUT_TM, LANES), lambda i: (i, 0)),
            const((8, LANES)),
        ],
        out_shape=[
            jax.ShapeDtypeStruct((n, D_MODEL), F32),
            jax.ShapeDtypeStruct((n, LANES), F32),
            jax.ShapeDtypeStruct((8, LANES), F32),
        ],
        scratch_shapes=[pltpu.VMEM((1, LANES), F32)],
        compiler_params=_cparams("arbitrary"),
        name="out_ln1_route",
    )(conv, attn, x, w_out.astype(BF16), ln_g.reshape(1, D_MODEL), ln_b.reshape(1, D_MODEL), wr, br)


MOE_BLK = 128
N_ASSIGN = N_TOK * TOP_K
MOE_BLOCKS = -(-(N_ASSIGN + N_EXPERTS * (MOE_BLK - 1)) // MOE_BLK)
MOE_ROWS = MOE_BLOCKS * MOE_BLK
CMB_TM = 128


def _gather_rows(src_hbm, dst_ref, sem, idx_ref, base, n_rows):
    def body(r, carry):
        pltpu.make_async_copy(src_hbm.at[pl.ds(idx_ref[base + r], 1)], dst_ref.at[pl.ds(r, 1)], sem).start()
        return carry
    lax.fori_loop(0, n_rows, body, 0)


def _wait_rows(src_hbm, dst_ref, sem):
    pltpu.make_async_copy(src_hbm.at[pl.ds(0, dst_ref.shape[0])], dst_ref, sem).wait()


def _wait_row_count(src_hbm, dst_ref, sem, n_rows):
    p = MOE_BLK
    while p >= 1:
        @pl.when((n_rows & p) != 0)
        def _(p=p):
            pltpu.make_async_copy(src_hbm.at[pl.ds(0, p)], dst_ref.at[pl.ds(0, p)], sem).wait()
        p //= 2


def _moe_kernel(blk_e_ref, row_tok_ref, blk_rows_ref, h_hbm, wg_ref, wu_ref, wd_ref, y_ref,
                xbuf, sems, wg_b, wu_b, wd_b):
    i = pl.program_id(0)
    slot = i % 2
    rows = blk_rows_ref[i]
    nxt = jnp.minimum(i + 1, MOE_BLOCKS - 1)
    rows_next = jnp.where(i + 1 < MOE_BLOCKS, blk_rows_ref[nxt], 0)

    @pl.when(i == 0)
    def _():
        xbuf[...] = jnp.zeros_like(xbuf)
        _gather_rows(h_hbm, xbuf.at[0], sems.at[0], row_tok_ref, 0, rows)

    @pl.when(rows_next > 0)
    def _():
        _gather_rows(h_hbm, xbuf.at[1 - slot], sems.at[1 - slot], row_tok_ref, (i + 1) * MOE_BLK, rows_next)

    @pl.when((i == 0) | (blk_e_ref[i] != blk_e_ref[jnp.maximum(i - 1, 0)]))
    def _():
        wg_b[...] = wg_ref[...].astype(BF16)
        wu_b[...] = wu_ref[...].astype(BF16)
        wd_b[...] = wd_ref[...].astype(BF16)

    _wait_row_count(h_hbm, xbuf.at[slot], sems.at[slot], rows)

    @pl.when(rows > 0)
    def _():
        x = xbuf[slot].astype(BF16)
        hg = jnp.dot(x, wg_b[...], preferred_element_type=F32)
        hu = jnp.dot(x, wu_b[...], preferred_element_type=F32)
        act = (hg * jax.nn.sigmoid(hg) * hu).astype(BF16)
        y_ref[...] = jnp.dot(act, wd_b[...], preferred_element_type=F32)

    @pl.when(rows == 0)
    def _():
        y_ref[...] = jnp.zeros_like(y_ref)


def _moe(h, blk_e, row_tok, blk_rows, w_gate, w_up, w_down):
    wspec = lambda shape: pl.BlockSpec((None,) + shape, lambda i, be, rt, nv: (be[i], 0, 0))
    return pl.pallas_call(
        _moe_kernel,
        grid_spec=pltpu.PrefetchScalarGridSpec(
            num_scalar_prefetch=3,
            grid=(MOE_BLOCKS,),
            in_specs=[
                pl.BlockSpec(memory_space=pl.ANY),
                wspec((D_MODEL, MOE_D_FF)),
                wspec((D_MODEL, MOE_D_FF)),
                wspec((MOE_D_FF, D_MODEL)),
            ],
            out_specs=pl.BlockSpec((MOE_BLK, D_MODEL), lambda i, be, rt, nv: (i, 0)),
            scratch_shapes=[
                pltpu.VMEM((2, MOE_BLK, D_MODEL), F32),
                pltpu.SemaphoreType.DMA((2,)),
                pltpu.VMEM((D_MODEL, MOE_D_FF), BF16),
                pltpu.VMEM((D_MODEL, MOE_D_FF), BF16),
                pltpu.VMEM((MOE_D_FF, D_MODEL), BF16),
            ],
        ),
        out_shape=jax.ShapeDtypeStruct((MOE_ROWS, D_MODEL), F32),
        compiler_params=_cparams("arbitrary"),
        name="moe_ffn",
    )(blk_e, row_tok, blk_rows, h, w_gate, w_up, w_down)


def _combine_kernel(dest_ref, y_hbm, h_ref, route_ref, g2_ref, b2_ref, o_ref, ybuf, sems):
    i = pl.program_id(0)
    n_steps = pl.num_programs(0)
    slot = i % 2

    @pl.when(i == 0)
    def _():
        _gather_rows(y_hbm, ybuf.at[0], sems.at[0], dest_ref, 0, TOP_K * CMB_TM)

    @pl.when(i + 1 < n_steps)
    def _():
        _gather_rows(y_hbm, ybuf.at[1 - slot], sems.at[1 - slot], dest_ref, (i + 1) * TOP_K * CMB_TM, TOP_K * CMB_TM)

    _wait_rows(y_hbm, ybuf.at[slot], sems.at[slot])
    w0 = route_ref[:, ROUTE_W0:ROUTE_W0 + 1]
    w1 = route_ref[:, ROUTE_W1:ROUTE_W1 + 1]
    moe = ybuf[slot, 0:CMB_TM] * w0 + ybuf[slot, CMB_TM:2 * CMB_TM] * w1
    o_ref[...] = _layer_norm(ALPHA * h_ref[...] + moe, g2_ref[...], b2_ref[...])


def _combine(dest, y_rows, h, route, ln_g, ln_b):
    n = h.shape[0]
    return pl.pallas_call(
        _combine_kernel,
        grid_spec=pltpu.PrefetchScalarGridSpec(
            num_scalar_prefetch=1,
            grid=(n // CMB_TM,),
            in_specs=[
                pl.BlockSpec(memory_space=pl.ANY),
                pl.BlockSpec((CMB_TM, D_MODEL), lambda i, d: (i, 0)),
                pl.BlockSpec((CMB_TM, LANES), lambda i, d: (i, 0)),
                pl.BlockSpec((1, D_MODEL), lambda i, d: (0, 0)),
                pl.BlockSpec((1, D_MODEL), lambda i, d: (0, 0)),
            ],
            out_specs=pl.BlockSpec((CMB_TM, D_MODEL), lambda i, d: (i, 0)),
            scratch_shapes=[pltpu.VMEM((2, TOP_K * CMB_TM, D_MODEL), F32), pltpu.SemaphoreType.DMA((2,))],
        ),
        out_shape=jax.ShapeDtypeStruct((n, D_MODEL), F32),
        compiler_params=_cparams("arbitrary"),
        name="moe_combine_ln2",
    )(dest, y_rows, h, route, ln_g.reshape(1, D_MODEL), ln_b.reshape(1, D_MODEL))


def _dispatch_indices(route, counts):
    eid = route[:, ROUTE_E0:ROUTE_E1 + 1].astype(jnp.int32)
    rank = route[:, ROUTE_R0:ROUTE_R1 + 1].astype(jnp.int32)
    cnt = counts[0, :N_EXPERTS].astype(jnp.int32)
    padded = (cnt + MOE_BLK - 1) // MOE_BLK * MOE_BLK
    pad_end = jnp.cumsum(padded)
    pad_start = pad_end - padded
    dest = pad_start[eid] + rank
    tok = jnp.broadcast_to(jnp.arange(N_TOK, dtype=jnp.int32)[:, None], dest.shape)
    row_tok = jnp.zeros((MOE_ROWS,), jnp.int32).at[dest.reshape(-1)].set(tok.reshape(-1))
    blk_start = jnp.arange(MOE_BLOCKS, dtype=jnp.int32) * MOE_BLK
    blk_e = jnp.minimum(jnp.sum(pad_end[None, :] <= blk_start[:, None], axis=1), N_EXPERTS - 1).astype(jnp.int32)
    blk_rows = jnp.clip(cnt[blk_e] - (blk_start - pad_start[blk_e]), 0, MOE_BLK).astype(jnp.int32)
    dest_tiles = dest.reshape(N_TOK // CMB_TM, CMB_TM, TOP_K).transpose(0, 2, 1).reshape(-1)
    return blk_e, row_tok, blk_rows, dest_tiles


def kernel(x_prompt, x_sample, cache_cmp_kv, cache_slc_kv, state_win_kv, state_conv, page_table, w_in, conv_w, conv_b, conv_ln_g, conv_ln_b, cmp_pos_k, cmp_w1_k, cmp_w2_k, cmp_pos_v, cmp_w1_v, cmp_w2_v, w_out, ln1_g, ln1_b, w_grp, b_grp, w_exp, b_exp, w_gate, w_up, w_down, ln2_g, ln2_b):
    xs_t = x_sample.transpose(1, 0, 2).reshape(N_SAMPLE, D_MODEL)
    x = jnp.concatenate([x_prompt.reshape(N_PROMPT, D_MODEL), xs_t], axis=0)
    glu, q, kvc, kvs, kvw, gates, ks_k, vs_t, kw_k, vw_t = _proj(x, w_in[0])
    conv_p = _conv_prompt(glu, conv_w[0], conv_b[0], conv_ln_g[0], conv_ln_b[0])
    a_s = glu[N_PROMPT:].reshape(DEC_SEQ, DEC_BATCH, CONV_CH)
    xt = jnp.concatenate([state_conv[0].transpose(1, 0, 2), a_s], axis=0)
    conv_s = _conv_sample(xt, conv_w[0], conv_b[0], conv_ln_g[0], conv_ln_b[0])
    cw = (cmp_pos_k[0], cmp_w1_k[0], cmp_w2_k[0], cmp_pos_v[0], cmp_w1_v[0], cmp_w2_v[0])
    kcv_p = _compress_prompt(kvc, cw)
    o_c, drop = _cmp_prompt(q, kcv_p)
    attn_p = _selw_prompt(q, gates, o_c, drop, ks_k, vs_t, kw_k, vw_t)
    attn_s = _nsa_sample_t(page_table, cache_cmp_kv, cache_slc_kv, state_win_kv, q[N_PROMPT:],
                           gates[N_PROMPT:], kvs[N_PROMPT:], kvw[N_PROMPT:], cw)
    conv = jnp.concatenate([conv_p, conv_s.reshape(N_SAMPLE, CONV_CH)], axis=0)
    attn = jnp.concatenate([attn_p, attn_s], axis=0)
    h1, route, counts = _out_route(conv, attn, x, w_out[0], ln1_g[0], ln1_b[0], w_grp[0], b_grp[0],
                                   w_exp[0], b_exp[0])
    blk_e, row_tok, blk_rows, dest_tiles = _dispatch_indices(route, counts)
    y_rows = _moe(h1, blk_e, row_tok, blk_rows, w_gate[0], w_up[0], w_down[0])
    y = _combine(dest_tiles, y_rows, h1, route, ln2_g[0], ln2_b[0])

    def batch_major(a):
        return a[N_PROMPT:].reshape(DEC_SEQ, DEC_BATCH, -1).transpose(1, 0, 2)

    kv_shape = (2, N_KV, HEAD_DIM)
    win_len = min(WINDOW, SEQ)
    y_prompt = y[:N_PROMPT].reshape(1, SEQ, D_MODEL)
    y_sample = batch_major(y)
    p_cmp = kvc[:N_PROMPT].reshape(1, 1, SEQ, *kv_shape)
    p_slc = kvs[:N_PROMPT].reshape(1, 1, SEQ, *kv_shape)
    p_win = kvw[N_PROMPT - win_len:N_PROMPT].reshape(1, 1, win_len, *kv_shape)
    p_conv = glu[N_PROMPT - (CONV_WIDTH - 1):N_PROMPT].reshape(1, 1, CONV_WIDTH - 1, CONV_CH)
    s_cmp = batch_major(kvc).reshape(1, DEC_BATCH, DEC_SEQ, *kv_shape)
    s_slc = batch_major(kvs).reshape(1, DEC_BATCH, DEC_SEQ, *kv_shape)
    s_win = jnp.concatenate([state_win_kv[0][:, DEC_SEQ:],
                             batch_major(kvw).reshape(DEC_BATCH, DEC_SEQ, *kv_shape)], axis=1)[None]
    s_conv = jnp.concatenate([state_conv[0][:, DEC_SEQ:], batch_major(glu)], axis=1)[None]
    return (y_prompt, y_sample, p_cmp, p_slc, p_win, p_conv, s_cmp, s_slc, s_win, s_conv)
```

```python
import numpy as np
import jax
import jax.numpy as jnp
from jax import lax
from jax.experimental import pallas as pl
from jax.experimental.pallas import tpu as pltpu

F32 = jnp.float32
BF16 = jnp.bfloat16

D_MODEL = 2048
SEQ = 8192
DEC_BATCH = 128
DEC_SEQ = 4
PAST_LEN = 2048
PAGE_SIZE = 128
CONV_CH = 1024
CONV_WIDTH = 31
N_HEADS = 16
HEAD_DIM = 64
N_KV = 4
GQA = 4
Q_WIDTH = 1024
KV_WIDTH = 256
COMP_BLOCK = 32
COMP_STRIDE = 16
CMP_HIDDEN = 128
SEL_BLOCK = 64
N_SEL = 16
WINDOW = 512
N_GROUPS = 8
EXPERTS_PER_GROUP = 8
N_EXPERTS = 64
TOP_K = 2
MOE_D_FF = 512
LN_EPS = 1e-5
ALPHA = 2.0 ** 0.25
LOG2E = 1.4426950408889634

N_PROMPT = SEQ
N_SAMPLE = DEC_BATCH * DEC_SEQ
N_TOK = N_PROMPT + N_SAMPLE

LANES = 128
VMEM_LIMIT = 48 * 1024 * 1024


def _cparams(*sem):
    return pltpu.CompilerParams(dimension_semantics=sem, vmem_limit_bytes=VMEM_LIMIT)


PROJ_TM = 512
PROJ_TN = 512
PROJ_VMEM_LIMIT = 56 * 1024 * 1024


def _proj_kernel(x_ref, w_ref, wg_ref, glu_ref, q_ref, kvc_ref, kvs_ref, kvw_ref, g_ref,
                 ksk_ref, vst_ref, kwk_ref, vwt_ref, xb_ref):
    xb_ref[...] = x_ref[...].astype(BF16)

    def cols(j):
        return jnp.dot(xb_ref[...], w_ref[:, PROJ_TN * j:PROJ_TN * (j + 1)], preferred_element_type=F32)

    for j in range(4):
        acc = cols(j)
        glu_ref[:, 256 * j:256 * (j + 1)] = acc[:, :256] * jax.nn.sigmoid(acc[:, 256:])
    for j in range(2):
        q_ref[:, PROJ_TN * j:PROJ_TN * (j + 1)] = (cols(4 + j) * (HEAD_DIM ** -0.5 * LOG2E)).astype(BF16)
    kvc_ref[...] = cols(6)
    acc = cols(7)
    kvs_ref[...] = acc
    ksk_ref[...] = acc[:, :KV_WIDTH].astype(BF16)
    vst_ref[...] = acc[:, KV_WIDTH:].T.astype(BF16)
    acc = cols(8)
    kvw_ref[...] = acc
    kwk_ref[...] = acc[:, :KV_WIDTH].astype(BF16)
    vwt_ref[...] = acc[:, KV_WIDTH:].T.astype(BF16)
    g_ref[...] = jax.nn.sigmoid(jnp.dot(xb_ref[...], wg_ref[...], preferred_element_type=F32))


def _proj(x, w_in):
    n = x.shape[0]
    o1 = 2 * CONV_CH
    o2 = o1 + Q_WIDTH
    o3 = o2 + 6 * KV_WIDTH
    wa, wb = w_in[:, :CONV_CH], w_in[:, CONV_CH:o1]
    glu_cols = [jnp.concatenate([wa[:, 256 * s:256 * (s + 1)], wb[:, 256 * s:256 * (s + 1)]], axis=1)
                for s in range(4)]
    w_cat = jnp.concatenate(glu_cols + [w_in[:, o1:o3]], axis=1).astype(BF16)
    wg = w_in[:, o3:].reshape(D_MODEL, N_HEADS, 3).transpose(0, 2, 1).reshape(D_MODEL, 3 * N_HEADS)
    wg = jnp.pad(wg, ((0, 0), (0, LANES - 3 * N_HEADS))).astype(BF16)
    rows = lambda width: pl.BlockSpec((PROJ_TM, width), lambda i: (i, 0))
    cols_t = pl.BlockSpec((KV_WIDTH, PROJ_TM), lambda i: (0, i))
    return pl.pallas_call(
        _proj_kernel,
        grid=(n // PROJ_TM,),
        in_specs=[
            rows(D_MODEL),
            pl.BlockSpec((D_MODEL, 9 * PROJ_TN), lambda i: (0, 0), pipeline_mode=pl.Buffered(1)),
            pl.BlockSpec((D_MODEL, LANES), lambda i: (0, 0)),
        ],
        out_specs=[rows(CONV_CH), rows(Q_WIDTH), rows(2 * KV_WIDTH), rows(2 * KV_WIDTH), rows(2 * KV_WIDTH),
                   rows(LANES), rows(KV_WIDTH), cols_t, rows(KV_WIDTH), cols_t],
        out_shape=[
            jax.ShapeDtypeStruct((n, CONV_CH), F32),
            jax.ShapeDtypeStruct((n, Q_WIDTH), BF16),
            jax.ShapeDtypeStruct((n, 2 * KV_WIDTH), F32),
            jax.ShapeDtypeStruct((n, 2 * KV_WIDTH), F32),
            jax.ShapeDtypeStruct((n, 2 * KV_WIDTH), F32),
            jax.ShapeDtypeStruct((n, LANES), F32),
            jax.ShapeDtypeStruct((n, KV_WIDTH), BF16),
            jax.ShapeDtypeStruct((KV_WIDTH, n), BF16),
            jax.ShapeDtypeStruct((n, KV_WIDTH), BF16),
            jax.ShapeDtypeStruct((KV_WIDTH, n), BF16),
        ],
        scratch_shapes=[pltpu.VMEM((PROJ_TM, D_MODEL), BF16)],
        compiler_params=pltpu.CompilerParams(dimension_semantics=("parallel",), vmem_limit_bytes=PROJ_VMEM_LIMIT),
        name="proj",
    )(x, w_cat, wg)


CONV_T = 256
CONV_HALO = 32
CONV_CC = 256


def _ln_silu(y, g, beta):
    mu = jnp.mean(y, axis=-1, keepdims=True)
    yc = y - mu
    var = jnp.mean(yc * yc, axis=-1, keepdims=True)
    z = yc * lax.rsqrt(var + LN_EPS) * g + beta
    return z * jax.nn.sigmoid(z)


def _conv_prompt_kernel(cur_ref, prev_ref, w_ref, b_ref, g_ref, beta_ref, o_ref, xp_ref, y_ref, z_ref):
    i = pl.program_id(0)
    xp_ref[CONV_HALO:, :] = cur_ref[...]

    @pl.when(i == 0)
    def _():
        xp_ref[0:CONV_HALO, :] = jnp.zeros((CONV_HALO, CONV_CH), F32)

    @pl.when(i > 0)
    def _():
        xp_ref[0:CONV_HALO, :] = prev_ref[...]

    off = CONV_HALO - (CONV_WIDTH - 1)
    for c in range(CONV_CH // CONV_CC):
        cols = slice(c * CONV_CC, (c + 1) * CONV_CC)
        for b in range(8):
            n = CONV_T if b == 0 else CONV_T + 8
            z = None
            for k in range(CONV_WIDTH):
                if (off + k) % 8 != b:
                    continue
                a8 = off + k - b
                term = w_ref[k:k + 1, cols] * xp_ref[a8:a8 + n, cols]
                z = term if z is None else z + term
            z_ref[b, 0:n] = z
        acc = z_ref[0, 0:CONV_T]
        for b in range(1, 8):
            acc = acc + z_ref[b, b:b + CONV_T]
        y_ref[:, cols] = acc
    y = y_ref[...] + b_ref[...]
    o_ref[...] = _ln_silu(y, g_ref[...], beta_ref[...]).astype(BF16)


def _conv_prompt(a, conv_w, conv_b, ln_g, ln_b):
    wp = jnp.pad(conv_w, ((0, 1), (0, 0)))
    row = lambda v: v.reshape(1, CONV_CH)
    hb = CONV_T // CONV_HALO
    return pl.pallas_call(
        _conv_prompt_kernel,
        grid=(N_PROMPT // CONV_T,),
        in_specs=[
            pl.BlockSpec((CONV_T, CONV_CH), lambda i: (i, 0)),
            pl.BlockSpec((CONV_HALO, CONV_CH), lambda i: (jnp.maximum(i * hb - 1, 0), 0)),
            pl.BlockSpec((CONV_WIDTH + 1, CONV_CH), lambda i: (0, 0)),
            pl.BlockSpec((1, CONV_CH), lambda i: (0, 0)),
            pl.BlockSpec((1, CONV_CH), lambda i: (0, 0)),
            pl.BlockSpec((1, CONV_CH), lambda i: (0, 0)),
        ],
        out_specs=pl.BlockSpec((CONV_T, CONV_CH), lambda i: (i, 0)),
        out_shape=jax.ShapeDtypeStruct((N_PROMPT, CONV_CH), BF16),
        scratch_shapes=[pltpu.VMEM((CONV_T + CONV_HALO, CONV_CH), F32), pltpu.VMEM((CONV_T, CONV_CH), F32),
                        pltpu.VMEM((8, CONV_T + 8, CONV_CC), F32)],
        compiler_params=_cparams("parallel"),
        name="conv_prompt",
    )(a, a, wp, row(conv_b), row(ln_g), row(ln_b))


def _conv_sample_kernel(xt_ref, w_ref, b_ref, g_ref, beta_ref, o_ref):
    for t in range(DEC_SEQ):
        acc = jnp.zeros((DEC_BATCH, CONV_CH), F32)
        for k in range(CONV_WIDTH):
            acc = acc + w_ref[k:k + 1, :] * xt_ref[t + k]
        y = acc + b_ref[...]
        o_ref[t] = _ln_silu(y, g_ref[...], beta_ref[...]).astype(BF16)


def _conv_sample(xt, conv_w, conv_b, ln_g, ln_b):
    wp = jnp.pad(conv_w, ((0, 1), (0, 0)))
    row = lambda v: v.reshape(1, CONV_CH)
    npos = CONV_WIDTH - 1 + DEC_SEQ
    return pl.pallas_call(
        _conv_sample_kernel,
        grid=(1,),
        in_specs=[
            pl.BlockSpec((npos, DEC_BATCH, CONV_CH), lambda i: (0, 0, 0)),
            pl.BlockSpec((CONV_WIDTH + 1, CONV_CH), lambda i: (0, 0)),
            pl.BlockSpec((1, CONV_CH), lambda i: (0, 0)),
            pl.BlockSpec((1, CONV_CH), lambda i: (0, 0)),
            pl.BlockSpec((1, CONV_CH), lambda i: (0, 0)),
        ],
        out_specs=pl.BlockSpec((DEC_SEQ, DEC_BATCH, CONV_CH), lambda i: (0, 0, 0)),
        out_shape=jax.ShapeDtypeStruct((DEC_SEQ, DEC_BATCH, CONV_CH), BF16),
        compiler_params=_cparams("arbitrary"),
        name="conv_sample",
    )(xt, wp, row(conv_b), row(ln_g), row(ln_b))


HALF = HEAD_DIM
NEG_BIG = -1e30
CHUNKS_PER_PAGE = PAGE_SIZE // COMP_STRIDE
CHUNK_LANES = COMP_STRIDE * 2 * KV_WIDTH


def _alibi_slope(h):
    return float(2.0 ** (-8.0 * (h + 1) / N_HEADS) * LOG2E)


def _gelu_tanh(x):
    c = np.float32(np.sqrt(2.0 / np.pi))
    return 0.5 * x * (1.0 + jnp.tanh(c * (x + 0.044715 * (x * x * x))))


def _dot_nt(a, b, precision=None):
    return lax.dot_general(a, b, (((1,), (1,)), ((), ())), preferred_element_type=F32, precision=precision)


def _half_mask(shape, half):
    lane = lax.broadcasted_iota(jnp.int32, shape, len(shape) - 1)
    return (lane >= HALF) if half else (lane < HALF)


def _swap_halves(x):
    return pltpu.roll(x, HALF, axis=x.ndim - 1)


def _compress_weights(pos, w1, w2):
    eye2 = jnp.eye(2, dtype=F32)
    base = w1.reshape(2, COMP_STRIDE, HEAD_DIM, CMP_HIDDEN)
    w1p = jnp.einsum('rsde,hg->rshdge', base, eye2).reshape(2, COMP_STRIDE * 2 * HEAD_DIM, 2 * CMP_HIDDEN)
    pe = jnp.broadcast_to(pos.reshape(2, COMP_STRIDE, 1, HEAD_DIM), (2, COMP_STRIDE, 2, HEAD_DIM))
    pe = pe.reshape(2, 1, COMP_STRIDE * 2 * HEAD_DIM)
    w2p = jnp.einsum('ed,hg->hegd', w2, eye2).reshape(2 * CMP_HIDDEN, 2 * HEAD_DIM)
    return w1p.astype(BF16), pe, w2p.astype(BF16)


def _compress_pair(x, pe_ref, w1_ref, w2_ref):
    n = x.shape[0]
    a0 = jnp.dot((x + pe_ref[0]).astype(BF16), w1_ref[0], preferred_element_type=F32)
    a1 = jnp.dot((x + pe_ref[1]).astype(BF16), w1_ref[1], preferred_element_type=F32)
    hid = a0 + pltpu.roll(a1, n - 1, axis=0)
    return jnp.dot(_gelu_tanh(hid).astype(BF16), w2_ref[...], preferred_element_type=F32)


def _overlap_map(nc, ns, nc_pad, ns_pad):
    i = np.arange(nc_pad)[None, :]
    j = np.arange(ns_pad)[:, None]
    c_start, s_start = i * COMP_STRIDE, j * SEL_BLOCK
    m = (c_start <= s_start + SEL_BLOCK - 1) & (c_start + COMP_BLOCK - 1 >= s_start) & (i < nc) & (j < ns)
    return jnp.asarray(m.astype(np.float32))


def _select_blocks(imp_t, q_pos, n_iter):
    nb = imp_t.shape[0]
    blk = lax.broadcasted_iota(jnp.int32, imp_t.shape, 0)
    blk_f = blk.astype(F32)
    cur = q_pos >> 6
    valid = blk * SEL_BLOCK <= q_pos
    forced = (blk == 0) | (blk == cur) | (blk == cur - 1)
    score = jnp.where(valid, jnp.where(forced, 3e38, imp_t), -1.0)
    sel = jnp.zeros(imp_t.shape, F32)
    for _ in range(n_iter):
        mx = jnp.max(score, axis=0, keepdims=True)
        first = jnp.min(jnp.where(score == mx, blk_f, float(nb)), axis=0, keepdims=True)
        pick = (blk_f == first) & (mx >= 0.0)
        sel = jnp.where(pick, 1.0, sel)
        score = jnp.where(pick, -2.0, score)
    return sel


P_CHUNKS = SEQ // COMP_STRIDE
P_NC = (SEQ - COMP_BLOCK) // COMP_STRIDE + 1
P_NS = SEQ // SEL_BLOCK


def _compress_prompt_kernel(*refs):
    c_refs = refs[:COMP_STRIDE]
    pe_ref, w1_ref, w2_ref, o_ref = refs[COMP_STRIDE:]
    x = jnp.concatenate([r[...] for r in c_refs], axis=1)
    o_ref[...] = _compress_pair(x, pe_ref, w1_ref, w2_ref).astype(BF16)


def _compress_prompt(kvc, cw):
    pos_k, w1_k, w2_k, pos_v, w1_v, w2_v = cw
    wk, wv = _compress_weights(pos_k, w1_k, w2_k), _compress_weights(pos_v, w1_v, w2_v)
    w1p, pe, w2p = (jnp.stack([a, b]) for a, b in zip(wk, wv))
    chunks = kvc.reshape(N_TOK // COMP_STRIDE, CHUNK_LANES)
    c_specs = [pl.BlockSpec((P_CHUNKS, LANES), lambda g, s=s: (0, s * 4 + g)) for s in range(COMP_STRIDE)]
    return pl.pallas_call(
        _compress_prompt_kernel,
        grid=(4,),
        in_specs=c_specs + [
            pl.BlockSpec((None, 2, 1, 2048), lambda g: (g // 2, 0, 0, 0)),
            pl.BlockSpec((None, 2, 2048, 256), lambda g: (g // 2, 0, 0, 0)),
            pl.BlockSpec((None, 256, LANES), lambda g: (g // 2, 0, 0)),
        ],
        out_specs=pl.BlockSpec((P_CHUNKS, LANES), lambda g: (0, g)),
        out_shape=jax.ShapeDtypeStruct((P_CHUNKS, 4 * LANES), BF16),
        compiler_params=_cparams("parallel"),
        name="compress_prompt",
    )(*([chunks] * COMP_STRIDE), pe, w1p, w2p)


QB = 256


def _stack_gqa(q_ref, k):
    qa = q_ref[:, 256 * k:256 * k + LANES]
    qb = q_ref[:, 256 * k + LANES:256 * (k + 1)]
    qas = _swap_halves(qa.astype(F32)).astype(BF16)
    qbs = _swap_halves(qb.astype(F32)).astype(BF16)
    order = [qa, qas, qb, qbs] if k % 2 == 0 else [qas, qa, qbs, qb]
    return jnp.concatenate(order, axis=0)


def _pair_up(o0, o1, half):
    return o0 + _swap_halves(o1) if half == 0 else _swap_halves(o0) + o1


def _cmp_prompt_kernel(q_ref, kcv_ref, smap_ref, oc_ref, drop_ref):
    i = pl.program_id(0)
    q0 = i * QB
    q_pos_col = q0 + lax.broadcasted_iota(jnp.int32, (QB, 1), 0)
    c_end = lax.broadcasted_iota(jnp.int32, (1, P_CHUNKS), 1) * COMP_STRIDE + (COMP_BLOCK - 1)
    d_c = q_pos_col - c_end
    ok = d_c >= 0
    d_cf = d_c.astype(F32)
    q_pos_row = q0 + lax.broadcasted_iota(jnp.int32, (1, QB), 1)
    for k in range(N_KV):
        pair, half = k // 2, k % 2
        keep = _half_mask((P_CHUNKS, LANES), half)
        kc = jnp.where(keep, kcv_ref[:, LANES * pair:LANES * (pair + 1)], 0)
        vc = jnp.where(keep, kcv_ref[:, 2 * LANES + LANES * pair:2 * LANES + LANES * (pair + 1)], 0)
        s4 = _dot_nt(_stack_gqa(q_ref, k), kc)
        psum = jnp.zeros((QB, P_CHUNKS), F32)
        outs = []
        for g in range(GQA):
            s = s4[g * QB:(g + 1) * QB] - _alibi_slope(4 * k + g) * d_cf
            s = jnp.where(ok, s, NEG_BIG)
            m = jnp.max(s, axis=-1, keepdims=True)
            e = jnp.where(ok, jnp.exp2(s - m), 0.0)
            den = jnp.sum(e, axis=-1, keepdims=True)
            p = e / jnp.where(den > 0, den, 1.0)
            psum = psum + p
            outs.append(jnp.dot(p.astype(BF16), vc, preferred_element_type=F32))
        oc_ref[:, 256 * k:256 * k + LANES] = _pair_up(outs[0], outs[1], half)
        oc_ref[:, 256 * k + LANES:256 * (k + 1)] = _pair_up(outs[2], outs[3], half)
        imp_t = _dot_nt(smap_ref[...], psum, precision=lax.Precision.HIGHEST)
        sel_t = _select_blocks(imp_t, q_pos_row, N_SEL)
        drop_ref[:, LANES * k:LANES * (k + 1)] = (1.0 - sel_t.T).astype(BF16)


def _cmp_prompt(q, kcv):
    smap_t = _overlap_map(P_NC, P_NS, P_CHUNKS, P_NS)
    return pl.pallas_call(
        _cmp_prompt_kernel,
        grid=(N_PROMPT // QB,),
        in_specs=[
            pl.BlockSpec((QB, Q_WIDTH), lambda i: (i, 0)),
            pl.BlockSpec((P_CHUNKS, 4 * LANES), lambda i: (0, 0)),
            pl.BlockSpec((P_NS, P_CHUNKS), lambda i: (0, 0)),
        ],
        out_specs=[
            pl.BlockSpec((QB, Q_WIDTH), lambda i: (i, 0)),
            pl.BlockSpec((QB, N_KV * P_NS), lambda i: (i, 0)),
        ],
        out_shape=[
            jax.ShapeDtypeStruct((N_PROMPT, Q_WIDTH), F32),
            jax.ShapeDtypeStruct((N_PROMPT, N_KV * P_NS), BF16),
        ],
        compiler_params=_cparams("parallel"),
        name="nsa_cmp_prompt",
    )(q, kcv, smap_t)


TKT = 512
POS_PIECES = 3


def _bf16_pieces(x, n):
    out = []
    for _ in range(n):
        p = float(np.float32(x).astype(jnp.bfloat16))
        out.append(p)
        x = x - p
    return out


def _position_lanes():
    pos = np.arange(SEQ)
    out = np.zeros((2, SEQ, LANES), np.float32)
    for half in range(2):
        base = HALF * (1 - half)
        for j in range(POS_PIECES):
            out[half, :, base + j] = (pos >> 6) * SEL_BLOCK
            out[half, :, base + POS_PIECES + j] = pos & (SEL_BLOCK - 1)
    return jnp.asarray(out, BF16)


def _block_drop_rows():
    m = np.where(np.arange(P_NS)[None, :] == (np.arange(SEQ)[:, None] // SEL_BLOCK), NEG_BIG, 0.0)
    return jnp.asarray(m.astype(np.float32), BF16)


def _selw_kernel(q_ref, g_ref, oc_ref, drop_ref, ksk_ref, vst_ref, kwk_ref, vwt_ref, pos_ref, e_ref, o_ref,
                 qa_ref, m_ref, acc_ref, os_ref):
    i = pl.program_id(0)
    q0 = pl.multiple_of(i * QB, QB)
    lane1 = lax.broadcasted_iota(jnp.int32, (1, LANES), 1)

    for k in range(N_KV):
        half = k % 2
        base = HALF * (1 - half)
        qa = q_ref[:, 256 * k:256 * k + LANES]
        qb = q_ref[:, 256 * k + LANES:256 * (k + 1)]
        qas = _swap_halves(qa.astype(F32)).astype(BF16)
        qbs = _swap_halves(qb.astype(F32)).astype(BF16)
        order = [qa, qas, qb, qbs] if half == 0 else [qas, qa, qbs, qb]
        own = _half_mask((QB, LANES), half)
        parts = []
        for g in range(GQA):
            sv = jnp.zeros((1, LANES), F32)
            pieces = _bf16_pieces(_alibi_slope(4 * k + g), POS_PIECES)
            for j, piece in enumerate(pieces + pieces):
                sv = jnp.where(lane1 == base + j, piece, sv)
            parts.append(jnp.where(own, order[g], sv.astype(BF16)))
        drop = drop_ref[:, LANES * k:LANES * (k + 1)]
        qa_ref[k] = jnp.concatenate([jnp.concatenate(parts, axis=0), jnp.concatenate([drop] * GQA, axis=0)], axis=1)

    key_row = lax.broadcasted_iota(jnp.int32, (TKT, 1), 0)
    q_lane = lax.broadcasted_iota(jnp.int32, (1, GQA * QB), 1) & (QB - 1)
    vrow = lax.broadcasted_iota(jnp.int32, (LANES, 1), 0)

    def tile(kk_ref, vt_ref, row0, selected, mode, first):
        if mode is not None:
            d = (q0 + q_lane) - (row0 + key_row)
            ok = (d >= 0) if mode == "causal" else ((d >= 0) & (d <= WINDOW))
        for k in range(N_KV):
            pair, half = k // 2, k % 2
            own = _half_mask((TKT, LANES), half)
            ka = jnp.where(own, kk_ref[pl.ds(row0, TKT), LANES * pair:LANES * (pair + 1)],
                           pos_ref[half, pl.ds(row0, TKT), :])
            if selected:
                st = _dot_nt(jnp.concatenate([ka, e_ref[pl.ds(row0, TKT), :]], axis=1), qa_ref[k])
            else:
                st = _dot_nt(ka, qa_ref[k, :, 0:LANES])
            if mode is not None:
                st = jnp.where(ok, st, NEG_BIG)
            m_tile = jnp.max(st, axis=0, keepdims=True)
            own_rows = (vrow >= HALF) if half else (vrow < HALF)
            ones_row = jnp.where(vrow == HALF * (1 - half), 1.0, 0.0).astype(BF16)
            va = jnp.where(own_rows, vt_ref[LANES * pair:LANES * (pair + 1), pl.ds(row0, TKT)], ones_row)
            if first:
                m_new = m_tile
            else:
                m_old = m_ref[k]
                m_new = jnp.maximum(m_old, m_tile)
            p = jnp.exp2(st - m_new).astype(BF16)
            pv = jnp.dot(va, p, preferred_element_type=F32)
            if first:
                acc_ref[k] = pv
            else:
                acc_ref[k] = jnp.exp2(m_old - m_new) * acc_ref[k] + pv
            m_ref[k] = m_new

    def normalised(k):
        half = k % 2
        den_row = HALF * (1 - half)
        return acc_ref[k] / acc_ref[k, den_row:den_row + 1, :]

    n_below = lax.div(i, jnp.int32(TKT // QB))
    diag0 = pl.multiple_of(n_below * TKT, TKT)
    tile(ksk_ref, vst_ref, diag0, True, "causal", True)

    def body(t, carry):
        tile(ksk_ref, vst_ref, pl.multiple_of(t * TKT, TKT), True, None, False)
        return carry

    lax.fori_loop(0, n_below, body, 0)
    for k in range(N_KV):
        os_ref[k] = normalised(k)

    tile(kwk_ref, vwt_ref, diag0, False, "window", True)

    @pl.when(n_below >= 1)
    def _():
        tile(kwk_ref, vwt_ref, pl.multiple_of(diag0 - TKT, TKT), False, "window", False)

    g_t = g_ref[...].T
    low = _half_mask((QB, LANES), 0)
    for k in range(N_KV):
        half = k % 2
        o_w = normalised(k)
        contrib = []
        for g in range(GQA):
            h = 4 * k + g
            cols = slice(g * QB, (g + 1) * QB)
            c_t = (os_ref[k, :, cols] * g_t[N_HEADS + h:N_HEADS + h + 1, :]
                   + o_w[:, cols] * g_t[2 * N_HEADS + h:2 * N_HEADS + h + 1, :])
            contrib.append(c_t.T)
        for pr in range(2):
            h0 = 4 * k + 2 * pr
            c0, c1 = contrib[2 * pr], contrib[2 * pr + 1]
            both = jnp.where(low, c0, _swap_halves(c1)) if half == 0 else jnp.where(low, _swap_halves(c0), c1)
            gate_c = jnp.where(low, g_ref[:, h0:h0 + 1], g_ref[:, h0 + 1:h0 + 2])
            cols = slice(256 * k + LANES * pr, 256 * k + LANES * (pr + 1))
            o_ref[:, cols] = (gate_c * oc_ref[:, cols] + both).astype(BF16)


def _selw_prompt(q, gates, o_c, drop, ks_k, vs_t, kw_k, vw_t):
    resident = lambda shape: pl.BlockSpec(shape, lambda i: (0,) * len(shape), pipeline_mode=pl.Buffered(1))
    return pl.pallas_call(
        _selw_kernel,
        grid=(N_PROMPT // QB,),
        in_specs=[
            pl.BlockSpec((QB, Q_WIDTH), lambda i: (i, 0)),
            pl.BlockSpec((QB, LANES), lambda i: (i, 0)),
            pl.BlockSpec((QB, Q_WIDTH), lambda i: (i, 0)),
            pl.BlockSpec((QB, N_KV * P_NS), lambda i: (i, 0)),
            resident((N_PROMPT, KV_WIDTH)),
            resident((KV_WIDTH, N_PROMPT)),
            resident((N_PROMPT, KV_WIDTH)),
            resident((KV_WIDTH, N_PROMPT)),
            resident((2, SEQ, LANES)),
            resident((SEQ, LANES)),
        ],
        out_specs=pl.BlockSpec((QB, Q_WIDTH), lambda i: (i, 0)),
        out_shape=jax.ShapeDtypeStruct((N_PROMPT, Q_WIDTH), BF16),
        scratch_shapes=[
            pltpu.VMEM((N_KV, GQA * QB, 2 * LANES), BF16),
            pltpu.VMEM((N_KV, 1, GQA * QB), F32),
            pltpu.VMEM((N_KV, LANES, GQA * QB), F32),
            pltpu.VMEM((N_KV, LANES, GQA * QB), F32),
        ],
        compiler_params=_cparams("parallel"),
        name="nsa_selw_prompt",
    )(q, gates, o_c, drop, ks_k, vs_t, kw_k, vw_t, _position_lanes(), _block_drop_rows())


N_PAGES = PAST_LEN // PAGE_SIZE
S_CHUNKS = PAST_LEN // COMP_STRIDE
S_NC = (PAST_LEN + DEC_SEQ - COMP_BLOCK) // COMP_STRIDE + 1
S_NS = -(-(PAST_LEN + DEC_SEQ) // SEL_BLOCK)
S_KEYS = 17 * LANES
WIN_BUF = min(WINDOW, PAST_LEN)
S_WKEYS = 5 * LANES
ROWS16 = GQA * DEC_SEQ
HEADS8 = 2 * N_KV


def _row_softmax(s, ok):
    s = jnp.where(ok, s, NEG_BIG)
    m = jnp.max(s, axis=-1, keepdims=True)
    e = jnp.where(ok, jnp.exp2(s - m), 0.0)
    den = jnp.sum(e, axis=-1, keepdims=True)
    return e / jnp.where(den > 0, den, 1.0)


def _nsa_sample_kernel(pt_ref, *refs):
    cp = refs[:N_PAGES]
    sp = refs[N_PAGES:2 * N_PAGES]
    (win_ref, news_ref, neww_ref, q_ref, g_ref, pe_ref, w1_ref, w2_ref, smap_ref, expand_ref,
     o_ref, ks_ref, kw_ref) = refs[2 * N_PAGES:]
    del pt_ref

    for c in range(HEADS8):
        kv, h = c // N_KV, c % N_KV
        for p in range(N_PAGES):
            ks_ref[c, :, PAGE_SIZE * p:PAGE_SIZE * (p + 1)] = sp[p][kv, h].astype(BF16)
        ks_ref[c, :, PAST_LEN:] = news_ref[c]
        kw_ref[c, :, 0:WIN_BUF] = win_ref[kv, h].astype(BF16)
        kw_ref[c, :, WIN_BUF:] = neww_ref[c]

    kcv = []
    for c in range(HEADS8):
        kv = c // N_KV
        x = jnp.concatenate([cp[p][c] for p in range(N_PAGES)], axis=0)
        a0 = jnp.dot((x + pe_ref[kv, 0]).astype(BF16), w1_ref[kv, 0], preferred_element_type=F32)
        a1 = jnp.dot((x + pe_ref[kv, 1]).astype(BF16), w1_ref[kv, 1], preferred_element_type=F32)
        hid = a0 + pltpu.roll(a1, S_CHUNKS - 1, axis=0)
        kcv.append(jnp.dot(_gelu_tanh(hid).astype(BF16), w2_ref[kv], preferred_element_type=F32).astype(BF16))

    row = lax.broadcasted_iota(jnp.int32, (ROWS16, 1), 0)
    q_pos = PAST_LEN + (row & (DEC_SEQ - 1))
    grp = row >> 2

    def slopes(k):
        s = jnp.full((ROWS16, 1), _alibi_slope(4 * k), F32)
        for g in range(1, GQA):
            s = jnp.where(grp == g, _alibi_slope(4 * k + g), s)
        return s

    c_end = lax.broadcasted_iota(jnp.int32, (1, S_CHUNKS), 1) * COMP_STRIDE + (COMP_BLOCK - 1)
    d_c = q_pos - c_end
    o_cmp, psums = [], []
    for k in range(N_KV):
        s = _dot_nt(q_ref[k], kcv[k]) - slopes(k) * d_c.astype(F32)
        p = _row_softmax(s, d_c >= 0)
        o_cmp.append(jnp.dot(p.astype(BF16), kcv[N_KV + k], preferred_element_type=F32))
        ps = p
        for g in range(1, GQA):
            ps = ps + pltpu.roll(p, DEC_SEQ * g, axis=0)
        psums.append(ps)
    psum_all = jnp.concatenate(psums + [jnp.zeros((LANES - N_KV * ROWS16, S_CHUNKS), F32)], axis=0)
    imp_t = _dot_nt(smap_ref[...], psum_all, precision=lax.Precision.HIGHEST)
    col = lax.broadcasted_iota(jnp.int32, (1, LANES), 1)
    sel_t = _select_blocks(imp_t, PAST_LEN + (col & (DEC_SEQ - 1)), N_SEL)
    sel = sel_t.T.astype(BF16)

    key_s = lax.broadcasted_iota(jnp.int32, (1, S_KEYS), 1)
    d_s = q_pos - key_s
    idx_w = lax.broadcasted_iota(jnp.int32, (1, S_WKEYS), 1)
    d_w = q_pos - (PAST_LEN - WIN_BUF + idx_w)
    ok_w = (d_w >= 0) & (d_w <= WINDOW)
    for k in range(N_KV):
        chosen = jnp.dot(sel[ROWS16 * k:ROWS16 * (k + 1)], expand_ref[...], preferred_element_type=F32)
        s = jnp.dot(q_ref[k], ks_ref[k], preferred_element_type=F32) - slopes(k) * d_s.astype(F32)
        p = _row_softmax(s, (chosen > 0.5) & (d_s >= 0))
        o_sel = _dot_nt(p.astype(BF16), ks_ref[N_KV + k])
        s = jnp.dot(q_ref[k], kw_ref[k], preferred_element_type=F32) - slopes(k) * d_w.astype(F32)
        p = _row_softmax(s, ok_w)
        o_win = _dot_nt(p.astype(BF16), kw_ref[N_KV + k])
        o_ref[k] = (g_ref[k][:, 0:1] * o_cmp[k] + g_ref[k][:, 1:2] * o_sel + g_ref[k][:, 2:3] * o_win)


def _nsa_sample(page_table, cache_c, cache_s, win_prev, q_s, gates_s, kvs_new, kvw_new, cw):
    pos_k, w1_k, w2_k, pos_v, w1_v, w2_v = cw
    flat = COMP_STRIDE * HEAD_DIM
    n_phys = cache_c.shape[1]
    w1 = jnp.stack([w1_k, w1_v]).reshape(2, 2, flat, CMP_HIDDEN).astype(BF16)
    pe = jnp.stack([pos_k, pos_v]).reshape(2, 2, 1, flat)
    w2 = jnp.stack([w2_k, w2_v]).astype(BF16)
    chunks = cache_c.reshape(n_phys, CHUNKS_PER_PAGE, COMP_STRIDE, HEADS8, HEAD_DIM)
    chunks = chunks.transpose(0, 3, 1, 2, 4).reshape(n_phys, HEADS8, CHUNKS_PER_PAGE, flat)
    slc_t = cache_s.transpose(0, 1, 3, 4, 5, 2)
    win_t = win_prev.transpose(0, 1, 3, 4, 5, 2)
    qh = q_s.reshape(DEC_SEQ, DEC_BATCH, N_KV, GQA, HEAD_DIM).transpose(1, 2, 3, 0, 4)
    q16 = qh.reshape(DEC_BATCH, N_KV, ROWS16, HEAD_DIM)
    gh = gates_s[:, :3 * N_HEADS].reshape(DEC_SEQ, DEC_BATCH, 3, N_KV, GQA).transpose(1, 3, 4, 0, 2)
    g16 = jnp.pad(gh.reshape(DEC_BATCH, N_KV, ROWS16, 3), ((0, 0), (0, 0), (0, 0), (0, LANES - 3)))

    def new_rows_t(a, width):
        a = a.reshape(DEC_SEQ, DEC_BATCH, HEADS8, HEAD_DIM).transpose(1, 2, 3, 0).astype(BF16)
        return jnp.pad(a, ((0, 0), (0, 0), (0, 0), (0, width - DEC_SEQ)))

    smap_t = _overlap_map(S_NC, S_NS, S_CHUNKS, LANES)
    expand = jnp.asarray((np.arange(LANES)[:, None] == (np.arange(S_KEYS)[None, :] // SEL_BLOCK)), BF16)
    pt = page_table.reshape(-1).astype(jnp.int32)
    c_specs = [pl.BlockSpec((None, HEADS8, CHUNKS_PER_PAGE, flat), lambda b, pt, p=p: (pt[b * N_PAGES + p], 0, 0, 0))
               for p in range(N_PAGES)]
    s_specs = [pl.BlockSpec((None, None, 2, N_KV, HEAD_DIM, PAGE_SIZE),
                            lambda b, pt, p=p: (0, pt[b * N_PAGES + p], 0, 0, 0, 0)) for p in range(N_PAGES)]
    per_b4 = lambda shape: pl.BlockSpec((None,) + shape, lambda b, pt: (b, 0, 0, 0))
    const = lambda shape: pl.BlockSpec(shape, lambda b, pt: (0,) * len(shape))
    o16 = pl.pallas_call(
        _nsa_sample_kernel,
        grid_spec=pltpu.PrefetchScalarGridSpec(
            num_scalar_prefetch=1,
            grid=(DEC_BATCH,),
            in_specs=c_specs + s_specs + [
                pl.BlockSpec((None, None, 2, N_KV, HEAD_DIM, WIN_BUF), lambda b, pt: (0, b, 0, 0, 0, 0)),
                per_b4((HEADS8, HEAD_DIM, S_KEYS - PAST_LEN)),
                per_b4((HEADS8, HEAD_DIM, S_WKEYS - WIN_BUF)),
                per_b4((N_KV, ROWS16, HEAD_DIM)),
                per_b4((N_KV, ROWS16, LANES)),
                const((2, 2, 1, flat)),
                const((2, 2, flat, CMP_HIDDEN)),
                const((2, CMP_HIDDEN, HEAD_DIM)),
                const((LANES, S_CHUNKS)),
                const((LANES, S_KEYS)),
            ],
            out_specs=per_b4((N_KV, ROWS16, HEAD_DIM)),
            scratch_shapes=[pltpu.VMEM((HEADS8, HEAD_DIM, S_KEYS), BF16), pltpu.VMEM((HEADS8, HEAD_DIM, S_WKEYS), BF16)],
        ),
        out_shape=jax.ShapeDtypeStruct((DEC_BATCH, N_KV, ROWS16, HEAD_DIM), F32),
        compiler_params=_cparams("parallel"),
        name="nsa_sample",
    )(pt, *([chunks] * N_PAGES), *([slc_t] * N_PAGES), win_t,
      new_rows_t(kvs_new, S_KEYS - PAST_LEN), new_rows_t(kvw_new, S_WKEYS - WIN_BUF),
      q16, g16, pe, w1, w2, smap_t, expand)
    o = o16.reshape(DEC_BATCH, N_KV, GQA, DEC_SEQ, HEAD_DIM)
    return o.transpose(3, 0, 1, 2, 4).reshape(N_SAMPLE, Q_WIDTH).astype(BF16)


OUT_TM = 512
ROUTE_E0, ROUTE_E1, ROUTE_W0, ROUTE_W1, ROUTE_R0, ROUTE_R1 = range(6)


def _layer_norm(x, g, b):
    mu = jnp.mean(x, axis=-1, keepdims=True)
    xc = x - mu
    var = jnp.mean(xc * xc, axis=-1, keepdims=True)
    return xc * lax.rsqrt(var + LN_EPS) * g + b


def _first_max(vals, lane_f):
    mx = jnp.max(vals, axis=-1, keepdims=True)
    idx = jnp.min(jnp.where(vals == mx, lane_f, float(LANES)), axis=-1, keepdims=True)
    return mx, idx


def _out_route_kernel(conv_ref, attn_ref, x_ref, wo_ref, g1_ref, b1_ref, wr_ref, br_ref,
                      h_ref, route_ref, counts_ref, run_ref):
    i = pl.program_id(0)

    @pl.when(i == 0)
    def _():
        run_ref[...] = jnp.zeros_like(run_ref)

    mix = jnp.dot(conv_ref[...], wo_ref[0:CONV_CH, :], preferred_element_type=F32)
    mix = mix + jnp.dot(attn_ref[...], wo_ref[CONV_CH:, :], preferred_element_type=F32)
    h = _layer_norm(ALPHA * x_ref[...] + mix, g1_ref[...], b1_ref[...])
    h_ref[...] = h

    logit = jnp.dot(h, wr_ref[...], preferred_element_type=F32, precision=lax.Precision.HIGHEST) + br_ref[...]
    lane = lax.broadcasted_iota(jnp.int32, logit.shape, 1)
    lane_f = lane.astype(F32)
    is_grp = (lane >= N_EXPERTS) & (lane < N_EXPERTS + N_GROUPS)
    g_logit = jnp.where(is_grp, logit, NEG_BIG)
    g_top, g_lane = _first_max(g_logit, lane_f)
    lse = g_top + jnp.log(jnp.sum(jnp.where(is_grp, jnp.exp(g_logit - g_top), 0.0), axis=-1, keepdims=True))
    g_p = jnp.exp(g_top - lse)
    g_idx = g_lane.astype(jnp.int32) - N_EXPERTS
    in_grp = (lane >> 3) == g_idx
    e_logit = jnp.where(in_grp, logit, NEG_BIG)
    e0, i0 = _first_max(e_logit, lane_f)
    e1, i1 = _first_max(jnp.where(lane_f == i0, NEG_BIG, e_logit), lane_f)
    z1 = jnp.exp(e1 - e0)
    w0 = g_p / (1.0 + z1)
    w1 = g_p * z1 / (1.0 + z1)

    hit = jnp.where((lane_f == i0) | (lane_f == i1), 1.0, 0.0)
    r = lax.broadcasted_iota(jnp.int32, (OUT_TM, OUT_TM), 0)
    c = lax.broadcasted_iota(jnp.int32, (OUT_TM, OUT_TM), 1)
    lower = jnp.where(c < r, 1.0, 0.0).astype(BF16)
    before = jnp.dot(lower, hit.astype(BF16), preferred_element_type=F32) + run_ref[...]
    r0 = jnp.sum(jnp.where(lane_f == i0, before, 0.0), axis=-1, keepdims=True)
    r1 = jnp.sum(jnp.where(lane_f == i1, before, 0.0), axis=-1, keepdims=True)
    run_ref[...] = run_ref[...] + jnp.sum(hit, axis=0, keepdims=True)
    counts_ref[...] = jnp.broadcast_to(run_ref[...], counts_ref.shape)

    rec = jnp.zeros(logit.shape, F32)
    for col_id, v in ((ROUTE_E0, i0), (ROUTE_E1, i1), (ROUTE_W0, w0), (ROUTE_W1, w1), (ROUTE_R0, r0), (ROUTE_R1, r1)):
        rec = jnp.where(lane == col_id, v, rec)
    route_ref[...] = rec


def _out_route(conv, attn, x, w_out, ln_g, ln_b, w_grp, b_grp, w_exp, b_exp):
    n = x.shape[0]
    wr = jnp.pad(jnp.concatenate([w_exp, w_grp], axis=1), ((0, 0), (0, LANES - N_EXPERTS - N_GROUPS)))
    br = jnp.pad(jnp.concatenate([b_exp, b_grp]), (0, LANES - N_EXPERTS - N_GROUPS)).reshape(1, LANES)
    const = lambda shape: pl.BlockSpec(shape, lambda i: (0, 0))
    return pl.pallas_call(
        _out_route_kernel,
        grid=(n // OUT_TM,),
        in_specs=[
            pl.BlockSpec((OUT_TM, CONV_CH), lambda i: (i, 0)),
            pl.BlockSpec((OUT_TM, Q_WIDTH), lambda i: (i, 0)),
            pl.BlockSpec((OUT_TM, D_MODEL), lambda i: (i, 0)),
            pl.BlockSpec((D_MODEL, D_MODEL), lambda i: (0, 0), pipeline_mode=pl.Buffered(1)),
            const((1, D_MODEL)),
            const((1, D_MODEL)),
            const((D_MODEL, LANES)),
            const((1, LANES)),
        ],
        out_specs=[
            pl.BlockSpec((OUT_TM, D_MODEL), lambda i: (i, 0)),
            pl.BlockSpec((OUT_TM, LANES), lambda i: (i, 0)),
            const((8, LANES)),
        ],
        out_shape=[
            jax.ShapeDtypeStruct((n, D_MODEL), F32),
            jax.ShapeDtypeStruct((n, LANES), F32),
            jax.ShapeDtypeStruct((8, LANES), F32),
        ],
        scratch_shapes=[pltpu.VMEM((1, LANES), F32)],
        compiler_params=_cparams("arbitrary"),
        name="out_ln1_route",
    )(conv, attn, x, w_out.astype(BF16), ln_g.reshape(1, D_MODEL), ln_b.reshape(1, D_MODEL), wr, br)


MOE_BLK = 128
N_ASSIGN = N_TOK * TOP_K
MOE_BLOCKS = -(-(N_ASSIGN + N_EXPERTS * (MOE_BLK - 1)) // MOE_BLK)
MOE_ROWS = MOE_BLOCKS * MOE_BLK
CMB_TM = 128


def _gather_rows(src_hbm, dst_ref, sem, idx_ref, base, n_rows):
    def body(r, carry):
        pltpu.make_async_copy(src_hbm.at[pl.ds(idx_ref[base + r], 1)], dst_ref.at[pl.ds(r, 1)], sem).start()
        return carry
    lax.fori_loop(0, n_rows, body, 0)


def _wait_rows(src_hbm, dst_ref, sem):
    pltpu.make_async_copy(src_hbm.at[pl.ds(0, dst_ref.shape[0])], dst_ref, sem).wait()


def _wait_row_count(src_hbm, dst_ref, sem, n_rows):
    p = MOE_BLK
    while p >= 1:
        @pl.when((n_rows & p) != 0)
        def _(p=p):
            pltpu.make_async_copy(src_hbm.at[pl.ds(0, p)], dst_ref.at[pl.ds(0, p)], sem).wait()
        p //= 2


def _moe_kernel(blk_e_ref, row_tok_ref, blk_rows_ref, h_hbm, wg_ref, wu_ref, wd_ref, y_ref,
                xbuf, sems, wg_b, wu_b, wd_b):
    i = pl.program_id(0)
    slot = i % 2
    rows = blk_rows_ref[i]
    nxt = jnp.minimum(i + 1, MOE_BLOCKS - 1)
    rows_next = jnp.where(i + 1 < MOE_BLOCKS, blk_rows_ref[nxt], 0)

    @pl.when(i == 0)
    def _():
        xbuf[...] = jnp.zeros_like(xbuf)
        _gather_rows(h_hbm, xbuf.at[0], sems.at[0], row_tok_ref, 0, rows)

    @pl.when(rows_next > 0)
    def _():
        _gather_rows(h_hbm, xbuf.at[1 - slot], sems.at[1 - slot], row_tok_ref, (i + 1) * MOE_BLK, rows_next)

    @pl.when((i == 0) | (blk_e_ref[i] != blk_e_ref[jnp.maximum(i - 1, 0)]))
    def _():
        wg_b[...] = wg_ref[...].astype(BF16)
        wu_b[...] = wu_ref[...].astype(BF16)
        wd_b[...] = wd_ref[...].astype(BF16)

    _wait_row_count(h_hbm, xbuf.at[slot], sems.at[slot], rows)

    @pl.when(rows > 0)
    def _():
        x = xbuf[slot].astype(BF16)
        hg = jnp.dot(x, wg_b[...], preferred_element_type=F32)
        hu = jnp.dot(x, wu_b[...], preferred_element_type=F32)
        act = (hg * jax.nn.sigmoid(hg) * hu).astype(BF16)
        y_ref[...] = jnp.dot(act, wd_b[...], preferred_element_type=F32)

    @pl.when(rows == 0)
    def _():
        y_ref[...] = jnp.zeros_like(y_ref)


def _moe(h, blk_e, row_tok, blk_rows, w_gate, w_up, w_down):
    wspec = lambda shape: pl.BlockSpec((None,) + shape, lambda i, be, rt, br: (be[i], 0, 0))
    return pl.pallas_call(
        _moe_kernel,
        grid_spec=pltpu.PrefetchScalarGridSpec(
            num_scalar_prefetch=3,
            grid=(MOE_BLOCKS,),
            in_specs=[
                pl.BlockSpec(memory_space=pl.ANY),
                wspec((D_MODEL, MOE_D_FF)),
                wspec((D_MODEL, MOE_D_FF)),
                wspec((MOE_D_FF, D_MODEL)),
            ],
            out_specs=pl.BlockSpec((MOE_BLK, D_MODEL), lambda i, be, rt, br: (i, 0)),
            scratch_shapes=[
                pltpu.VMEM((2, MOE_BLK, D_MODEL), F32),
                pltpu.SemaphoreType.DMA((2,)),
                pltpu.VMEM((D_MODEL, MOE_D_FF), BF16),
                pltpu.VMEM((D_MODEL, MOE_D_FF), BF16),
                pltpu.VMEM((MOE_D_FF, D_MODEL), BF16),
            ],
        ),
        out_shape=jax.ShapeDtypeStruct((MOE_ROWS, D_MODEL), F32),
        compiler_params=_cparams("arbitrary"),
        name="moe_ffn",
    )(blk_e, row_tok, blk_rows, h, w_gate, w_up, w_down)


def _combine_kernel(dest_ref, y_hbm, h_ref, route_ref, g2_ref, b2_ref, o_ref, ybuf, sems):
    i = pl.program_id(0)
    n_steps = pl.num_programs(0)
    slot = i % 2

    @pl.when(i == 0)
    def _():
        _gather_rows(y_hbm, ybuf.at[0], sems.at[0], dest_ref, 0, TOP_K * CMB_TM)

    @pl.when(i + 1 < n_steps)
    def _():
        _gather_rows(y_hbm, ybuf.at[1 - slot], sems.at[1 - slot], dest_ref, (i + 1) * TOP_K * CMB_TM, TOP_K * CMB_TM)

    _wait_rows(y_hbm, ybuf.at[slot], sems.at[slot])
    w0 = route_ref[:, ROUTE_W0:ROUTE_W0 + 1]
    w1 = route_ref[:, ROUTE_W1:ROUTE_W1 + 1]
    moe = ybuf[slot, 0:CMB_TM] * w0 + ybuf[slot, CMB_TM:2 * CMB_TM] * w1
    o_ref[...] = _layer_norm(ALPHA * h_ref[...] + moe, g2_ref[...], b2_ref[...])


def _combine(dest, y_rows, h, route, ln_g, ln_b):
    n = h.shape[0]
    return pl.pallas_call(
        _combine_kernel,
        grid_spec=pltpu.PrefetchScalarGridSpec(
            num_scalar_prefetch=1,
            grid=(n // CMB_TM,),
            in_specs=[
                pl.BlockSpec(memory_space=pl.ANY),
                pl.BlockSpec((CMB_TM, D_MODEL), lambda i, d: (i, 0)),
                pl.BlockSpec((CMB_TM, LANES), lambda i, d: (i, 0)),
                pl.BlockSpec((1, D_MODEL), lambda i, d: (0, 0)),
                pl.BlockSpec((1, D_MODEL), lambda i, d: (0, 0)),
            ],
            out_specs=pl.BlockSpec((CMB_TM, D_MODEL), lambda i, d: (i, 0)),
            scratch_shapes=[pltpu.VMEM((2, TOP_K * CMB_TM, D_MODEL), F32), pltpu.SemaphoreType.DMA((2,))],
        ),
        out_shape=jax.ShapeDtypeStruct((n, D_MODEL), F32),
        compiler_params=_cparams("arbitrary"),
        name="moe_combine_ln2",
    )(dest, y_rows, h, route, ln_g.reshape(1, D_MODEL), ln_b.reshape(1, D_MODEL))


def _dispatch_indices(route, counts):
    eid = route[:, ROUTE_E0:ROUTE_E1 + 1].astype(jnp.int32)
    rank = route[:, ROUTE_R0:ROUTE_R1 + 1].astype(jnp.int32)
    cnt = counts[0, :N_EXPERTS].astype(jnp.int32)
    padded = (cnt + MOE_BLK - 1) // MOE_BLK * MOE_BLK
    pad_end = jnp.cumsum(padded)
    pad_start = pad_end - padded
    dest = pad_start[eid] + rank
    tok = jnp.broadcast_to(jnp.arange(N_TOK, dtype=jnp.int32)[:, None], dest.shape)
    row_tok = jnp.zeros((MOE_ROWS,), jnp.int32).at[dest.reshape(-1)].set(tok.reshape(-1))
    blk_start = jnp.arange(MOE_BLOCKS, dtype=jnp.int32) * MOE_BLK
    blk_e = jnp.minimum(jnp.sum(pad_end[None, :] <= blk_start[:, None], axis=1), N_EXPERTS - 1).astype(jnp.int32)
    blk_rows = jnp.clip(cnt[blk_e] - (blk_start - pad_start[blk_e]), 0, MOE_BLK).astype(jnp.int32)
    dest_tiles = dest.reshape(N_TOK // CMB_TM, CMB_TM, TOP_K).transpose(0, 2, 1).reshape(-1)
    return blk_e, row_tok, blk_rows, dest_tiles


def kernel(x_prompt, x_sample, cache_cmp_kv, cache_slc_kv, state_win_kv, state_conv, page_table, w_in, conv_w, conv_b, conv_ln_g, conv_ln_b, cmp_pos_k, cmp_w1_k, cmp_w2_k, cmp_pos_v, cmp_w1_v, cmp_w2_v, w_out, ln1_g, ln1_b, w_grp, b_grp, w_exp, b_exp, w_gate, w_up, w_down, ln2_g, ln2_b):
    xs_t = x_sample.transpose(1, 0, 2).reshape(N_SAMPLE, D_MODEL)
    x = jnp.concatenate([x_prompt.reshape(N_PROMPT, D_MODEL), xs_t], axis=0)
    glu, q, kvc, kvs, kvw, gates, ks_k, vs_t, kw_k, vw_t = _proj(x, w_in[0])
    conv_p = _conv_prompt(glu, conv_w[0], conv_b[0], conv_ln_g[0], conv_ln_b[0])
    a_s = glu[N_PROMPT:].reshape(DEC_SEQ, DEC_BATCH, CONV_CH)
    xt = jnp.concatenate([state_conv[0].transpose(1, 0, 2), a_s], axis=0)
    conv_s = _conv_sample(xt, conv_w[0], conv_b[0], conv_ln_g[0], conv_ln_b[0])
    cw = (cmp_pos_k[0], cmp_w1_k[0], cmp_w2_k[0], cmp_pos_v[0], cmp_w1_v[0], cmp_w2_v[0])
    kcv_p = _compress_prompt(kvc, cw)
    o_c, drop = _cmp_prompt(q, kcv_p)
    attn_p = _selw_prompt(q, gates, o_c, drop, ks_k, vs_t, kw_k, vw_t)
    attn_s = _nsa_sample(page_table, cache_cmp_kv, cache_slc_kv, state_win_kv, q[N_PROMPT:],
                         gates[N_PROMPT:], kvs[N_PROMPT:], kvw[N_PROMPT:], cw)
    conv = jnp.concatenate([conv_p, conv_s.reshape(N_SAMPLE, CONV_CH)], axis=0)
    attn = jnp.concatenate([attn_p, attn_s], axis=0)
    h1, route, counts = _out_route(conv, attn, x, w_out[0], ln1_g[0], ln1_b[0], w_grp[0], b_grp[0],
                                   w_exp[0], b_exp[0])
    blk_e, row_tok, blk_rows, dest_tiles = _dispatch_indices(route, counts)
    y_rows = _moe(h1, blk_e, row_tok, blk_rows, w_gate[0], w_up[0], w_down[0])
    y = _combine(dest_tiles, y_rows, h1, route, ln2_g[0], ln2_b[0])

    def batch_major(a):
        return a[N_PROMPT:].reshape(DEC_SEQ, DEC_BATCH, -1).transpose(1, 0, 2)

    kv_shape = (2, N_KV, HEAD_DIM)
    win_len = min(WINDOW, SEQ)
    y_prompt = y[:N_PROMPT].reshape(1, SEQ, D_MODEL)
    y_sample = batch_major(y)
    p_cmp = kvc[:N_PROMPT].reshape(1, 1, SEQ, *kv_shape)
    p_slc = kvs[:N_PROMPT].reshape(1, 1, SEQ, *kv_shape)
    p_win = kvw[N_PROMPT - win_len:N_PROMPT].reshape(1, 1, win_len, *kv_shape)
    p_conv = glu[N_PROMPT - (CONV_WIDTH - 1):N_PROMPT].reshape(1, 1, CONV_WIDTH - 1, CONV_CH)
    s_cmp = batch_major(kvc).reshape(1, DEC_BATCH, DEC_SEQ, *kv_shape)
    s_slc = batch_major(kvs).reshape(1, DEC_BATCH, DEC_SEQ, *kv_shape)
    s_win = jnp.concatenate([state_win_kv[0][:, DEC_SEQ:],
                             batch_major(kvw).reshape(DEC_BATCH, DEC_SEQ, *kv_shape)], axis=1)[None]
    s_conv = jnp.concatenate([state_conv[0][:, DEC_SEQ:], batch_major(glu)], axis=1)[None]
    return (y_prompt, y_sample, p_cmp, p_slc, p_win, p_conv, s_cmp, s_slc, s_win, s_conv)
```

```python
import numpy as np
import jax
import jax.numpy as jnp
from jax import lax
from jax.experimental import pallas as pl
from jax.experimental.pallas import tpu as pltpu

F32 = jnp.float32
BF16 = jnp.bfloat16

D_MODEL = 2048
SEQ = 8192
DEC_BATCH = 128
DEC_SEQ = 4
PAST_LEN = 2048
PAGE_SIZE = 128
CONV_CH = 1024
CONV_WIDTH = 31
N_HEADS = 16
HEAD_DIM = 64
N_KV = 4
GQA = 4
Q_WIDTH = 1024
KV_WIDTH = 256
COMP_BLOCK = 32
COMP_STRIDE = 16
CMP_HIDDEN = 128
SEL_BLOCK = 64
N_SEL = 16
WINDOW = 512
N_GROUPS = 8
EXPERTS_PER_GROUP = 8
N_EXPERTS = 64
TOP_K = 2
MOE_D_FF = 512
LN_EPS = 1e-5
ALPHA = 2.0 ** 0.25
LOG2E = 1.4426950408889634

N_PROMPT = SEQ
N_SAMPLE = DEC_BATCH * DEC_SEQ
N_TOK = N_PROMPT + N_SAMPLE

LANES = 128
VMEM_LIMIT = 48 * 1024 * 1024


def _cparams(*sem):
    return pltpu.CompilerParams(dimension_semantics=sem, vmem_limit_bytes=VMEM_LIMIT)


PROJ_TM = 512
PROJ_TN = 512
PROJ_VMEM_LIMIT = 56 * 1024 * 1024


def _proj_kernel(x_ref, w_ref, wg_ref, glu_ref, q_ref, kvc_ref, kvs_ref, kvw_ref, g_ref,
                 ksk_ref, vst_ref, kwk_ref, vwt_ref, xb_ref):
    xb_ref[...] = x_ref[...].astype(BF16)

    def cols(j):
        return jnp.dot(xb_ref[...], w_ref[:, PROJ_TN * j:PROJ_TN * (j + 1)], preferred_element_type=F32)

    for j in range(4):
        acc = cols(j)
        glu_ref[:, 256 * j:256 * (j + 1)] = acc[:, :256] * jax.nn.sigmoid(acc[:, 256:])
    for j in range(2):
        q_ref[:, PROJ_TN * j:PROJ_TN * (j + 1)] = (cols(4 + j) * (HEAD_DIM ** -0.5 * LOG2E)).astype(BF16)
    kvc_ref[...] = cols(6)
    acc = cols(7)
    kvs_ref[...] = acc
    ksk_ref[...] = acc[:, :KV_WIDTH].astype(BF16)
    vst_ref[...] = acc[:, KV_WIDTH:].T.astype(BF16)
    acc = cols(8)
    kvw_ref[...] = acc
    kwk_ref[...] = acc[:, :KV_WIDTH].astype(BF16)
    vwt_ref[...] = acc[:, KV_WIDTH:].T.astype(BF16)
    g_ref[...] = jax.nn.sigmoid(jnp.dot(xb_ref[...], wg_ref[...], preferred_element_type=F32))


def _proj(x, w_in):
    n = x.shape[0]
    o1 = 2 * CONV_CH
    o2 = o1 + Q_WIDTH
    o3 = o2 + 6 * KV_WIDTH
    wa, wb = w_in[:, :CONV_CH], w_in[:, CONV_CH:o1]
    glu_cols = [jnp.concatenate([wa[:, 256 * s:256 * (s + 1)], wb[:, 256 * s:256 * (s + 1)]], axis=1)
                for s in range(4)]
    w_cat = jnp.concatenate(glu_cols + [w_in[:, o1:o3]], axis=1).astype(BF16)
    wg = w_in[:, o3:].reshape(D_MODEL, N_HEADS, 3).transpose(0, 2, 1).reshape(D_MODEL, 3 * N_HEADS)
    wg = jnp.pad(wg, ((0, 0), (0, LANES - 3 * N_HEADS))).astype(BF16)
    rows = lambda width: pl.BlockSpec((PROJ_TM, width), lambda i: (i, 0))
    cols_t = pl.BlockSpec((KV_WIDTH, PROJ_TM), lambda i: (0, i))
    return pl.pallas_call(
        _proj_kernel,
        grid=(n // PROJ_TM,),
        in_specs=[
            rows(D_MODEL),
            pl.BlockSpec((D_MODEL, 9 * PROJ_TN), lambda i: (0, 0), pipeline_mode=pl.Buffered(1)),
            pl.BlockSpec((D_MODEL, LANES), lambda i: (0, 0)),
        ],
        out_specs=[rows(CONV_CH), rows(Q_WIDTH), rows(2 * KV_WIDTH), rows(2 * KV_WIDTH), rows(2 * KV_WIDTH),
                   rows(LANES), rows(KV_WIDTH), cols_t, rows(KV_WIDTH), cols_t],
        out_shape=[
            jax.ShapeDtypeStruct((n, CONV_CH), F32),
            jax.ShapeDtypeStruct((n, Q_WIDTH), BF16),
            jax.ShapeDtypeStruct((n, 2 * KV_WIDTH), F32),
            jax.ShapeDtypeStruct((n, 2 * KV_WIDTH), F32),
            jax.ShapeDtypeStruct((n, 2 * KV_WIDTH), F32),
            jax.ShapeDtypeStruct((n, LANES), F32),
            jax.ShapeDtypeStruct((n, KV_WIDTH), BF16),
            jax.ShapeDtypeStruct((KV_WIDTH, n), BF16),
            jax.ShapeDtypeStruct((n, KV_WIDTH), BF16),
            jax.ShapeDtypeStruct((KV_WIDTH, n), BF16),
        ],
        scratch_shapes=[pltpu.VMEM((PROJ_TM, D_MODEL), BF16)],
        compiler_params=pltpu.CompilerParams(dimension_semantics=("parallel",), vmem_limit_bytes=PROJ_VMEM_LIMIT),
        name="proj",
    )(x, w_cat, wg)


CONV_T = 256
CONV_HALO = 32
CONV_CC = 256


def _ln_silu(y, g, beta):
    mu = jnp.mean(y, axis=-1, keepdims=True)
    yc = y - mu
    var = jnp.mean(yc * yc, axis=-1, keepdims=True)
    z = yc * lax.rsqrt(var + LN_EPS) * g + beta
    return z * jax.nn.sigmoid(z)


def _conv_prompt_kernel(cur_ref, prev_ref, w_ref, b_ref, g_ref, beta_ref, o_ref, xp_ref, y_ref, z_ref):
    i = pl.program_id(0)
    xp_ref[CONV_HALO:, :] = cur_ref[...]

    @pl.when(i == 0)
    def _():
        xp_ref[0:CONV_HALO, :] = jnp.zeros((CONV_HALO, CONV_CH), F32)

    @pl.when(i > 0)
    def _():
        xp_ref[0:CONV_HALO, :] = prev_ref[...]

    off = CONV_HALO - (CONV_WIDTH - 1)
    for c in range(CONV_CH // CONV_CC):
        cols = slice(c * CONV_CC, (c + 1) * CONV_CC)
        for b in range(8):
            n = CONV_T if b == 0 else CONV_T + 8
            z = None
            for k in range(CONV_WIDTH):
                if (off + k) % 8 != b:
                    continue
                a8 = off + k - b
                term = w_ref[k:k + 1, cols] * xp_ref[a8:a8 + n, cols]
                z = term if z is None else z + term
            z_ref[b, 0:n] = z
        acc = z_ref[0, 0:CONV_T]
        for b in range(1, 8):
            acc = acc + z_ref[b, b:b + CONV_T]
        y_ref[:, cols] = acc
    y = y_ref[...] + b_ref[...]
    o_ref[...] = _ln_silu(y, g_ref[...], beta_ref[...]).astype(BF16)


def _conv_prompt(a, conv_w, conv_b, ln_g, ln_b):
    wp = jnp.pad(conv_w, ((0, 1), (0, 0)))
    row = lambda v: v.reshape(1, CONV_CH)
    hb = CONV_T // CONV_HALO
    return pl.pallas_call(
        _conv_prompt_kernel,
        grid=(N_PROMPT // CONV_T,),
        in_specs=[
            pl.BlockSpec((CONV_T, CONV_CH), lambda i: (i, 0)),
            pl.BlockSpec((CONV_HALO, CONV_CH), lambda i: (jnp.maximum(i * hb - 1, 0), 0)),
            pl.BlockSpec((CONV_WIDTH + 1, CONV_CH), lambda i: (0, 0)),
            pl.BlockSpec((1, CONV_CH), lambda i: (0, 0)),
            pl.BlockSpec((1, CONV_CH), lambda i: (0, 0)),
            pl.BlockSpec((1, CONV_CH), lambda i: (0, 0)),
        ],
        out_specs=pl.BlockSpec((CONV_T, CONV_CH), lambda i: (i, 0)),
        out_shape=jax.ShapeDtypeStruct((N_PROMPT, CONV_CH), BF16),
        scratch_shapes=[pltpu.VMEM((CONV_T + CONV_HALO, CONV_CH), F32), pltpu.VMEM((CONV_T, CONV_CH), F32),
                        pltpu.VMEM((8, CONV_T + 8, CONV_CC), F32)],
        compiler_params=_cparams("parallel"),
        name="conv_prompt",
    )(a, a, wp, row(conv_b), row(ln_g), row(ln_b))


def _conv_sample_kernel(xt_ref, w_ref, b_ref, g_ref, beta_ref, o_ref):
    for t in range(DEC_SEQ):
        acc = jnp.zeros((DEC_BATCH, CONV_CH), F32)
        for k in range(CONV_WIDTH):
            acc = acc + w_ref[k:k + 1, :] * xt_ref[t + k]
        y = acc + b_ref[...]
        o_ref[t] = _ln_silu(y, g_ref[...], beta_ref[...]).astype(BF16)


def _conv_sample(xt, conv_w, conv_b, ln_g, ln_b):
    wp = jnp.pad(conv_w, ((0, 1), (0, 0)))
    row = lambda v: v.reshape(1, CONV_CH)
    npos = CONV_WIDTH - 1 + DEC_SEQ
    return pl.pallas_call(
        _conv_sample_kernel,
        grid=(1,),
        in_specs=[
            pl.BlockSpec((npos, DEC_BATCH, CONV_CH), lambda i: (0, 0, 0)),
            pl.BlockSpec((CONV_WIDTH + 1, CONV_CH), lambda i: (0, 0)),
            pl.BlockSpec((1, CONV_CH), lambda i: (0, 0)),
            pl.BlockSpec((1, CONV_CH), lambda i: (0, 0)),
            pl.BlockSpec((1, CONV_CH), lambda i: (0, 0)),
        ],
        out_specs=pl.BlockSpec((DEC_SEQ, DEC_BATCH, CONV_CH), lambda i: (0, 0, 0)),
        out_shape=jax.ShapeDtypeStruct((DEC_SEQ, DEC_BATCH, CONV_CH), BF16),
        compiler_params=_cparams("arbitrary"),
        name="conv_sample",
    )(xt, wp, row(conv_b), row(ln_g), row(ln_b))


HALF = HEAD_DIM
NEG_BIG = -1e30
CHUNKS_PER_PAGE = PAGE_SIZE // COMP_STRIDE
CHUNK_LANES = COMP_STRIDE * 2 * KV_WIDTH


def _alibi_slope(h):
    return float(2.0 ** (-8.0 * (h + 1) / N_HEADS) * LOG2E)


def _gelu_tanh(x):
    c = np.float32(np.sqrt(2.0 / np.pi))
    return 0.5 * x * (1.0 + jnp.tanh(c * (x + 0.044715 * (x * x * x))))


def _dot_nt(a, b, precision=None):
    return lax.dot_general(a, b, (((1,), (1,)), ((), ())), preferred_element_type=F32, precision=precision)


def _half_mask(shape, half):
    lane = lax.broadcasted_iota(jnp.int32, shape, len(shape) - 1)
    return (lane >= HALF) if half else (lane < HALF)


def _swap_halves(x):
    return pltpu.roll(x, HALF, axis=x.ndim - 1)


def _compress_weights(pos, w1, w2):
    eye2 = jnp.eye(2, dtype=F32)
    base = w1.reshape(2, COMP_STRIDE, HEAD_DIM, CMP_HIDDEN)
    w1p = jnp.einsum('rsde,hg->rshdge', base, eye2).reshape(2, COMP_STRIDE * 2 * HEAD_DIM, 2 * CMP_HIDDEN)
    pe = jnp.broadcast_to(pos.reshape(2, COMP_STRIDE, 1, HEAD_DIM), (2, COMP_STRIDE, 2, HEAD_DIM))
    pe = pe.reshape(2, 1, COMP_STRIDE * 2 * HEAD_DIM)
    w2p = jnp.einsum('ed,hg->hegd', w2, eye2).reshape(2 * CMP_HIDDEN, 2 * HEAD_DIM)
    return w1p.astype(BF16), pe, w2p.astype(BF16)


def _compress_pair(x, pe_ref, w1_ref, w2_ref):
    n = x.shape[0]
    a0 = jnp.dot((x + pe_ref[0]).astype(BF16), w1_ref[0], preferred_element_type=F32)
    a1 = jnp.dot((x + pe_ref[1]).astype(BF16), w1_ref[1], preferred_element_type=F32)
    hid = a0 + pltpu.roll(a1, n - 1, axis=0)
    return jnp.dot(_gelu_tanh(hid).astype(BF16), w2_ref[...], preferred_element_type=F32)


def _overlap_map(nc, ns, nc_pad, ns_pad):
    i = np.arange(nc_pad)[None, :]
    j = np.arange(ns_pad)[:, None]
    c_start, s_start = i * COMP_STRIDE, j * SEL_BLOCK
    m = (c_start <= s_start + SEL_BLOCK - 1) & (c_start + COMP_BLOCK - 1 >= s_start) & (i < nc) & (j < ns)
    return jnp.asarray(m.astype(np.float32))


def _select_blocks(imp_t, q_pos, n_iter):
    nb = imp_t.shape[0]
    blk = lax.broadcasted_iota(jnp.int32, imp_t.shape, 0)
    blk_f = blk.astype(F32)
    cur = q_pos >> 6
    valid = blk * SEL_BLOCK <= q_pos
    forced = (blk == 0) | (blk == cur) | (blk == cur - 1)
    score = jnp.where(valid, jnp.where(forced, 3e38, imp_t), -1.0)
    sel = jnp.zeros(imp_t.shape, F32)
    for _ in range(n_iter):
        mx = jnp.max(score, axis=0, keepdims=True)
        first = jnp.min(jnp.where(score == mx, blk_f, float(nb)), axis=0, keepdims=True)
        pick = (blk_f == first) & (mx >= 0.0)
        sel = jnp.where(pick, 1.0, sel)
        score = jnp.where(pick, -2.0, score)
    return sel


def _select_blocks_by_rank(imp_t, q_pos, n_sel, n_blocks):
    nb = imp_t.shape[0]
    rows = -(-n_blocks // 8) * 8
    blk = lax.broadcasted_iota(jnp.int32, (rows, imp_t.shape[1]), 0)
    cur = q_pos >> 6
    valid = blk * SEL_BLOCK <= q_pos
    forced = (blk == 0) | (blk == cur) | (blk == cur - 1)
    score = jnp.where(valid, jnp.where(forced, 3e38, imp_t[:rows]), -1.0)
    beaten_by = jnp.zeros(score.shape, F32)
    for i in range(n_blocks):
        s_i = score[i:i + 1, :]
        beats = (s_i > score) | ((s_i == score) & (blk > i))
        beaten_by = beaten_by + jnp.where(beats, 1.0, 0.0)
    sel = jnp.where((beaten_by < n_sel) & (score >= 0.0), 1.0, 0.0)
    return jnp.concatenate([sel, jnp.zeros((nb - rows, imp_t.shape[1]), F32)], axis=0)


P_CHUNKS = SEQ // COMP_STRIDE
P_NC = (SEQ - COMP_BLOCK) // COMP_STRIDE + 1
P_NS = SEQ // SEL_BLOCK


def _compress_prompt_kernel(*refs):
    c_refs = refs[:COMP_STRIDE]
    pe_ref, w1_ref, w2_ref, o_ref = refs[COMP_STRIDE:]
    x = jnp.concatenate([r[...] for r in c_refs], axis=1)
    o_ref[...] = _compress_pair(x, pe_ref, w1_ref, w2_ref).astype(BF16)


def _compress_prompt(kvc, cw):
    pos_k, w1_k, w2_k, pos_v, w1_v, w2_v = cw
    wk, wv = _compress_weights(pos_k, w1_k, w2_k), _compress_weights(pos_v, w1_v, w2_v)
    w1p, pe, w2p = (jnp.stack([a, b]) for a, b in zip(wk, wv))
    chunks = kvc.reshape(N_TOK // COMP_STRIDE, CHUNK_LANES)
    c_specs = [pl.BlockSpec((P_CHUNKS, LANES), lambda g, s=s: (0, s * 4 + g)) for s in range(COMP_STRIDE)]
    return pl.pallas_call(
        _compress_prompt_kernel,
        grid=(4,),
        in_specs=c_specs + [
            pl.BlockSpec((None, 2, 1, 2048), lambda g: (g // 2, 0, 0, 0)),
            pl.BlockSpec((None, 2, 2048, 256), lambda g: (g // 2, 0, 0, 0)),
            pl.BlockSpec((None, 256, LANES), lambda g: (g // 2, 0, 0)),
        ],
        out_specs=pl.BlockSpec((P_CHUNKS, LANES), lambda g: (0, g)),
        out_shape=jax.ShapeDtypeStruct((P_CHUNKS, 4 * LANES), BF16),
        compiler_params=_cparams("parallel"),
        name="compress_prompt",
    )(*([chunks] * COMP_STRIDE), pe, w1p, w2p)


QB = 256


def _stack_gqa(q_ref, k):
    qa = q_ref[:, 256 * k:256 * k + LANES]
    qb = q_ref[:, 256 * k + LANES:256 * (k + 1)]
    qas = _swap_halves(qa.astype(F32)).astype(BF16)
    qbs = _swap_halves(qb.astype(F32)).astype(BF16)
    order = [qa, qas, qb, qbs] if k % 2 == 0 else [qas, qa, qbs, qb]
    return jnp.concatenate(order, axis=0)


def _pair_up(o0, o1, half):
    return o0 + _swap_halves(o1) if half == 0 else _swap_halves(o0) + o1


def _cmp_prompt_kernel(q_ref, kcv_ref, smap_ref, oc_ref, drop_ref):
    i = pl.program_id(0)
    q0 = i * QB
    q_pos_col = q0 + lax.broadcasted_iota(jnp.int32, (QB, 1), 0)
    c_end = lax.broadcasted_iota(jnp.int32, (1, P_CHUNKS), 1) * COMP_STRIDE + (COMP_BLOCK - 1)
    d_c = q_pos_col - c_end
    ok = d_c >= 0
    d_cf = d_c.astype(F32)
    q_pos_row = q0 + lax.broadcasted_iota(jnp.int32, (1, QB), 1)
    for k in range(N_KV):
        pair, half = k // 2, k % 2
        keep = _half_mask((P_CHUNKS, LANES), half)
        kc = jnp.where(keep, kcv_ref[:, LANES * pair:LANES * (pair + 1)], 0)
        vc = jnp.where(keep, kcv_ref[:, 2 * LANES + LANES * pair:2 * LANES + LANES * (pair + 1)], 0)
        s4 = _dot_nt(_stack_gqa(q_ref, k), kc)
        psum = jnp.zeros((QB, P_CHUNKS), F32)
        outs = []
        for g in range(GQA):
            s = s4[g * QB:(g + 1) * QB] - _alibi_slope(4 * k + g) * d_cf
            s = jnp.where(ok, s, NEG_BIG)
            m = jnp.max(s, axis=-1, keepdims=True)
            e = jnp.where(ok, jnp.exp2(s - m), 0.0)
            den = jnp.sum(e, axis=-1, keepdims=True)
            p = e / jnp.where(den > 0, den, 1.0)
            psum = psum + p
            outs.append(jnp.dot(p.astype(BF16), vc, preferred_element_type=F32))
        oc_ref[:, 256 * k:256 * k + LANES] = _pair_up(outs[0], outs[1], half)
        oc_ref[:, 256 * k + LANES:256 * (k + 1)] = _pair_up(outs[2], outs[3], half)
        imp_t = _dot_nt(smap_ref[...], psum, precision=lax.Precision.HIGHEST)
        sel_t = _select_blocks(imp_t, q_pos_row, N_SEL)
        drop_ref[:, LANES * k:LANES * (k + 1)] = (1.0 - sel_t.T).astype(BF16)


def _cmp_prompt(q, kcv):
    smap_t = _overlap_map(P_NC, P_NS, P_CHUNKS, P_NS)
    return pl.pallas_call(
        _cmp_prompt_kernel,
        grid=(N_PROMPT // QB,),
        in_specs=[
            pl.BlockSpec((QB, Q_WIDTH), lambda i: (i, 0)),
            pl.BlockSpec((P_CHUNKS, 4 * LANES), lambda i: (0, 0)),
            pl.BlockSpec((P_NS, P_CHUNKS), lambda i: (0, 0)),
        ],
        out_specs=[
            pl.BlockSpec((QB, Q_WIDTH), lambda i: (i, 0)),
            pl.BlockSpec((QB, N_KV * P_NS), lambda i: (i, 0)),
        ],
        out_shape=[
            jax.ShapeDtypeStruct((N_PROMPT, Q_WIDTH), F32),
            jax.ShapeDtypeStruct((N_PROMPT, N_KV * P_NS), BF16),
        ],
        compiler_params=_cparams("parallel"),
        name="nsa_cmp_prompt",
    )(q, kcv, smap_t)


TKT = 512
POS_PIECES = 3


def _bf16_pieces(x, n):
    out = []
    for _ in range(n):
        p = float(np.float32(x).astype(jnp.bfloat16))
        out.append(p)
        x = x - p
    return out


def _position_lanes():
    pos = np.arange(SEQ)
    out = np.zeros((2, SEQ, LANES), np.float32)
    for half in range(2):
        base = HALF * (1 - half)
        for j in range(POS_PIECES):
            out[half, :, base + j] = (pos >> 6) * SEL_BLOCK
            out[half, :, base + POS_PIECES + j] = pos & (SEL_BLOCK - 1)
    return jnp.asarray(out, BF16)


def _block_drop_rows():
    m = np.where(np.arange(P_NS)[None, :] == (np.arange(SEQ)[:, None] // SEL_BLOCK), NEG_BIG, 0.0)
    return jnp.asarray(m.astype(np.float32), BF16)


def _selw_kernel(q_ref, g_ref, oc_ref, drop_ref, ksk_ref, vst_ref, kwk_ref, vwt_ref, pos_ref, e_ref, o_ref,
                 qa_ref, m_ref, acc_ref, os_ref):
    i = pl.program_id(0)
    q0 = pl.multiple_of(i * QB, QB)
    lane1 = lax.broadcasted_iota(jnp.int32, (1, LANES), 1)

    for k in range(N_KV):
        half = k % 2
        base = HALF * (1 - half)
        qa = q_ref[:, 256 * k:256 * k + LANES]
        qb = q_ref[:, 256 * k + LANES:256 * (k + 1)]
        qas = _swap_halves(qa.astype(F32)).astype(BF16)
        qbs = _swap_halves(qb.astype(F32)).astype(BF16)
        order = [qa, qas, qb, qbs] if half == 0 else [qas, qa, qbs, qb]
        own = _half_mask((QB, LANES), half)
        parts = []
        for g in range(GQA):
            sv = jnp.zeros((1, LANES), F32)
            pieces = _bf16_pieces(_alibi_slope(4 * k + g), POS_PIECES)
            for j, piece in enumerate(pieces + pieces):
                sv = jnp.where(lane1 == base + j, piece, sv)
            parts.append(jnp.where(own, order[g], sv.astype(BF16)))
        drop = drop_ref[:, LANES * k:LANES * (k + 1)]
        qa_ref[k] = jnp.concatenate([jnp.concatenate(parts, axis=0), jnp.concatenate([drop] * GQA, axis=0)], axis=1)

    key_row = lax.broadcasted_iota(jnp.int32, (TKT, 1), 0)
    q_lane = lax.broadcasted_iota(jnp.int32, (1, GQA * QB), 1) & (QB - 1)
    vrow = lax.broadcasted_iota(jnp.int32, (LANES, 1), 0)

    def tile(kk_ref, vt_ref, row0, selected, mode, first):
        if mode is not None:
            d = (q0 + q_lane) - (row0 + key_row)
            ok = (d >= 0) if mode == "causal" else ((d >= 0) & (d <= WINDOW))
        for k in range(N_KV):
            pair, half = k // 2, k % 2
            own = _half_mask((TKT, LANES), half)
            ka = jnp.where(own, kk_ref[pl.ds(row0, TKT), LANES * pair:LANES * (pair + 1)],
                           pos_ref[half, pl.ds(row0, TKT), :])
            if selected:
                st = _dot_nt(jnp.concatenate([ka, e_ref[pl.ds(row0, TKT), :]], axis=1), qa_ref[k])
            else:
                st = _dot_nt(ka, qa_ref[k, :, 0:LANES])
            if mode is not None:
                st = jnp.where(ok, st, NEG_BIG)
            m_tile = jnp.max(st, axis=0, keepdims=True)
            own_rows = (vrow >= HALF) if half else (vrow < HALF)
            ones_row = jnp.where(vrow == HALF * (1 - half), 1.0, 0.0).astype(BF16)
            va = jnp.where(own_rows, vt_ref[LANES * pair:LANES * (pair + 1), pl.ds(row0, TKT)], ones_row)
            if first:
                m_new = m_tile
            else:
                m_old = m_ref[k]
                m_new = jnp.maximum(m_old, m_tile)
            p = jnp.exp2(st - m_new).astype(BF16)
            pv = jnp.dot(va, p, preferred_element_type=F32)
            if first:
                acc_ref[k] = pv
            else:
                acc_ref[k] = jnp.exp2(m_old - m_new) * acc_ref[k] + pv
            m_ref[k] = m_new

    def normalised(k):
        half = k % 2
        den_row = HALF * (1 - half)
        return acc_ref[k] / acc_ref[k, den_row:den_row + 1, :]

    n_below = lax.div(i, jnp.int32(TKT // QB))
    diag0 = pl.multiple_of(n_below * TKT, TKT)
    tile(ksk_ref, vst_ref, diag0, True, "causal", True)

    def body(t, carry):
        tile(ksk_ref, vst_ref, pl.multiple_of(t * TKT, TKT), True, None, False)
        return carry

    lax.fori_loop(0, n_below, body, 0)
    for k in range(N_KV):
        os_ref[k] = normalised(k)

    tile(kwk_ref, vwt_ref, diag0, False, "window", True)

    @pl.when(n_below >= 1)
    def _():
        tile(kwk_ref, vwt_ref, pl.multiple_of(diag0 - TKT, TKT), False, "window", False)

    g_t = g_ref[...].T
    low = _half_mask((QB, LANES), 0)
    for k in range(N_KV):
        half = k % 2
        o_w = normalised(k)
        contrib = []
        for g in range(GQA):
            h = 4 * k + g
            cols = slice(g * QB, (g + 1) * QB)
            c_t = (os_ref[k, :, cols] * g_t[N_HEADS + h:N_HEADS + h + 1, :]
                   + o_w[:, cols] * g_t[2 * N_HEADS + h:2 * N_HEADS + h + 1, :])
            contrib.append(c_t.T)
        for pr in range(2):
            h0 = 4 * k + 2 * pr
            c0, c1 = contrib[2 * pr], contrib[2 * pr + 1]
            both = jnp.where(low, c0, _swap_halves(c1)) if half == 0 else jnp.where(low, _swap_halves(c0), c1)
            gate_c = jnp.where(low, g_ref[:, h0:h0 + 1], g_ref[:, h0 + 1:h0 + 2])
            cols = slice(256 * k + LANES * pr, 256 * k + LANES * (pr + 1))
            o_ref[:, cols] = (gate_c * oc_ref[:, cols] + both).astype(BF16)


def _selw_prompt(q, gates, o_c, drop, ks_k, vs_t, kw_k, vw_t):
    resident = lambda shape: pl.BlockSpec(shape, lambda i: (0,) * len(shape), pipeline_mode=pl.Buffered(1))
    return pl.pallas_call(
        _selw_kernel,
        grid=(N_PROMPT // QB,),
        in_specs=[
            pl.BlockSpec((QB, Q_WIDTH), lambda i: (i, 0)),
            pl.BlockSpec((QB, LANES), lambda i: (i, 0)),
            pl.BlockSpec((QB, Q_WIDTH), lambda i: (i, 0)),
            pl.BlockSpec((QB, N_KV * P_NS), lambda i: (i, 0)),
            resident((N_PROMPT, KV_WIDTH)),
            resident((KV_WIDTH, N_PROMPT)),
            resident((N_PROMPT, KV_WIDTH)),
            resident((KV_WIDTH, N_PROMPT)),
            resident((2, SEQ, LANES)),
            resident((SEQ, LANES)),
        ],
        out_specs=pl.BlockSpec((QB, Q_WIDTH), lambda i: (i, 0)),
        out_shape=jax.ShapeDtypeStruct((N_PROMPT, Q_WIDTH), BF16),
        scratch_shapes=[
            pltpu.VMEM((N_KV, GQA * QB, 2 * LANES), BF16),
            pltpu.VMEM((N_KV, 1, GQA * QB), F32),
            pltpu.VMEM((N_KV, LANES, GQA * QB), F32),
            pltpu.VMEM((N_KV, LANES, GQA * QB), F32),
        ],
        compiler_params=_cparams("parallel"),
        name="nsa_selw_prompt",
    )(q, gates, o_c, drop, ks_k, vs_t, kw_k, vw_t, _position_lanes(), _block_drop_rows())


N_PAGES = PAST_LEN // PAGE_SIZE
S_CHUNKS = PAST_LEN // COMP_STRIDE
S_NC = (PAST_LEN + DEC_SEQ - COMP_BLOCK) // COMP_STRIDE + 1
S_NS = -(-(PAST_LEN + DEC_SEQ) // SEL_BLOCK)
S_KEYS = 17 * LANES
WIN_BUF = min(WINDOW, PAST_LEN)
S_WKEYS = 5 * LANES
ROWS16 = GQA * DEC_SEQ
HEADS8 = 2 * N_KV


def _row_softmax(s, ok):
    s = jnp.where(ok, s, NEG_BIG)
    m = jnp.max(s, axis=-1, keepdims=True)
    e = jnp.where(ok, jnp.exp2(s - m), 0.0)
    den = jnp.sum(e, axis=-1, keepdims=True)
    return e / jnp.where(den > 0, den, 1.0)


def _nsa_sample_kernel(pt_ref, *refs):
    cp = refs[:N_PAGES]
    sp = refs[N_PAGES:2 * N_PAGES]
    (win_ref, news_ref, neww_ref, q_ref, g_ref, pe_ref, w1_ref, w2_ref, smap_ref, expand_ref,
     o_ref, ks_ref, kw_ref) = refs[2 * N_PAGES:]
    del pt_ref

    for c in range(HEADS8):
        kv, h = c // N_KV, c % N_KV
        for p in range(N_PAGES):
            ks_ref[c, :, PAGE_SIZE * p:PAGE_SIZE * (p + 1)] = sp[p][kv, h].astype(BF16)
        ks_ref[c, :, PAST_LEN:] = news_ref[c]
        kw_ref[c, :, 0:WIN_BUF] = win_ref[kv, h].astype(BF16)
        kw_ref[c, :, WIN_BUF:] = neww_ref[c]

    kcv = []
    for kv in range(2):
        x = jnp.concatenate([cp[p][N_KV * kv + h] for h in range(N_KV) for p in range(N_PAGES)], axis=0)
        a0 = jnp.dot((x + pe_ref[kv, 0]).astype(BF16), w1_ref[kv, 0], preferred_element_type=F32)
        a1 = jnp.dot((x + pe_ref[kv, 1]).astype(BF16), w1_ref[kv, 1], preferred_element_type=F32)
        hid = a0 + pltpu.roll(a1, N_KV * S_CHUNKS - 1, axis=0)
        out = jnp.dot(_gelu_tanh(hid).astype(BF16), w2_ref[kv], preferred_element_type=F32).astype(BF16)
        kcv += [out[S_CHUNKS * h:S_CHUNKS * (h + 1)] for h in range(N_KV)]

    row = lax.broadcasted_iota(jnp.int32, (ROWS16, 1), 0)
    q_pos = PAST_LEN + (row & (DEC_SEQ - 1))
    grp = row >> 2

    def slopes(k):
        s = jnp.full((ROWS16, 1), _alibi_slope(4 * k), F32)
        for g in range(1, GQA):
            s = jnp.where(grp == g, _alibi_slope(4 * k + g), s)
        return s

    c_end = lax.broadcasted_iota(jnp.int32, (1, S_CHUNKS), 1) * COMP_STRIDE + (COMP_BLOCK - 1)
    d_c = q_pos - c_end
    o_cmp, psums = [], []
    for k in range(N_KV):
        s = _dot_nt(q_ref[k], kcv[k]) - slopes(k) * d_c.astype(F32)
        p = _row_softmax(s, d_c >= 0)
        o_cmp.append(jnp.dot(p.astype(BF16), kcv[N_KV + k], preferred_element_type=F32))
        ps = p
        for g in range(1, GQA):
            ps = ps + pltpu.roll(p, DEC_SEQ * g, axis=0)
        psums.append(ps)
    psum_all = jnp.concatenate(psums + [jnp.zeros((LANES - N_KV * ROWS16, S_CHUNKS), F32)], axis=0)
    imp_t = _dot_nt(smap_ref[...], psum_all, precision=lax.Precision.HIGHEST)
    col = lax.broadcasted_iota(jnp.int32, (1, LANES), 1)
    sel_t = _select_blocks_by_rank(imp_t, PAST_LEN + (col & (DEC_SEQ - 1)), N_SEL, S_NS)
    sel = sel_t.T.astype(BF16)

    key_s = lax.broadcasted_iota(jnp.int32, (1, S_KEYS), 1)
    d_s = q_pos - key_s
    idx_w = lax.broadcasted_iota(jnp.int32, (1, S_WKEYS), 1)
    d_w = q_pos - (PAST_LEN - WIN_BUF + idx_w)
    ok_w = (d_w >= 0) & (d_w <= WINDOW)
    for k in range(N_KV):
        chosen = jnp.dot(sel[ROWS16 * k:ROWS16 * (k + 1)], expand_ref[...], preferred_element_type=F32)
        s = jnp.dot(q_ref[k], ks_ref[k], preferred_element_type=F32) - slopes(k) * d_s.astype(F32)
        p = _row_softmax(s, (chosen > 0.5) & (d_s >= 0))
        o_sel = _dot_nt(p.astype(BF16), ks_ref[N_KV + k])
        s = jnp.dot(q_ref[k], kw_ref[k], preferred_element_type=F32) - slopes(k) * d_w.astype(F32)
        p = _row_softmax(s, ok_w)
        o_win = _dot_nt(p.astype(BF16), kw_ref[N_KV + k])
        o_ref[k] = (g_ref[k][:, 0:1] * o_cmp[k] + g_ref[k][:, 1:2] * o_sel + g_ref[k][:, 2:3] * o_win)


def _nsa_sample(page_table, cache_c, cache_s, win_prev, q_s, gates_s, kvs_new, kvw_new, cw):
    pos_k, w1_k, w2_k, pos_v, w1_v, w2_v = cw
    flat = COMP_STRIDE * HEAD_DIM
    n_phys = cache_c.shape[1]
    w1 = jnp.stack([w1_k, w1_v]).reshape(2, 2, flat, CMP_HIDDEN).astype(BF16)
    pe = jnp.stack([pos_k, pos_v]).reshape(2, 2, 1, flat)
    w2 = jnp.stack([w2_k, w2_v]).astype(BF16)
    chunks = cache_c.reshape(n_phys, CHUNKS_PER_PAGE, COMP_STRIDE, HEADS8, HEAD_DIM)
    chunks = chunks.transpose(0, 3, 1, 2, 4).reshape(n_phys, HEADS8, CHUNKS_PER_PAGE, flat)
    slc_t = cache_s.transpose(0, 1, 3, 4, 5, 2)
    win_t = win_prev.transpose(0, 1, 3, 4, 5, 2)
    qh = q_s.reshape(DEC_SEQ, DEC_BATCH, N_KV, GQA, HEAD_DIM).transpose(1, 2, 3, 0, 4)
    q16 = qh.reshape(DEC_BATCH, N_KV, ROWS16, HEAD_DIM)
    gh = gates_s[:, :3 * N_HEADS].reshape(DEC_SEQ, DEC_BATCH, 3, N_KV, GQA).transpose(1, 3, 4, 0, 2)
    g16 = jnp.pad(gh.reshape(DEC_BATCH, N_KV, ROWS16, 3), ((0, 0), (0, 0), (0, 0), (0, LANES - 3)))

    def new_rows_t(a, width):
        a = a.reshape(DEC_SEQ, DEC_BATCH, HEADS8, HEAD_DIM).transpose(1, 2, 3, 0).astype(BF16)
        return jnp.pad(a, ((0, 0), (0, 0), (0, 0), (0, width - DEC_SEQ)))

    smap_t = _overlap_map(S_NC, S_NS, S_CHUNKS, LANES)
    expand = jnp.asarray((np.arange(LANES)[:, None] == (np.arange(S_KEYS)[None, :] // SEL_BLOCK)), BF16)
    pt = page_table.reshape(-1).astype(jnp.int32)
    c_specs = [pl.BlockSpec((None, HEADS8, CHUNKS_PER_PAGE, flat), lambda b, pt, p=p: (pt[b * N_PAGES + p], 0, 0, 0))
               for p in range(N_PAGES)]
    s_specs = [pl.BlockSpec((None, None, 2, N_KV, HEAD_DIM, PAGE_SIZE),
                            lambda b, pt, p=p: (0, pt[b * N_PAGES + p], 0, 0, 0, 0)) for p in range(N_PAGES)]
    per_b4 = lambda shape: pl.BlockSpec((None,) + shape, lambda b, pt: (b, 0, 0, 0))
    const = lambda shape: pl.BlockSpec(shape, lambda b, pt: (0,) * len(shape))
    o16 = pl.pallas_call(
        _nsa_sample_kernel,
        grid_spec=pltpu.PrefetchScalarGridSpec(
            num_scalar_prefetch=1,
            grid=(DEC_BATCH,),
            in_specs=c_specs + s_specs + [
                pl.BlockSpec((None, None, 2, N_KV, HEAD_DIM, WIN_BUF), lambda b, pt: (0, b, 0, 0, 0, 0)),
                per_b4((HEADS8, HEAD_DIM, S_KEYS - PAST_LEN)),
                per_b4((HEADS8, HEAD_DIM, S_WKEYS - WIN_BUF)),
                per_b4((N_KV, ROWS16, HEAD_DIM)),
                per_b4((N_KV, ROWS16, LANES)),
                const((2, 2, 1, flat)),
                const((2, 2, flat, CMP_HIDDEN)),
                const((2, CMP_HIDDEN, HEAD_DIM)),
                const((LANES, S_CHUNKS)),
                const((LANES, S_KEYS)),
            ],
            out_specs=per_b4((N_KV, ROWS16, HEAD_DIM)),
            scratch_shapes=[pltpu.VMEM((HEADS8, HEAD_DIM, S_KEYS), BF16), pltpu.VMEM((HEADS8, HEAD_DIM, S_WKEYS), BF16)],
        ),
        out_shape=jax.ShapeDtypeStruct((DEC_BATCH, N_KV, ROWS16, HEAD_DIM), F32),
        compiler_params=_cparams("parallel"),
        name="nsa_sample",
    )(pt, *([chunks] * N_PAGES), *([slc_t] * N_PAGES), win_t,
      new_rows_t(kvs_new, S_KEYS - PAST_LEN), new_rows_t(kvw_new, S_WKEYS - WIN_BUF),
      q16, g16, pe, w1, w2, smap_t, expand)
    o = o16.reshape(DEC_BATCH, N_KV, GQA, DEC_SEQ, HEAD_DIM)
    return o.transpose(3, 0, 1, 2, 4).reshape(N_SAMPLE, Q_WIDTH).astype(BF16)


OUT_TM = 512
ROUTE_E0, ROUTE_E1, ROUTE_W0, ROUTE_W1, ROUTE_R0, ROUTE_R1 = range(6)


def _layer_norm(x, g, b):
    mu = jnp.mean(x, axis=-1, keepdims=True)
    xc = x - mu
    var = jnp.mean(xc * xc, axis=-1, keepdims=True)
    return xc * lax.rsqrt(var + LN_EPS) * g + b


def _first_max(vals, lane_f):
    mx = jnp.max(vals, axis=-1, keepdims=True)
    idx = jnp.min(jnp.where(vals == mx, lane_f, float(LANES)), axis=-1, keepdims=True)
    return mx, idx


def _out_route_kernel(conv_ref, attn_ref, x_ref, wo_ref, g1_ref, b1_ref, wr_ref, br_ref,
                      h_ref, route_ref, counts_ref, run_ref):
    i = pl.program_id(0)

    @pl.when(i == 0)
    def _():
        run_ref[...] = jnp.zeros_like(run_ref)

    mix = jnp.dot(conv_ref[...], wo_ref[0:CONV_CH, :], preferred_element_type=F32)
    mix = mix + jnp.dot(attn_ref[...], wo_ref[CONV_CH:, :], preferred_element_type=F32)
    h = _layer_norm(ALPHA * x_ref[...] + mix, g1_ref[...], b1_ref[...])
    h_ref[...] = h

    logit = jnp.dot(h, wr_ref[...], preferred_element_type=F32, precision=lax.Precision.HIGHEST) + br_ref[...]
    lane = lax.broadcasted_iota(jnp.int32, logit.shape, 1)
    lane_f = lane.astype(F32)
    is_grp = (lane >= N_EXPERTS) & (lane < N_EXPERTS + N_GROUPS)
    g_logit = jnp.where(is_grp, logit, NEG_BIG)
    g_top, g_lane = _first_max(g_logit, lane_f)
    lse = g_top + jnp.log(jnp.sum(jnp.where(is_grp, jnp.exp(g_logit - g_top), 0.0), axis=-1, keepdims=True))
    g_p = jnp.exp(g_top - lse)
    g_idx = g_lane.astype(jnp.int32) - N_EXPERTS
    in_grp = (lane >> 3) == g_idx
    e_logit = jnp.where(in_grp, logit, NEG_BIG)
    e0, i0 = _first_max(e_logit, lane_f)
    e1, i1 = _first_max(jnp.where(lane_f == i0, NEG_BIG, e_logit), lane_f)
    z1 = jnp.exp(e1 - e0)
    w0 = g_p / (1.0 + z1)
    w1 = g_p * z1 / (1.0 + z1)

    hit = jnp.where((lane_f == i0) | (lane_f == i1), 1.0, 0.0)
    r = lax.broadcasted_iota(jnp.int32, (OUT_TM, OUT_TM), 0)
    c = lax.broadcasted_iota(jnp.int32, (OUT_TM, OUT_TM), 1)
    lower = jnp.where(c < r, 1.0, 0.0).astype(BF16)
    before = jnp.dot(lower, hit.astype(BF16), preferred_element_type=F32) + run_ref[...]
    r0 = jnp.sum(jnp.where(lane_f == i0, before, 0.0), axis=-1, keepdims=True)
    r1 = jnp.sum(jnp.where(lane_f == i1, before, 0.0), axis=-1, keepdims=True)
    run_ref[...] = run_ref[...] + jnp.sum(hit, axis=0, keepdims=True)
    counts_ref[...] = jnp.broadcast_to(run_ref[...], counts_ref.shape)

    rec = jnp.zeros(logit.shape, F32)
    for col_id, v in ((ROUTE_E0, i0), (ROUTE_E1, i1), (ROUTE_W0, w0), (ROUTE_W1, w1), (ROUTE_R0, r0), (ROUTE_R1, r1)):
        rec = jnp.where(lane == col_id, v, rec)
    route_ref[...] = rec


def _out_route(conv, attn, x, w_out, ln_g, ln_b, w_grp, b_grp, w_exp, b_exp):
    n = x.shape[0]
    wr = jnp.pad(jnp.concatenate([w_exp, w_grp], axis=1), ((0, 0), (0, LANES - N_EXPERTS - N_GROUPS)))
    br = jnp.pad(jnp.concatenate([b_exp, b_grp]), (0, LANES - N_EXPERTS - N_GROUPS)).reshape(1, LANES)
    const = lambda shape: pl.BlockSpec(shape, lambda i: (0, 0))
    return pl.pallas_call(
        _out_route_kernel,
        grid=(n // OUT_TM,),
        in_specs=[
            pl.BlockSpec((OUT_TM, CONV_CH), lambda i: (i, 0)),
            pl.BlockSpec((OUT_TM, Q_WIDTH), lambda i: (i, 0)),
            pl.BlockSpec((OUT_TM, D_MODEL), lambda i: (i, 0)),
            pl.BlockSpec((D_MODEL, D_MODEL), lambda i: (0, 0), pipeline_mode=pl.Buffered(1)),
            const((1, D_MODEL)),
            const((1, D_MODEL)),
            const((D_MODEL, LANES)),
            const((1, LANES)),
        ],
        out_specs=[
            pl.BlockSpec((OUT_TM, D_MODEL), lambda i: (i, 0)),
            pl.BlockSpec((OUT_TM, LANES), lambda i: (i, 0)),
            const((8, LANES)),
        ],
        out_shape=[
            jax.ShapeDtypeStruct((n, D_MODEL), F32),
            jax.ShapeDtypeStruct((n, LANES), F32),
            jax.ShapeDtypeStruct((8, LANES), F32),
        ],
        scratch_shapes=[pltpu.VMEM((1, LANES), F32)],
        compiler_params=_cparams("arbitrary"),
        name="out_ln1_route",
    )(conv, attn, x, w_out.astype(BF16), ln_g.reshape(1, D_MODEL), ln_b.reshape(1, D_MODEL), wr, br)


MOE_BLK = 128
N_ASSIGN = N_TOK * TOP_K
MOE_BLOCKS = -(-(N_ASSIGN + N_EXPERTS * (MOE_BLK - 1)) // MOE_BLK)
MOE_ROWS = MOE_BLOCKS * MOE_BLK
CMB_TM = 128


def _gather_rows(src_hbm, dst_ref, sem, idx_ref, base, n_rows):
    def body(r, carry):
        pltpu.make_async_copy(src_hbm.at[pl.ds(idx_ref[base + r], 1)], dst_ref.at[pl.ds(r, 1)], sem).start()
        return carry
    lax.fori_loop(0, n_rows, body, 0)


def _wait_rows(src_hbm, dst_ref, sem):
    pltpu.make_async_copy(src_hbm.at[pl.ds(0, dst_ref.shape[0])], dst_ref, sem).wait()


def _wait_row_count(src_hbm, dst_ref, sem, n_rows):
    p = MOE_BLK
    while p >= 1:
        @pl.when((n_rows & p) != 0)
        def _(p=p):
            pltpu.make_async_copy(src_hbm.at[pl.ds(0, p)], dst_ref.at[pl.ds(0, p)], sem).wait()
        p //= 2


def _moe_kernel(blk_e_ref, row_tok_ref, blk_rows_ref, h_hbm, wg_ref, wu_ref, wd_ref, y_ref,
                xbuf, sems, wg_b, wu_b, wd_b):
    i = pl.program_id(0)
    slot = i % 2
    rows = blk_rows_ref[i]
    nxt = jnp.minimum(i + 1, MOE_BLOCKS - 1)
    rows_next = jnp.where(i + 1 < MOE_BLOCKS, blk_rows_ref[nxt], 0)

    @pl.when(i == 0)
    def _():
        xbuf[...] = jnp.zeros_like(xbuf)
        _gather_rows(h_hbm, xbuf.at[0], sems.at[0], row_tok_ref, 0, rows)

    @pl.when(rows_next > 0)
    def _():
        _gather_rows(h_hbm, xbuf.at[1 - slot], sems.at[1 - slot], row_tok_ref, (i + 1) * MOE_BLK, rows_next)

    @pl.when((i == 0) | (blk_e_ref[i] != blk_e_ref[jnp.maximum(i - 1, 0)]))
    def _():
        wg_b[...] = wg_ref[...].astype(BF16)
        wu_b[...] = wu_ref[...].astype(BF16)
        wd_b[...] = wd_ref[...].astype(BF16)

    _wait_row_count(h_hbm, xbuf.at[slot], sems.at[slot], rows)

    @pl.when(rows > 0)
    def _():
        x = xbuf[slot].astype(BF16)
        hg = jnp.dot(x, wg_b[...], preferred_element_type=F32)
        hu = jnp.dot(x, wu_b[...], preferred_element_type=F32)
        act = (hg * jax.nn.sigmoid(hg) * hu).astype(BF16)
        y_ref[...] = jnp.dot(act, wd_b[...], preferred_element_type=F32)

    @pl.when(rows == 0)
    def _():
        y_ref[...] = jnp.zeros_like(y_ref)


def _moe(h, blk_e, row_tok, blk_rows, w_gate, w_up, w_down):
    wspec = lambda shape: pl.BlockSpec((None,) + shape, lambda i, be, rt, br: (be[i], 0, 0))
    return pl.pallas_call(
        _moe_kernel,
        grid_spec=pltpu.PrefetchScalarGridSpec(
            num_scalar_prefetch=3,
            grid=(MOE_BLOCKS,),
            in_specs=[
                pl.BlockSpec(memory_space=pl.ANY),
                wspec((D_MODEL, MOE_D_FF)),
                wspec((D_MODEL, MOE_D_FF)),
                wspec((MOE_D_FF, D_MODEL)),
            ],
            out_specs=pl.BlockSpec((MOE_BLK, D_MODEL), lambda i, be, rt, br: (i, 0)),
            scratch_shapes=[
                pltpu.VMEM((2, MOE_BLK, D_MODEL), F32),
                pltpu.SemaphoreType.DMA((2,)),
                pltpu.VMEM((D_MODEL, MOE_D_FF), BF16),
                pltpu.VMEM((D_MODEL, MOE_D_FF), BF16),
                pltpu.VMEM((MOE_D_FF, D_MODEL), BF16),
            ],
        ),
        out_shape=jax.ShapeDtypeStruct((MOE_ROWS, D_MODEL), F32),
        compiler_params=_cparams("arbitrary"),
        name="moe_ffn",
    )(blk_e, row_tok, blk_rows, h, w_gate, w_up, w_down)


def _combine_kernel(dest_ref, y_hbm, h_ref, route_ref, g2_ref, b2_ref, o_ref, ybuf, sems):
    i = pl.program_id(0)
    n_steps = pl.num_programs(0)
    slot = i % 2

    @pl.when(i == 0)
    def _():
        _gather_rows(y_hbm, ybuf.at[0], sems.at[0], dest_ref, 0, TOP_K * CMB_TM)

    @pl.when(i + 1 < n_steps)
    def _():
        _gather_rows(y_hbm, ybuf.at[1 - slot], sems.at[1 - slot], dest_ref, (i + 1) * TOP_K * CMB_TM, TOP_K * CMB_TM)

    _wait_rows(y_hbm, ybuf.at[slot], sems.at[slot])
    w0 = route_ref[:, ROUTE_W0:ROUTE_W0 + 1]
    w1 = route_ref[:, ROUTE_W1:ROUTE_W1 + 1]
    moe = ybuf[slot, 0:CMB_TM] * w0 + ybuf[slot, CMB_TM:2 * CMB_TM] * w1
    o_ref[...] = _layer_norm(ALPHA * h_ref[...] + moe, g2_ref[...], b2_ref[...])


def _combine(dest, y_rows, h, route, ln_g, ln_b):
    n = h.shape[0]
    return pl.pallas_call(
        _combine_kernel,
        grid_spec=pltpu.PrefetchScalarGridSpec(
            num_scalar_prefetch=1,
            grid=(n // CMB_TM,),
            in_specs=[
                pl.BlockSpec(memory_space=pl.ANY),
                pl.BlockSpec((CMB_TM, D_MODEL), lambda i, d: (i, 0)),
                pl.BlockSpec((CMB_TM, LANES), lambda i, d: (i, 0)),
                pl.BlockSpec((1, D_MODEL), lambda i, d: (0, 0)),
                pl.BlockSpec((1, D_MODEL), lambda i, d: (0, 0)),
            ],
            out_specs=pl.BlockSpec((CMB_TM, D_MODEL), lambda i, d: (i, 0)),
            scratch_shapes=[pltpu.VMEM((2, TOP_K * CMB_TM, D_MODEL), F32), pltpu.SemaphoreType.DMA((2,))],
        ),
        out_shape=jax.ShapeDtypeStruct((n, D_MODEL), F32),
        compiler_params=_cparams("arbitrary"),
        name="moe_combine_ln2",
    )(dest, y_rows, h, route, ln_g.reshape(1, D_MODEL), ln_b.reshape(1, D_MODEL))


def _dispatch_indices(route, counts):
    eid = route[:, ROUTE_E0:ROUTE_E1 + 1].astype(jnp.int32)
    rank = route[:, ROUTE_R0:ROUTE_R1 + 1].astype(jnp.int32)
    cnt = counts[0, :N_EXPERTS].astype(jnp.int32)
    padded = (cnt + MOE_BLK - 1) // MOE_BLK * MOE_BLK
    pad_end = jnp.cumsum(padded)
    pad_start = pad_end - padded
    dest = pad_start[eid] + rank
    tok = jnp.broadcast_to(jnp.arange(N_TOK, dtype=jnp.int32)[:, None], dest.shape)
    row_tok = jnp.zeros((MOE_ROWS,), jnp.int32).at[dest.reshape(-1)].set(tok.reshape(-1))
    blk_start = jnp.arange(MOE_BLOCKS, dtype=jnp.int32) * MOE_BLK
    blk_e = jnp.minimum(jnp.sum(pad_end[None, :] <= blk_start[:, None], axis=1), N_EXPERTS - 1).astype(jnp.int32)
    blk_rows = jnp.clip(cnt[blk_e] - (blk_start - pad_start[blk_e]), 0, MOE_BLK).astype(jnp.int32)
    dest_tiles = dest.reshape(N_TOK // CMB_TM, CMB_TM, TOP_K).transpose(0, 2, 1).reshape(-1)
    return blk_e, row_tok, blk_rows, dest_tiles


def kernel(x_prompt, x_sample, cache_cmp_kv, cache_slc_kv, state_win_kv, state_conv, page_table, w_in, conv_w, conv_b, conv_ln_g, conv_ln_b, cmp_pos_k, cmp_w1_k, cmp_w2_k, cmp_pos_v, cmp_w1_v, cmp_w2_v, w_out, ln1_g, ln1_b, w_grp, b_grp, w_exp, b_exp, w_gate, w_up, w_down, ln2_g, ln2_b):
    xs_t = x_sample.transpose(1, 0, 2).reshape(N_SAMPLE, D_MODEL)
    x = jnp.concatenate([x_prompt.reshape(N_PROMPT, D_MODEL), xs_t], axis=0)
    glu, q, kvc, kvs, kvw, gates, ks_k, vs_t, kw_k, vw_t = _proj(x, w_in[0])
    conv_p = _conv_prompt(glu, conv_w[0], conv_b[0], conv_ln_g[0], conv_ln_b[0])
    a_s = glu[N_PROMPT:].reshape(DEC_SEQ, DEC_BATCH, CONV_CH)
    xt = jnp.concatenate([state_conv[0].transpose(1, 0, 2), a_s], axis=0)
    conv_s = _conv_sample(xt, conv_w[0], conv_b[0], conv_ln_g[0], conv_ln_b[0])
    cw = (cmp_pos_k[0], cmp_w1_k[0], cmp_w2_k[0], cmp_pos_v[0], cmp_w1_v[0], cmp_w2_v[0])
    kcv_p = _compress_prompt(kvc, cw)
    o_c, drop = _cmp_prompt(q, kcv_p)
    attn_p = _selw_prompt(q, gates, o_c, drop, ks_k, vs_t, kw_k, vw_t)
    attn_s = _nsa_sample(page_table, cache_cmp_kv, cache_slc_kv, state_win_kv, q[N_PROMPT:],
                         gates[N_PROMPT:], kvs[N_PROMPT:], kvw[N_PROMPT:], cw)
    conv = jnp.concatenate([conv_p, conv_s.reshape(N_SAMPLE, CONV_CH)], axis=0)
    attn = jnp.concatenate([attn_p, attn_s], axis=0)
    h1, route, counts = _out_route(conv, attn, x, w_out[0], ln1_g[0], ln1_b[0], w_grp[0], b_grp[0],
                                   w_exp[0], b_exp[0])
    blk_e, row_tok, blk_rows, dest_tiles = _dispatch_indices(route, counts)
    y_rows = _moe(h1, blk_e, row_tok, blk_rows, w_gate[0], w_up[0], w_down[0])
    y = _combine(dest_tiles, y_rows, h1, route, ln2_g[0], ln2_b[0])

    def batch_major(a):
        return a[N_PROMPT:].reshape(DEC_SEQ, DEC_BATCH, -1).transpose(1, 0, 2)

    kv_shape = (2, N_KV, HEAD_DIM)
    win_len = min(WINDOW, SEQ)
    y_prompt = y[:N_PROMPT].reshape(1, SEQ, D_MODEL)
    y_sample = batch_major(y)
    p_cmp = kvc[:N_PROMPT].reshape(1, 1, SEQ, *kv_shape)
    p_slc = kvs[:N_PROMPT].reshape(1, 1, SEQ, *kv_shape)
    p_win = kvw[N_PROMPT - win_len:N_PROMPT].reshape(1, 1, win_len, *kv_shape)
    p_conv = glu[N_PROMPT - (CONV_WIDTH - 1):N_PROMPT].reshape(1, 1, CONV_WIDTH - 1, CONV_CH)
    s_cmp = batch_major(kvc).reshape(1, DEC_BATCH, DEC_SEQ, *kv_shape)
    s_slc = batch_major(kvs).reshape(1, DEC_BATCH, DEC_SEQ, *kv_shape)
    s_win = jnp.concatenate([state_win_kv[0][:, DEC_SEQ:],
                             batch_major(kvw).reshape(DEC_BATCH, DEC_SEQ, *kv_shape)], axis=1)[None]
    s_conv = jnp.concatenate([state_conv[0][:, DEC_SEQ:], batch_major(glu)], axis=1)[None]
    return (y_prompt, y_sample, p_cmp, p_slc, p_win, p_conv, s_cmp, s_slc, s_win, s_conv)
```

```python
import numpy as np
import jax
import jax.numpy as jnp
from jax import lax
from jax.experimental import pallas as pl
from jax.experimental.pallas import tpu as pltpu

F32 = jnp.float32
BF16 = jnp.bfloat16

D_MODEL = 2048
SEQ = 8192
DEC_BATCH = 128
DEC_SEQ = 4
PAST_LEN = 2048
PAGE_SIZE = 128
CONV_CH = 1024
CONV_WIDTH = 31
N_HEADS = 16
HEAD_DIM = 64
N_KV = 4
GQA = 4
Q_WIDTH = 1024
KV_WIDTH = 256
COMP_BLOCK = 32
COMP_STRIDE = 16
CMP_HIDDEN = 128
SEL_BLOCK = 64
N_SEL = 16
WINDOW = 512
N_GROUPS = 8
EXPERTS_PER_GROUP = 8
N_EXPERTS = 64
TOP_K = 2
MOE_D_FF = 512
LN_EPS = 1e-5
ALPHA = 2.0 ** 0.25
LOG2E = 1.4426950408889634

N_PROMPT = SEQ
N_SAMPLE = DEC_BATCH * DEC_SEQ
N_TOK = N_PROMPT + N_SAMPLE

LANES = 128
VMEM_LIMIT = 48 * 1024 * 1024


def _cparams(*sem):
    return pltpu.CompilerParams(dimension_semantics=sem, vmem_limit_bytes=VMEM_LIMIT)


PROJ_TM = 512
PROJ_TN = 512
PROJ_VMEM_LIMIT = 56 * 1024 * 1024


def _proj_kernel(x_ref, w_ref, wg_ref, glu_ref, q_ref, kvc_ref, kvs_ref, kvw_ref, g_ref,
                 ksk_ref, vst_ref, kwk_ref, vwt_ref, xb_ref):
    xb_ref[...] = x_ref[...].astype(BF16)

    def cols(j):
        return jnp.dot(xb_ref[...], w_ref[:, PROJ_TN * j:PROJ_TN * (j + 1)], preferred_element_type=F32)

    for j in range(4):
        acc = cols(j)
        glu_ref[:, 256 * j:256 * (j + 1)] = acc[:, :256] * jax.nn.sigmoid(acc[:, 256:])
    for j in range(2):
        q_ref[:, PROJ_TN * j:PROJ_TN * (j + 1)] = (cols(4 + j) * (HEAD_DIM ** -0.5 * LOG2E)).astype(BF16)
    kvc_ref[...] = cols(6)
    acc = cols(7)
    kvs_ref[...] = acc
    ksk_ref[...] = acc[:, :KV_WIDTH].astype(BF16)
    vst_ref[...] = acc[:, KV_WIDTH:].T.astype(BF16)
    acc = cols(8)
    kvw_ref[...] = acc
    kwk_ref[...] = acc[:, :KV_WIDTH].astype(BF16)
    vwt_ref[...] = acc[:, KV_WIDTH:].T.astype(BF16)
    g_ref[...] = jax.nn.sigmoid(jnp.dot(xb_ref[...], wg_ref[...], preferred_element_type=F32))


def _proj(x, w_in):
    n = x.shape[0]
    o1 = 2 * CONV_CH
    o2 = o1 + Q_WIDTH
    o3 = o2 + 6 * KV_WIDTH
    wa, wb = w_in[:, :CONV_CH], w_in[:, CONV_CH:o1]
    glu_cols = [jnp.concatenate([wa[:, 256 * s:256 * (s + 1)], wb[:, 256 * s:256 * (s + 1)]], axis=1)
                for s in range(4)]
    w_cat = jnp.concatenate(glu_cols + [w_in[:, o1:o3]], axis=1).astype(BF16)
    wg = w_in[:, o3:].reshape(D_MODEL, N_HEADS, 3).transpose(0, 2, 1).reshape(D_MODEL, 3 * N_HEADS)
    wg = jnp.pad(wg, ((0, 0), (0, LANES - 3 * N_HEADS))).astype(BF16)
    rows = lambda width: pl.BlockSpec((PROJ_TM, width), lambda i: (i, 0))
    cols_t = pl.BlockSpec((KV_WIDTH, PROJ_TM), lambda i: (0, i))
    return pl.pallas_call(
        _proj_kernel,
        grid=(n // PROJ_TM,),
        in_specs=[
            rows(D_MODEL),
            pl.BlockSpec((D_MODEL, 9 * PROJ_TN), lambda i: (0, 0), pipeline_mode=pl.Buffered(1)),
            pl.BlockSpec((D_MODEL, LANES), lambda i: (0, 0)),
        ],
        out_specs=[rows(CONV_CH), rows(Q_WIDTH), rows(2 * KV_WIDTH), rows(2 * KV_WIDTH), rows(2 * KV_WIDTH),
                   rows(LANES), rows(KV_WIDTH), cols_t, rows(KV_WIDTH), cols_t],
        out_shape=[
            jax.ShapeDtypeStruct((n, CONV_CH), F32),
            jax.ShapeDtypeStruct((n, Q_WIDTH), BF16),
            jax.ShapeDtypeStruct((n, 2 * KV_WIDTH), F32),
            jax.ShapeDtypeStruct((n, 2 * KV_WIDTH), F32),
            jax.ShapeDtypeStruct((n, 2 * KV_WIDTH), F32),
            jax.ShapeDtypeStruct((n, LANES), F32),
            jax.ShapeDtypeStruct((n, KV_WIDTH), BF16),
            jax.ShapeDtypeStruct((KV_WIDTH, n), BF16),
            jax.ShapeDtypeStruct((n, KV_WIDTH), BF16),
            jax.ShapeDtypeStruct((KV_WIDTH, n), BF16),
        ],
        scratch_shapes=[pltpu.VMEM((PROJ_TM, D_MODEL), BF16)],
        compiler_params=pltpu.CompilerParams(dimension_semantics=("parallel",), vmem_limit_bytes=PROJ_VMEM_LIMIT),
        name="proj",
    )(x, w_cat, wg)


CONV_T = 256
CONV_HALO = 32
CONV_CC = 256


def _ln_silu(y, g, beta):
    mu = jnp.mean(y, axis=-1, keepdims=True)
    yc = y - mu
    var = jnp.mean(yc * yc, axis=-1, keepdims=True)
    z = yc * lax.rsqrt(var + LN_EPS) * g + beta
    return z * jax.nn.sigmoid(z)


def _conv_prompt_kernel(cur_ref, prev_ref, w_ref, b_ref, g_ref, beta_ref, o_ref, xp_ref, y_ref, z_ref):
    i = pl.program_id(0)
    xp_ref[CONV_HALO:, :] = cur_ref[...]

    @pl.when(i == 0)
    def _():
        xp_ref[0:CONV_HALO, :] = jnp.zeros((CONV_HALO, CONV_CH), F32)

    @pl.when(i > 0)
    def _():
        xp_ref[0:CONV_HALO, :] = prev_ref[...]

    off = CONV_HALO - (CONV_WIDTH - 1)
    for c in range(CONV_CH // CONV_CC):
        cols = slice(c * CONV_CC, (c + 1) * CONV_CC)
        for b in range(8):
            n = CONV_T if b == 0 else CONV_T + 8
            z = None
            for k in range(CONV_WIDTH):
                if (off + k) % 8 != b:
                    continue
                a8 = off + k - b
                term = w_ref[k:k + 1, cols] * xp_ref[a8:a8 + n, cols]
                z = term if z is None else z + term
            z_ref[b, 0:n] = z
        acc = z_ref[0, 0:CONV_T]
        for b in range(1, 8):
            acc = acc + z_ref[b, b:b + CONV_T]
        y_ref[:, cols] = acc
    y = y_ref[...] + b_ref[...]
    o_ref[...] = _ln_silu(y, g_ref[...], beta_ref[...]).astype(BF16)


def _conv_prompt(a, conv_w, conv_b, ln_g, ln_b):
    wp = jnp.pad(conv_w, ((0, 1), (0, 0)))
    row = lambda v: v.reshape(1, CONV_CH)
    hb = CONV_T // CONV_HALO
    return pl.pallas_call(
        _conv_prompt_kernel,
        grid=(N_PROMPT // CONV_T,),
        in_specs=[
            pl.BlockSpec((CONV_T, CONV_CH), lambda i: (i, 0)),
            pl.BlockSpec((CONV_HALO, CONV_CH), lambda i: (jnp.maximum(i * hb - 1, 0), 0)),
            pl.BlockSpec((CONV_WIDTH + 1, CONV_CH), lambda i: (0, 0)),
            pl.BlockSpec((1, CONV_CH), lambda i: (0, 0)),
            pl.BlockSpec((1, CONV_CH), lambda i: (0, 0)),
            pl.BlockSpec((1, CONV_CH), lambda i: (0, 0)),
        ],
        out_specs=pl.BlockSpec((CONV_T, CONV_CH), lambda i: (i, 0)),
        out_shape=jax.ShapeDtypeStruct((N_PROMPT, CONV_CH), BF16),
        scratch_shapes=[pltpu.VMEM((CONV_T + CONV_HALO, CONV_CH), F32), pltpu.VMEM((CONV_T, CONV_CH), F32),
                        pltpu.VMEM((8, CONV_T + 8, CONV_CC), F32)],
        compiler_params=_cparams("parallel"),
        name="conv_prompt",
    )(a, a, wp, row(conv_b), row(ln_g), row(ln_b))


def _conv_sample_kernel(xt_ref, w_ref, b_ref, g_ref, beta_ref, o_ref):
    for t in range(DEC_SEQ):
        acc = jnp.zeros((DEC_BATCH, CONV_CH), F32)
        for k in range(CONV_WIDTH):
            acc = acc + w_ref[k:k + 1, :] * xt_ref[t + k]
        y = acc + b_ref[...]
        o_ref[t] = _ln_silu(y, g_ref[...], beta_ref[...]).astype(BF16)


def _conv_sample(xt, conv_w, conv_b, ln_g, ln_b):
    wp = jnp.pad(conv_w, ((0, 1), (0, 0)))
    row = lambda v: v.reshape(1, CONV_CH)
    npos = CONV_WIDTH - 1 + DEC_SEQ
    return pl.pallas_call(
        _conv_sample_kernel,
        grid=(1,),
        in_specs=[
            pl.BlockSpec((npos, DEC_BATCH, CONV_CH), lambda i: (0, 0, 0)),
            pl.BlockSpec((CONV_WIDTH + 1, CONV_CH), lambda i: (0, 0)),
            pl.BlockSpec((1, CONV_CH), lambda i: (0, 0)),
            pl.BlockSpec((1, CONV_CH), lambda i: (0, 0)),
            pl.BlockSpec((1, CONV_CH), lambda i: (0, 0)),
        ],
        out_specs=pl.BlockSpec((DEC_SEQ, DEC_BATCH, CONV_CH), lambda i: (0, 0, 0)),
        out_shape=jax.ShapeDtypeStruct((DEC_SEQ, DEC_BATCH, CONV_CH), BF16),
        compiler_params=_cparams("arbitrary"),
        name="conv_sample",
    )(xt, wp, row(conv_b), row(ln_g), row(ln_b))


HALF = HEAD_DIM
NEG_BIG = -1e30
CHUNKS_PER_PAGE = PAGE_SIZE // COMP_STRIDE
CHUNK_LANES = COMP_STRIDE * 2 * KV_WIDTH


def _alibi_slope(h):
    return float(2.0 ** (-8.0 * (h + 1) / N_HEADS) * LOG2E)


def _gelu_tanh(x):
    c = np.float32(np.sqrt(2.0 / np.pi))
    return 0.5 * x * (1.0 + jnp.tanh(c * (x + 0.044715 * (x * x * x))))


def _dot_nt(a, b, precision=None):
    return lax.dot_general(a, b, (((1,), (1,)), ((), ())), preferred_element_type=F32, precision=precision)


def _half_mask(shape, half):
    lane = lax.broadcasted_iota(jnp.int32, shape, len(shape) - 1)
    return (lane >= HALF) if half else (lane < HALF)


def _swap_halves(x):
    return pltpu.roll(x, HALF, axis=x.ndim - 1)


def _compress_weights(pos, w1, w2):
    eye2 = jnp.eye(2, dtype=F32)
    base = w1.reshape(2, COMP_STRIDE, HEAD_DIM, CMP_HIDDEN)
    w1p = jnp.einsum('rsde,hg->rshdge', base, eye2).reshape(2, COMP_STRIDE * 2 * HEAD_DIM, 2 * CMP_HIDDEN)
    pe = jnp.broadcast_to(pos.reshape(2, COMP_STRIDE, 1, HEAD_DIM), (2, COMP_STRIDE, 2, HEAD_DIM))
    pe = pe.reshape(2, 1, COMP_STRIDE * 2 * HEAD_DIM)
    w2p = jnp.einsum('ed,hg->hegd', w2, eye2).reshape(2 * CMP_HIDDEN, 2 * HEAD_DIM)
    return w1p.astype(BF16), pe, w2p.astype(BF16)


def _compress_pair(x, pe_ref, w1_ref, w2_ref):
    n = x.shape[0]
    a0 = jnp.dot((x + pe_ref[0]).astype(BF16), w1_ref[0], preferred_element_type=F32)
    a1 = jnp.dot((x + pe_ref[1]).astype(BF16), w1_ref[1], preferred_element_type=F32)
    hid = a0 + pltpu.roll(a1, n - 1, axis=0)
    return jnp.dot(_gelu_tanh(hid).astype(BF16), w2_ref[...], preferred_element_type=F32)


def _overlap_map(nc, ns, nc_pad, ns_pad):
    i = np.arange(nc_pad)[None, :]
    j = np.arange(ns_pad)[:, None]
    c_start, s_start = i * COMP_STRIDE, j * SEL_BLOCK
    m = (c_start <= s_start + SEL_BLOCK - 1) & (c_start + COMP_BLOCK - 1 >= s_start) & (i < nc) & (j < ns)
    return jnp.asarray(m.astype(np.float32))


def _select_blocks(imp_t, q_pos, n_iter):
    nb = imp_t.shape[0]
    blk = lax.broadcasted_iota(jnp.int32, imp_t.shape, 0)
    blk_f = blk.astype(F32)
    cur = q_pos >> 6
    valid = blk * SEL_BLOCK <= q_pos
    forced = (blk == 0) | (blk == cur) | (blk == cur - 1)
    score = jnp.where(valid, jnp.where(forced, 3e38, imp_t), -1.0)
    for _ in range(n_iter):
        mx = jnp.max(score, axis=0, keepdims=True)
        first = jnp.min(jnp.where(score == mx, blk_f, float(nb)), axis=0, keepdims=True)
        score = jnp.where(blk_f == first, -2.0, score)
    return jnp.where(valid & (score == -2.0), 1.0, 0.0)


def _select_blocks_by_rank(imp_t, q_pos, n_sel, n_blocks):
    nb = imp_t.shape[0]
    rows = -(-n_blocks // 8) * 8
    blk = lax.broadcasted_iota(jnp.int32, (rows, imp_t.shape[1]), 0)
    cur = q_pos >> 6
    valid = blk * SEL_BLOCK <= q_pos
    forced = (blk == 0) | (blk == cur) | (blk == cur - 1)
    score = jnp.where(valid, jnp.where(forced, 3e38, imp_t[:rows]), -1.0)
    beaten_by = jnp.zeros(score.shape, F32)
    for i in range(n_blocks):
        s_i = score[i:i + 1, :]
        beats = (s_i > score) | ((s_i == score) & (blk > i))
        beaten_by = beaten_by + jnp.where(beats, 1.0, 0.0)
    sel = jnp.where((beaten_by < n_sel) & (score >= 0.0), 1.0, 0.0)
    return jnp.concatenate([sel, jnp.zeros((nb - rows, imp_t.shape[1]), F32)], axis=0)


P_CHUNKS = SEQ // COMP_STRIDE
P_NC = (SEQ - COMP_BLOCK) // COMP_STRIDE + 1
P_NS = SEQ // SEL_BLOCK


def _compress_prompt_kernel(*refs):
    c_refs = refs[:COMP_STRIDE]
    pe_ref, w1_ref, w2_ref, o_ref = refs[COMP_STRIDE:]
    x = jnp.concatenate([r[...] for r in c_refs], axis=1)
    o_ref[...] = _compress_pair(x, pe_ref, w1_ref, w2_ref).astype(BF16)


def _compress_prompt(kvc, cw):
    pos_k, w1_k, w2_k, pos_v, w1_v, w2_v = cw
    wk, wv = _compress_weights(pos_k, w1_k, w2_k), _compress_weights(pos_v, w1_v, w2_v)
    w1p, pe, w2p = (jnp.stack([a, b]) for a, b in zip(wk, wv))
    chunks = kvc.reshape(N_TOK // COMP_STRIDE, CHUNK_LANES)
    c_specs = [pl.BlockSpec((P_CHUNKS, LANES), lambda g, s=s: (0, s * 4 + g)) for s in range(COMP_STRIDE)]
    return pl.pallas_call(
        _compress_prompt_kernel,
        grid=(4,),
        in_specs=c_specs + [
            pl.BlockSpec((None, 2, 1, 2048), lambda g: (g // 2, 0, 0, 0)),
            pl.BlockSpec((None, 2, 2048, 256), lambda g: (g // 2, 0, 0, 0)),
            pl.BlockSpec((None, 256, LANES), lambda g: (g // 2, 0, 0)),
        ],
        out_specs=pl.BlockSpec((P_CHUNKS, LANES), lambda g: (0, g)),
        out_shape=jax.ShapeDtypeStruct((P_CHUNKS, 4 * LANES), BF16),
        compiler_params=_cparams("parallel"),
        name="compress_prompt",
    )(*([chunks] * COMP_STRIDE), pe, w1p, w2p)


QB = 256


def _stack_gqa(q_ref, k):
    qa = q_ref[:, 256 * k:256 * k + LANES]
    qb = q_ref[:, 256 * k + LANES:256 * (k + 1)]
    qas = _swap_halves(qa.astype(F32)).astype(BF16)
    qbs = _swap_halves(qb.astype(F32)).astype(BF16)
    order = [qa, qas, qb, qbs] if k % 2 == 0 else [qas, qa, qbs, qb]
    return jnp.concatenate(order, axis=0)


def _pair_up(o0, o1, half):
    return o0 + _swap_halves(o1) if half == 0 else _swap_halves(o0) + o1


def _cmp_prompt_kernel(q_ref, kcv_ref, smap_ref, oc_ref, drop_ref):
    i = pl.program_id(0)
    q0 = i * QB
    q_pos_col = q0 + lax.broadcasted_iota(jnp.int32, (QB, 1), 0)
    c_end = lax.broadcasted_iota(jnp.int32, (1, P_CHUNKS), 1) * COMP_STRIDE + (COMP_BLOCK - 1)
    d_c = q_pos_col - c_end
    ok = d_c >= 0
    d_cf = d_c.astype(F32)
    q_pos_row = q0 + lax.broadcasted_iota(jnp.int32, (1, QB), 1)
    for k in range(N_KV):
        pair, half = k // 2, k % 2
        keep = _half_mask((P_CHUNKS, LANES), half)
        kc = jnp.where(keep, kcv_ref[:, LANES * pair:LANES * (pair + 1)], 0)
        vc = jnp.where(keep, kcv_ref[:, 2 * LANES + LANES * pair:2 * LANES + LANES * (pair + 1)], 0)
        s4 = _dot_nt(_stack_gqa(q_ref, k), kc)
        psum = jnp.zeros((QB, P_CHUNKS), F32)
        outs = []
        for g in range(GQA):
            s = s4[g * QB:(g + 1) * QB] - _alibi_slope(4 * k + g) * d_cf
            s = jnp.where(ok, s, NEG_BIG)
            m = jnp.max(s, axis=-1, keepdims=True)
            e = jnp.where(ok, jnp.exp2(s - m), 0.0)
            den = jnp.sum(e, axis=-1, keepdims=True)
            p = e / jnp.where(den > 0, den, 1.0)
            psum = psum + p
            outs.append(jnp.dot(p.astype(BF16), vc, preferred_element_type=F32))
        oc_ref[:, 256 * k:256 * k + LANES] = _pair_up(outs[0], outs[1], half)
        oc_ref[:, 256 * k + LANES:256 * (k + 1)] = _pair_up(outs[2], outs[3], half)
        smap = smap_ref[...]
        p_hi = psum.astype(BF16)
        rest = psum - p_hi.astype(F32)
        p_mid = rest.astype(BF16)
        p_lo = (rest - p_mid.astype(F32)).astype(BF16)
        imp_t = _dot_nt(smap, p_lo) + _dot_nt(smap, p_mid) + _dot_nt(smap, p_hi)
        sel_t = _select_blocks(imp_t, q_pos_row, N_SEL)
        drop_ref[:, LANES * k:LANES * (k + 1)] = (1.0 - sel_t.T).astype(BF16)


def _cmp_prompt(q, kcv):
    smap_t = _overlap_map(P_NC, P_NS, P_CHUNKS, P_NS).astype(BF16)
    return pl.pallas_call(
        _cmp_prompt_kernel,
        grid=(N_PROMPT // QB,),
        in_specs=[
            pl.BlockSpec((QB, Q_WIDTH), lambda i: (i, 0)),
            pl.BlockSpec((P_CHUNKS, 4 * LANES), lambda i: (0, 0)),
            pl.BlockSpec((P_NS, P_CHUNKS), lambda i: (0, 0)),
        ],
        out_specs=[
            pl.BlockSpec((QB, Q_WIDTH), lambda i: (i, 0)),
            pl.BlockSpec((QB, N_KV * P_NS), lambda i: (i, 0)),
        ],
        out_shape=[
            jax.ShapeDtypeStruct((N_PROMPT, Q_WIDTH), F32),
            jax.ShapeDtypeStruct((N_PROMPT, N_KV * P_NS), BF16),
        ],
        compiler_params=_cparams("parallel"),
        name="nsa_cmp_prompt",
    )(q, kcv, smap_t)


TKT = 512
POS_PIECES = 3


def _bf16_pieces(x, n):
    out = []
    for _ in range(n):
        p = float(np.float32(x).astype(jnp.bfloat16))
        out.append(p)
        x = x - p
    return out


def _position_lanes():
    pos = np.arange(SEQ)
    out = np.zeros((2, SEQ, LANES), np.float32)
    for half in range(2):
        base = HALF * (1 - half)
        for j in range(POS_PIECES):
            out[half, :, base + j] = (pos >> 6) * SEL_BLOCK
            out[half, :, base + POS_PIECES + j] = pos & (SEL_BLOCK - 1)
    return jnp.asarray(out, BF16)


def _block_drop_rows():
    m = np.where(np.arange(P_NS)[None, :] == (np.arange(SEQ)[:, None] // SEL_BLOCK), NEG_BIG, 0.0)
    return jnp.asarray(m.astype(np.float32), BF16)


def _selw_kernel(q_ref, g_ref, oc_ref, drop_ref, ksk_ref, vst_ref, kwk_ref, vwt_ref, pos_ref, e_ref, o_ref,
                 qa_ref, m_ref, acc_ref, os_ref):
    i = pl.program_id(0)
    q0 = pl.multiple_of(i * QB, QB)
    lane1 = lax.broadcasted_iota(jnp.int32, (1, LANES), 1)

    for k in range(N_KV):
        half = k % 2
        base = HALF * (1 - half)
        qa = q_ref[:, 256 * k:256 * k + LANES]
        qb = q_ref[:, 256 * k + LANES:256 * (k + 1)]
        qas = _swap_halves(qa.astype(F32)).astype(BF16)
        qbs = _swap_halves(qb.astype(F32)).astype(BF16)
        order = [qa, qas, qb, qbs] if half == 0 else [qas, qa, qbs, qb]
        own = _half_mask((QB, LANES), half)
        parts = []
        for g in range(GQA):
            sv = jnp.zeros((1, LANES), F32)
            pieces = _bf16_pieces(_alibi_slope(4 * k + g), POS_PIECES)
            for j, piece in enumerate(pieces + pieces):
                sv = jnp.where(lane1 == base + j, piece, sv)
            parts.append(jnp.where(own, order[g], sv.astype(BF16)))
        drop = drop_ref[:, LANES * k:LANES * (k + 1)]
        qa_ref[k] = jnp.concatenate([jnp.concatenate(parts, axis=0), jnp.concatenate([drop] * GQA, axis=0)], axis=1)

    key_row = lax.broadcasted_iota(jnp.int32, (TKT, 1), 0)
    q_lane = lax.broadcasted_iota(jnp.int32, (1, GQA * QB), 1) & (QB - 1)
    vrow = lax.broadcasted_iota(jnp.int32, (LANES, 1), 0)

    def tile(kk_ref, vt_ref, row0, selected, mode, first):
        if mode is not None:
            d = (q0 + q_lane) - (row0 + key_row)
            ok = (d >= 0) if mode == "causal" else ((d >= 0) & (d <= WINDOW))
        for k in range(N_KV):
            pair, half = k // 2, k % 2
            own = _half_mask((TKT, LANES), half)
            ka = jnp.where(own, kk_ref[pl.ds(row0, TKT), LANES * pair:LANES * (pair + 1)],
                           pos_ref[half, pl.ds(row0, TKT), :])
            if selected:
                st = _dot_nt(jnp.concatenate([ka, e_ref[pl.ds(row0, TKT), :]], axis=1), qa_ref[k])
            else:
                st = _dot_nt(ka, qa_ref[k, :, 0:LANES])
            if mode is not None:
                st = jnp.where(ok, st, NEG_BIG)
            m_tile = jnp.max(st, axis=0, keepdims=True)
            own_rows = (vrow >= HALF) if half else (vrow < HALF)
            ones_row = jnp.where(vrow == HALF * (1 - half), 1.0, 0.0).astype(BF16)
            va = jnp.where(own_rows, vt_ref[LANES * pair:LANES * (pair + 1), pl.ds(row0, TKT)], ones_row)
            if first:
                m_new = m_tile
            else:
                m_old = m_ref[k]
                m_new = jnp.maximum(m_old, m_tile)
            p = jnp.exp2(st - m_new).astype(BF16)
            pv = jnp.dot(va, p, preferred_element_type=F32)
            if first:
                acc_ref[k] = pv
            else:
                acc_ref[k] = jnp.exp2(m_old - m_new) * acc_ref[k] + pv
            m_ref[k] = m_new

    def normalised(k):
        half = k % 2
        den_row = HALF * (1 - half)
        return acc_ref[k] / acc_ref[k, den_row:den_row + 1, :]

    n_below = lax.div(i, jnp.int32(TKT // QB))
    diag0 = pl.multiple_of(n_below * TKT, TKT)
    tile(ksk_ref, vst_ref, diag0, True, "causal", True)

    def body(t, carry):
        tile(ksk_ref, vst_ref, pl.multiple_of(t * TKT, TKT), True, None, False)
        return carry

    lax.fori_loop(0, n_below, body, 0)
    for k in range(N_KV):
        os_ref[k] = normalised(k)

    tile(kwk_ref, vwt_ref, diag0, False, "window", True)

    @pl.when(n_below >= 1)
    def _():
        tile(kwk_ref, vwt_ref, pl.multiple_of(diag0 - TKT, TKT), False, "window", False)

    g_t = g_ref[...].T
    low = _half_mask((QB, LANES), 0)
    for k in range(N_KV):
        half = k % 2
        o_w = normalised(k)
        contrib = []
        for g in range(GQA):
            h = 4 * k + g
            cols = slice(g * QB, (g + 1) * QB)
            c_t = (os_ref[k, :, cols] * g_t[N_HEADS + h:N_HEADS + h + 1, :]
                   + o_w[:, cols] * g_t[2 * N_HEADS + h:2 * N_HEADS + h + 1, :])
            contrib.append(c_t.T)
        for pr in range(2):
            h0 = 4 * k + 2 * pr
            c0, c1 = contrib[2 * pr], contrib[2 * pr + 1]
            both = jnp.where(low, c0, _swap_halves(c1)) if half == 0 else jnp.where(low, _swap_halves(c0), c1)
            gate_c = jnp.where(low, g_ref[:, h0:h0 + 1], g_ref[:, h0 + 1:h0 + 2])
            cols = slice(256 * k + LANES * pr, 256 * k + LANES * (pr + 1))
            o_ref[:, cols] = (gate_c * oc_ref[:, cols] + both).astype(BF16)


def _selw_prompt(q, gates, o_c, drop, ks_k, vs_t, kw_k, vw_t):
    resident = lambda shape: pl.BlockSpec(shape, lambda i: (0,) * len(shape), pipeline_mode=pl.Buffered(1))
    return pl.pallas_call(
        _selw_kernel,
        grid=(N_PROMPT // QB,),
        in_specs=[
            pl.BlockSpec((QB, Q_WIDTH), lambda i: (i, 0)),
            pl.BlockSpec((QB, LANES), lambda i: (i, 0)),
            pl.BlockSpec((QB, Q_WIDTH), lambda i: (i, 0)),
            pl.BlockSpec((QB, N_KV * P_NS), lambda i: (i, 0)),
            resident((N_PROMPT, KV_WIDTH)),
            resident((KV_WIDTH, N_PROMPT)),
            resident((N_PROMPT, KV_WIDTH)),
            resident((KV_WIDTH, N_PROMPT)),
            resident((2, SEQ, LANES)),
            resident((SEQ, LANES)),
        ],
        out_specs=pl.BlockSpec((QB, Q_WIDTH), lambda i: (i, 0)),
        out_shape=jax.ShapeDtypeStruct((N_PROMPT, Q_WIDTH), BF16),
        scratch_shapes=[
            pltpu.VMEM((N_KV, GQA * QB, 2 * LANES), BF16),
            pltpu.VMEM((N_KV, 1, GQA * QB), F32),
            pltpu.VMEM((N_KV, LANES, GQA * QB), F32),
            pltpu.VMEM((N_KV, LANES, GQA * QB), F32),
        ],
        compiler_params=_cparams("parallel"),
        name="nsa_selw_prompt",
    )(q, gates, o_c, drop, ks_k, vs_t, kw_k, vw_t, _position_lanes(), _block_drop_rows())


N_PAGES = PAST_LEN // PAGE_SIZE
S_CHUNKS = PAST_LEN // COMP_STRIDE
S_NC = (PAST_LEN + DEC_SEQ - COMP_BLOCK) // COMP_STRIDE + 1
S_NS = -(-(PAST_LEN + DEC_SEQ) // SEL_BLOCK)
S_KEYS = 17 * LANES
WIN_BUF = min(WINDOW, PAST_LEN)
S_WKEYS = 5 * LANES
ROWS16 = GQA * DEC_SEQ
HEADS8 = 2 * N_KV


def _row_softmax(s, ok):
    s = jnp.where(ok, s, NEG_BIG)
    m = jnp.max(s, axis=-1, keepdims=True)
    e = jnp.where(ok, jnp.exp2(s - m), 0.0)
    den = jnp.sum(e, axis=-1, keepdims=True)
    return e / jnp.where(den > 0, den, 1.0)


def _nsa_sample_kernel(pt_ref, *refs):
    cp = refs[:N_PAGES]
    sp = refs[N_PAGES:2 * N_PAGES]
    (win_ref, news_ref, neww_ref, q_ref, g_ref, pe_ref, w1_ref, w2_ref, smap_ref, expand_ref,
     o_ref, ks_ref, kw_ref) = refs[2 * N_PAGES:]
    del pt_ref

    for c in range(HEADS8):
        kv, h = c // N_KV, c % N_KV
        for p in range(N_PAGES):
            ks_ref[c, :, PAGE_SIZE * p:PAGE_SIZE * (p + 1)] = sp[p][kv, h].astype(BF16)
        ks_ref[c, :, PAST_LEN:] = news_ref[c]
        kw_ref[c, :, 0:WIN_BUF] = win_ref[kv, h].astype(BF16)
        kw_ref[c, :, WIN_BUF:] = neww_ref[c]

    kcv = []
    for kv in range(2):
        x = jnp.concatenate([cp[p][N_KV * kv + h] for h in range(N_KV) for p in range(N_PAGES)], axis=0)
        a0 = jnp.dot((x + pe_ref[kv, 0]).astype(BF16), w1_ref[kv, 0], preferred_element_type=F32)
        a1 = jnp.dot((x + pe_ref[kv, 1]).astype(BF16), w1_ref[kv, 1], preferred_element_type=F32)
        hid = a0 + pltpu.roll(a1, N_KV * S_CHUNKS - 1, axis=0)
        out = jnp.dot(_gelu_tanh(hid).astype(BF16), w2_ref[kv], preferred_element_type=F32).astype(BF16)
        kcv += [out[S_CHUNKS * h:S_CHUNKS * (h + 1)] for h in range(N_KV)]

    row = lax.broadcasted_iota(jnp.int32, (ROWS16, 1), 0)
    q_pos = PAST_LEN + (row & (DEC_SEQ - 1))
    grp = row >> 2

    def slopes(k):
        s = jnp.full((ROWS16, 1), _alibi_slope(4 * k), F32)
        for g in range(1, GQA):
            s = jnp.where(grp == g, _alibi_slope(4 * k + g), s)
        return s

    c_end = lax.broadcasted_iota(jnp.int32, (1, S_CHUNKS), 1) * COMP_STRIDE + (COMP_BLOCK - 1)
    d_c = q_pos - c_end
    o_cmp, psums = [], []
    for k in range(N_KV):
        s = _dot_nt(q_ref[k], kcv[k]) - slopes(k) * d_c.astype(F32)
        p = _row_softmax(s, d_c >= 0)
        o_cmp.append(jnp.dot(p.astype(BF16), kcv[N_KV + k], preferred_element_type=F32))
        ps = p
        for g in range(1, GQA):
            ps = ps + pltpu.roll(p, DEC_SEQ * g, axis=0)
        psums.append(ps)
    psum_all = jnp.concatenate(psums + [jnp.zeros((LANES - N_KV * ROWS16, S_CHUNKS), F32)], axis=0)
    imp_t = _dot_nt(smap_ref[...], psum_all, precision=lax.Precision.HIGHEST)
    col = lax.broadcasted_iota(jnp.int32, (1, LANES), 1)
    sel_t = _select_blocks_by_rank(imp_t, PAST_LEN + (col & (DEC_SEQ - 1)), N_SEL, S_NS)
    sel = sel_t.T.astype(BF16)

    key_s = lax.broadcasted_iota(jnp.int32, (1, S_KEYS), 1)
    d_s = q_pos - key_s
    idx_w = lax.broadcasted_iota(jnp.int32, (1, S_WKEYS), 1)
    d_w = q_pos - (PAST_LEN - WIN_BUF + idx_w)
    ok_w = (d_w >= 0) & (d_w <= WINDOW)
    for k in range(N_KV):
        chosen = jnp.dot(sel[ROWS16 * k:ROWS16 * (k + 1)], expand_ref[...], preferred_element_type=F32)
        s = jnp.dot(q_ref[k], ks_ref[k], preferred_element_type=F32) - slopes(k) * d_s.astype(F32)
        p = _row_softmax(s, (chosen > 0.5) & (d_s >= 0))
        o_sel = _dot_nt(p.astype(BF16), ks_ref[N_KV + k])
        s = jnp.dot(q_ref[k], kw_ref[k], preferred_element_type=F32) - slopes(k) * d_w.astype(F32)
        p = _row_softmax(s, ok_w)
        o_win = _dot_nt(p.astype(BF16), kw_ref[N_KV + k])
        o_ref[k] = (g_ref[k][:, 0:1] * o_cmp[k] + g_ref[k][:, 1:2] * o_sel + g_ref[k][:, 2:3] * o_win)


def _nsa_sample(page_table, cache_c, cache_s, win_prev, q_s, gates_s, kvs_new, kvw_new, cw):
    pos_k, w1_k, w2_k, pos_v, w1_v, w2_v = cw
    flat = COMP_STRIDE * HEAD_DIM
    n_phys = cache_c.shape[1]
    w1 = jnp.stack([w1_k, w1_v]).reshape(2, 2, flat, CMP_HIDDEN).astype(BF16)
    pe = jnp.stack([pos_k, pos_v]).reshape(2, 2, 1, flat)
    w2 = jnp.stack([w2_k, w2_v]).astype(BF16)
    chunks = cache_c.reshape(n_phys, CHUNKS_PER_PAGE, COMP_STRIDE, HEADS8, HEAD_DIM)
    chunks = chunks.transpose(0, 3, 1, 2, 4).reshape(n_phys, HEADS8, CHUNKS_PER_PAGE, flat)
    slc_t = cache_s.transpose(0, 1, 3, 4, 5, 2)
    win_t = win_prev.transpose(0, 1, 3, 4, 5, 2)
    qh = q_s.reshape(DEC_SEQ, DEC_BATCH, N_KV, GQA, HEAD_DIM).transpose(1, 2, 3, 0, 4)
    q16 = qh.reshape(DEC_BATCH, N_KV, ROWS16, HEAD_DIM)
    gh = gates_s[:, :3 * N_HEADS].reshape(DEC_SEQ, DEC_BATCH, 3, N_KV, GQA).transpose(1, 3, 4, 0, 2)
    g16 = jnp.pad(gh.reshape(DEC_BATCH, N_KV, ROWS16, 3), ((0, 0), (0, 0), (0, 0), (0, LANES - 3)))

    def new_rows_t(a, width):
        a = a.reshape(DEC_SEQ, DEC_BATCH, HEADS8, HEAD_DIM).transpose(1, 2, 3, 0).astype(BF16)
        return jnp.pad(a, ((0, 0), (0, 0), (0, 0), (0, width - DEC_SEQ)))

    smap_t = _overlap_map(S_NC, S_NS, S_CHUNKS, LANES)
    expand = jnp.asarray((np.arange(LANES)[:, None] == (np.arange(S_KEYS)[None, :] // SEL_BLOCK)), BF16)
    pt = page_table.reshape(-1).astype(jnp.int32)
    c_specs = [pl.BlockSpec((None, HEADS8, CHUNKS_PER_PAGE, flat), lambda b, pt, p=p: (pt[b * N_PAGES + p], 0, 0, 0))
               for p in range(N_PAGES)]
    s_specs = [pl.BlockSpec((None, None, 2, N_KV, HEAD_DIM, PAGE_SIZE),
                            lambda b, pt, p=p: (0, pt[b * N_PAGES + p], 0, 0, 0, 0)) for p in range(N_PAGES)]
    per_b4 = lambda shape: pl.BlockSpec((None,) + shape, lambda b, pt: (b, 0, 0, 0))
    const = lambda shape: pl.BlockSpec(shape, lambda b, pt: (0,) * len(shape))
    o16 = pl.pallas_call(
        _nsa_sample_kernel,
        grid_spec=pltpu.PrefetchScalarGridSpec(
            num_scalar_prefetch=1,
            grid=(DEC_BATCH,),
            in_specs=c_specs + s_specs + [
                pl.BlockSpec((None, None, 2, N_KV, HEAD_DIM, WIN_BUF), lambda b, pt: (0, b, 0, 0, 0, 0)),
                per_b4((HEADS8, HEAD_DIM, S_KEYS - PAST_LEN)),
                per_b4((HEADS8, HEAD_DIM, S_WKEYS - WIN_BUF)),
                per_b4((N_KV, ROWS16, HEAD_DIM)),
                per_b4((N_KV, ROWS16, LANES)),
                const((2, 2, 1, flat)),
                const((2, 2, flat, CMP_HIDDEN)),
                const((2, CMP_HIDDEN, HEAD_DIM)),
                const((LANES, S_CHUNKS)),
                const((LANES, S_KEYS)),
            ],
            out_specs=per_b4((N_KV, ROWS16, HEAD_DIM)),
            scratch_shapes=[pltpu.VMEM((HEADS8, HEAD_DIM, S_KEYS), BF16), pltpu.VMEM((HEADS8, HEAD_DIM, S_WKEYS), BF16)],
        ),
        out_shape=jax.ShapeDtypeStruct((DEC_BATCH, N_KV, ROWS16, HEAD_DIM), F32),
        compiler_params=_cparams("parallel"),
        name="nsa_sample",
    )(pt, *([chunks] * N_PAGES), *([slc_t] * N_PAGES), win_t,
      new_rows_t(kvs_new, S_KEYS - PAST_LEN), new_rows_t(kvw_new, S_WKEYS - WIN_BUF),
      q16, g16, pe, w1, w2, smap_t, expand)
    o = o16.reshape(DEC_BATCH, N_KV, GQA, DEC_SEQ, HEAD_DIM)
    return o.transpose(3, 0, 1, 2, 4).reshape(N_SAMPLE, Q_WIDTH).astype(BF16)


OUT_TM = 512
ROUTE_E0, ROUTE_E1, ROUTE_W0, ROUTE_W1, ROUTE_R0, ROUTE_R1 = range(6)


def _layer_norm(x, g, b):
    mu = jnp.mean(x, axis=-1, keepdims=True)
    xc = x - mu
    var = jnp.mean(xc * xc, axis=-1, keepdims=True)
    return xc * lax.rsqrt(var + LN_EPS) * g + b


def _first_max(vals, lane_f):
    mx = jnp.max(vals, axis=-1, keepdims=True)
    idx = jnp.min(jnp.where(vals == mx, lane_f, float(LANES)), axis=-1, keepdims=True)
    return mx, idx


def _out_route_kernel(conv_ref, attn_ref, x_ref, wo_ref, g1_ref, b1_ref, wr_ref, br_ref,
                      h_ref, route_ref, counts_ref, run_ref):
    i = pl.program_id(0)

    @pl.when(i == 0)
    def _():
        run_ref[...] = jnp.zeros_like(run_ref)

    mix = jnp.dot(conv_ref[...], wo_ref[0:CONV_CH, :], preferred_element_type=F32)
    mix = mix + jnp.dot(attn_ref[...], wo_ref[CONV_CH:, :], preferred_element_type=F32)
    h = _layer_norm(ALPHA * x_ref[...] + mix, g1_ref[...], b1_ref[...])
    h_ref[...] = h

    logit = jnp.dot(h, wr_ref[...], preferred_element_type=F32, precision=lax.Precision.HIGHEST) + br_ref[...]
    lane = lax.broadcasted_iota(jnp.int32, logit.shape, 1)
    lane_f = lane.astype(F32)
    is_grp = (lane >= N_EXPERTS) & (lane < N_EXPERTS + N_GROUPS)
    g_logit = jnp.where(is_grp, logit, NEG_BIG)
    g_top, g_lane = _first_max(g_logit, lane_f)
    lse = g_top + jnp.log(jnp.sum(jnp.where(is_grp, jnp.exp(g_logit - g_top), 0.0), axis=-1, keepdims=True))
    g_p = jnp.exp(g_top - lse)
    g_idx = g_lane.astype(jnp.int32) - N_EXPERTS
    in_grp = (lane >> 3) == g_idx
    e_logit = jnp.where(in_grp, logit, NEG_BIG)
    e0, i0 = _first_max(e_logit, lane_f)
    e1, i1 = _first_max(jnp.where(lane_f == i0, NEG_BIG, e_logit), lane_f)
    z1 = jnp.exp(e1 - e0)
    w0 = g_p / (1.0 + z1)
    w1 = g_p * z1 / (1.0 + z1)

    hit = jnp.where((lane_f == i0) | (lane_f == i1), 1.0, 0.0)
    r = lax.broadcasted_iota(jnp.int32, (OUT_TM, OUT_TM), 0)
    c = lax.broadcasted_iota(jnp.int32, (OUT_TM, OUT_TM), 1)
    lower = jnp.where(c < r, 1.0, 0.0).astype(BF16)
    before = jnp.dot(lower, hit.astype(BF16), preferred_element_type=F32) + run_ref[...]
    r0 = jnp.sum(jnp.where(lane_f == i0, before, 0.0), axis=-1, keepdims=True)
    r1 = jnp.sum(jnp.where(lane_f == i1, before, 0.0), axis=-1, keepdims=True)
    run_ref[...] = run_ref[...] + jnp.sum(hit, axis=0, keepdims=True)
    counts_ref[...] = jnp.broadcast_to(run_ref[...], counts_ref.shape)

    rec = jnp.zeros(logit.shape, F32)
    for col_id, v in ((ROUTE_E0, i0), (ROUTE_E1, i1), (ROUTE_W0, w0), (ROUTE_W1, w1), (ROUTE_R0, r0), (ROUTE_R1, r1)):
        rec = jnp.where(lane == col_id, v, rec)
    route_ref[...] = rec


def _out_route(conv, attn, x, w_out, ln_g, ln_b, w_grp, b_grp, w_exp, b_exp):
    n = x.shape[0]
    wr = jnp.pad(jnp.concatenate([w_exp, w_grp], axis=1), ((0, 0), (0, LANES - N_EXPERTS - N_GROUPS)))
    br = jnp.pad(jnp.concatenate([b_exp, b_grp]), (0, LANES - N_EXPERTS - N_GROUPS)).reshape(1, LANES)
    const = lambda shape: pl.BlockSpec(shape, lambda i: (0, 0))
    return pl.pallas_call(
        _out_route_kernel,
        grid=(n // OUT_TM,),
        in_specs=[
            pl.BlockSpec((OUT_TM, CONV_CH), lambda i: (i, 0)),
            pl.BlockSpec((OUT_TM, Q_WIDTH), lambda i: (i, 0)),
            pl.BlockSpec((OUT_TM, D_MODEL), lambda i: (i, 0)),
            pl.BlockSpec((D_MODEL, D_MODEL), lambda i: (0, 0), pipeline_mode=pl.Buffered(1)),
            const((1, D_MODEL)),
            const((1, D_MODEL)),
            const((D_MODEL, LANES)),
            const((1, LANES)),
        ],
        out_specs=[
            pl.BlockSpec((OUT_TM, D_MODEL), lambda i: (i, 0)),
            pl.BlockSpec((OUT_TM, LANES), lambda i: (i, 0)),
            const((8, LANES)),
        ],
        out_shape=[
            jax.ShapeDtypeStruct((n, D_MODEL), F32),
            jax.ShapeDtypeStruct((n, LANES), F32),
            jax.ShapeDtypeStruct((8, LANES), F32),
        ],
        scratch_shapes=[pltpu.VMEM((1, LANES), F32)],
        compiler_params=_cparams("arbitrary"),
        name="out_ln1_route",
    )(conv, attn, x, w_out.astype(BF16), ln_g.reshape(1, D_MODEL), ln_b.reshape(1, D_MODEL), wr, br)


MOE_BLK = 128
N_ASSIGN = N_TOK * TOP_K
MOE_BLOCKS = -(-(N_ASSIGN + N_EXPERTS * (MOE_BLK - 1)) // MOE_BLK)
MOE_ROWS = MOE_BLOCKS * MOE_BLK
CMB_TM = 128


def _gather_rows(src_hbm, dst_ref, sem, idx_ref, base, n_rows):
    def body(r, carry):
        pltpu.make_async_copy(src_hbm.at[pl.ds(idx_ref[base + r], 1)], dst_ref.at[pl.ds(r, 1)], sem).start()
        return carry
    lax.fori_loop(0, n_rows, body, 0)


def _wait_rows(src_hbm, dst_ref, sem):
    pltpu.make_async_copy(src_hbm.at[pl.ds(0, dst_ref.shape[0])], dst_ref, sem).wait()


def _wait_row_count(src_hbm, dst_ref, sem, n_rows):
    p = MOE_BLK
    while p >= 1:
        @pl.when((n_rows & p) != 0)
        def _(p=p):
            pltpu.make_async_copy(src_hbm.at[pl.ds(0, p)], dst_ref.at[pl.ds(0, p)], sem).wait()
        p //= 2


def _moe_kernel(blk_e_ref, row_tok_ref, blk_rows_ref, h_hbm, wg_ref, wu_ref, wd_ref, y_ref,
                xbuf, sems, wg_b, wu_b, wd_b):
    i = pl.program_id(0)
    slot = i % 2
    rows = blk_rows_ref[i]
    nxt = jnp.minimum(i + 1, MOE_BLOCKS - 1)
    rows_next = jnp.where(i + 1 < MOE_BLOCKS, blk_rows_ref[nxt], 0)

    @pl.when(i == 0)
    def _():
        xbuf[...] = jnp.zeros_like(xbuf)
        _gather_rows(h_hbm, xbuf.at[0], sems.at[0], row_tok_ref, 0, rows)

    @pl.when(rows_next > 0)
    def _():
        _gather_rows(h_hbm, xbuf.at[1 - slot], sems.at[1 - slot], row_tok_ref, (i + 1) * MOE_BLK, rows_next)

    @pl.when((i == 0) | (blk_e_ref[i] != blk_e_ref[jnp.maximum(i - 1, 0)]))
    def _():
        wg_b[...] = wg_ref[...].astype(BF16)
        wu_b[...] = wu_ref[...].astype(BF16)
        wd_b[...] = wd_ref[...].astype(BF16)

    _wait_row_count(h_hbm, xbuf.at[slot], sems.at[slot], rows)

    @pl.when(rows > 0)
    def _():
        x = xbuf[slot].astype(BF16)
        hg = jnp.dot(x, wg_b[...], preferred_element_type=F32)
        hu = jnp.dot(x, wu_b[...], preferred_element_type=F32)
        act = (hg * jax.nn.sigmoid(hg) * hu).astype(BF16)
        y_ref[...] = jnp.dot(act, wd_b[...], preferred_element_type=F32)

    @pl.when(rows == 0)
    def _():
        y_ref[...] = jnp.zeros_like(y_ref)


def _moe(h, blk_e, row_tok, blk_rows, w_gate, w_up, w_down):
    wspec = lambda shape: pl.BlockSpec((None,) + shape, lambda i, be, rt, br: (be[i], 0, 0))
    return pl.pallas_call(
        _moe_kernel,
        grid_spec=pltpu.PrefetchScalarGridSpec(
            num_scalar_prefetch=3,
            grid=(MOE_BLOCKS,),
            in_specs=[
                pl.BlockSpec(memory_space=pl.ANY),
                wspec((D_MODEL, MOE_D_FF)),
                wspec((D_MODEL, MOE_D_FF)),
                wspec((MOE_D_FF, D_MODEL)),
            ],
            out_specs=pl.BlockSpec((MOE_BLK, D_MODEL), lambda i, be, rt, br: (i, 0)),
            scratch_shapes=[
                pltpu.VMEM((2, MOE_BLK, D_MODEL), F32),
                pltpu.SemaphoreType.DMA((2,)),
                pltpu.VMEM((D_MODEL, MOE_D_FF), BF16),
                pltpu.VMEM((D_MODEL, MOE_D_FF), BF16),
                pltpu.VMEM((MOE_D_FF, D_MODEL), BF16),
            ],
        ),
        out_shape=jax.ShapeDtypeStruct((MOE_ROWS, D_MODEL), F32),
        compiler_params=_cparams("arbitrary"),
        name="moe_ffn",
    )(blk_e, row_tok, blk_rows, h, w_gate, w_up, w_down)


def _combine_kernel(dest_ref, y_hbm, h_ref, route_ref, g2_ref, b2_ref, o_ref, ybuf, sems):
    i = pl.program_id(0)
    n_steps = pl.num_programs(0)
    slot = i % 2

    @pl.when(i == 0)
    def _():
        _gather_rows(y_hbm, ybuf.at[0], sems.at[0], dest_ref, 0, TOP_K * CMB_TM)

    @pl.when(i + 1 < n_steps)
    def _():
        _gather_rows(y_hbm, ybuf.at[1 - slot], sems.at[1 - slot], dest_ref, (i + 1) * TOP_K * CMB_TM, TOP_K * CMB_TM)

    _wait_rows(y_hbm, ybuf.at[slot], sems.at[slot])
    w0 = route_ref[:, ROUTE_W0:ROUTE_W0 + 1]
    w1 = route_ref[:, ROUTE_W1:ROUTE_W1 + 1]
    moe = ybuf[slot, 0:CMB_TM] * w0 + ybuf[slot, CMB_TM:2 * CMB_TM] * w1
    o_ref[...] = _layer_norm(ALPHA * h_ref[...] + moe, g2_ref[...], b2_ref[...])


def _combine(dest, y_rows, h, route, ln_g, ln_b):
    n = h.shape[0]
    return pl.pallas_call(
        _combine_kernel,
        grid_spec=pltpu.PrefetchScalarGridSpec(
            num_scalar_prefetch=1,
            grid=(n // CMB_TM,),
            in_specs=[
                pl.BlockSpec(memory_space=pl.ANY),
                pl.BlockSpec((CMB_TM, D_MODEL), lambda i, d: (i, 0)),
                pl.BlockSpec((CMB_TM, LANES), lambda i, d: (i, 0)),
                pl.BlockSpec((1, D_MODEL), lambda i, d: (0, 0)),
                pl.BlockSpec((1, D_MODEL), lambda i, d: (0, 0)),
            ],
            out_specs=pl.BlockSpec((CMB_TM, D_MODEL), lambda i, d: (i, 0)),
            scratch_shapes=[pltpu.VMEM((2, TOP_K * CMB_TM, D_MODEL), F32), pltpu.SemaphoreType.DMA((2,))],
        ),
        out_shape=jax.ShapeDtypeStruct((n, D_MODEL), F32),
        compiler_params=_cparams("arbitrary"),
        name="moe_combine_ln2",
    )(dest, y_rows, h, route, ln_g.reshape(1, D_MODEL), ln_b.reshape(1, D_MODEL))


def _dispatch_indices(route, counts):
    eid = route[:, ROUTE_E0:ROUTE_E1 + 1].astype(jnp.int32)
    rank = route[:, ROUTE_R0:ROUTE_R1 + 1].astype(jnp.int32)
    cnt = counts[0, :N_EXPERTS].astype(jnp.int32)
    padded = (cnt + MOE_BLK - 1) // MOE_BLK * MOE_BLK
    pad_end = jnp.cumsum(padded)
    pad_start = pad_end - padded
    dest = pad_start[eid] + rank
    tok = jnp.broadcast_to(jnp.arange(N_TOK, dtype=jnp.int32)[:, None], dest.shape)
    row_tok = jnp.zeros((MOE_ROWS,), jnp.int32).at[dest.reshape(-1)].set(tok.reshape(-1))
    blk_start = jnp.arange(MOE_BLOCKS, dtype=jnp.int32) * MOE_BLK
    blk_e = jnp.minimum(jnp.sum(pad_end[None, :] <= blk_start[:, None], axis=1), N_EXPERTS - 1).astype(jnp.int32)
    blk_rows = jnp.clip(cnt[blk_e] - (blk_start - pad_start[blk_e]), 0, MOE_BLK).astype(jnp.int32)
    dest_tiles = dest.reshape(N_TOK // CMB_TM, CMB_TM, TOP_K).transpose(0, 2, 1).reshape(-1)
    return blk_e, row_tok, blk_rows, dest_tiles


def kernel(x_prompt, x_sample, cache_cmp_kv, cache_slc_kv, state_win_kv, state_conv, page_table, w_in, conv_w, conv_b, conv_ln_g, conv_ln_b, cmp_pos_k, cmp_w1_k, cmp_w2_k, cmp_pos_v, cmp_w1_v, cmp_w2_v, w_out, ln1_g, ln1_b, w_grp, b_grp, w_exp, b_exp, w_gate, w_up, w_down, ln2_g, ln2_b):
    xs_t = x_sample.transpose(1, 0, 2).reshape(N_SAMPLE, D_MODEL)
    x = jnp.concatenate([x_prompt.reshape(N_PROMPT, D_MODEL), xs_t], axis=0)
    glu, q, kvc, kvs, kvw, gates, ks_k, vs_t, kw_k, vw_t = _proj(x, w_in[0])
    conv_p = _conv_prompt(glu, conv_w[0], conv_b[0], conv_ln_g[0], conv_ln_b[0])
    a_s = glu[N_PROMPT:].reshape(DEC_SEQ, DEC_BATCH, CONV_CH)
    xt = jnp.concatenate([state_conv[0].transpose(1, 0, 2), a_s], axis=0)
    conv_s = _conv_sample(xt, conv_w[0], conv_b[0], conv_ln_g[0], conv_ln_b[0])
    cw = (cmp_pos_k[0], cmp_w1_k[0], cmp_w2_k[0], cmp_pos_v[0], cmp_w1_v[0], cmp_w2_v[0])
    kcv_p = _compress_prompt(kvc, cw)
    o_c, drop = _cmp_prompt(q, kcv_p)
    attn_p = _selw_prompt(q, gates, o_c, drop, ks_k, vs_t, kw_k, vw_t)
    attn_s = _nsa_sample(page_table, cache_cmp_kv, cache_slc_kv, state_win_kv, q[N_PROMPT:],
                         gates[N_PROMPT:], kvs[N_PROMPT:], kvw[N_PROMPT:], cw)
    conv = jnp.concatenate([conv_p, conv_s.reshape(N_SAMPLE, CONV_CH)], axis=0)
    attn = jnp.concatenate([attn_p, attn_s], axis=0)
    h1, route, counts = _out_route(conv, attn, x, w_out[0], ln1_g[0], ln1_b[0], w_grp[0], b_grp[0],
                                   w_exp[0], b_exp[0])
    blk_e, row_tok, blk_rows, dest_tiles = _dispatch_indices(route, counts)
    y_rows = _moe(h1, blk_e, row_tok, blk_rows, w_gate[0], w_up[0], w_down[0])
    y = _combine(dest_tiles, y_rows, h1, route, ln2_g[0], ln2_b[0])

    def batch_major(a):
        return a[N_PROMPT:].reshape(DEC_SEQ, DEC_BATCH, -1).transpose(1, 0, 2)

    kv_shape = (2, N_KV, HEAD_DIM)
    win_len = min(WINDOW, SEQ)
    y_prompt = y[:N_PROMPT].reshape(1, SEQ, D_MODEL)
    y_sample = batch_major(y)
    p_cmp = kvc[:N_PROMPT].reshape(1, 1, SEQ, *kv_shape)
    p_slc = kvs[:N_PROMPT].reshape(1, 1, SEQ, *kv_shape)
    p_win = kvw[N_PROMPT - win_len:N_PROMPT].reshape(1, 1, win_len, *kv_shape)
    p_conv = glu[N_PROMPT - (CONV_WIDTH - 1):N_PROMPT].reshape(1, 1, CONV_WIDTH - 1, CONV_CH)
    s_cmp = batch_major(kvc).reshape(1, DEC_BATCH, DEC_SEQ, *kv_shape)
    s_slc = batch_major(kvs).reshape(1, DEC_BATCH, DEC_SEQ, *kv_shape)
    s_win = jnp.concatenate([state_win_kv[0][:, DEC_SEQ:],
                             batch_major(kvw).reshape(DEC_BATCH, DEC_SEQ, *kv_shape)], axis=1)[None]
    s_conv = jnp.concatenate([state_conv[0][:, DEC_SEQ:], batch_major(glu)], axis=1)[None]
    return (y_prompt, y_sample, p_cmp, p_slc, p_win, p_conv, s_cmp, s_slc, s_win, s_conv)
```

```python
import numpy as np
import jax
import jax.numpy as jnp
from jax import lax
from jax.experimental import pallas as pl
from jax.experimental.pallas import tpu as pltpu

F32 = jnp.float32
BF16 = jnp.bfloat16

D_MODEL = 2048
SEQ = 8192
DEC_BATCH = 128
DEC_SEQ = 4
PAST_LEN = 2048
PAGE_SIZE = 128
CONV_CH = 1024
CONV_WIDTH = 31
N_HEADS = 16
HEAD_DIM = 64
N_KV = 4
GQA = 4
Q_WIDTH = 1024
KV_WIDTH = 256
COMP_BLOCK = 32
COMP_STRIDE = 16
CMP_HIDDEN = 128
SEL_BLOCK = 64
N_SEL = 16
WINDOW = 512
N_GROUPS = 8
EXPERTS_PER_GROUP = 8
N_EXPERTS = 64
TOP_K = 2
MOE_D_FF = 512
LN_EPS = 1e-5
ALPHA = 2.0 ** 0.25
LOG2E = 1.4426950408889634

N_PROMPT = SEQ
N_SAMPLE = DEC_BATCH * DEC_SEQ
N_TOK = N_PROMPT + N_SAMPLE

LANES = 128
VMEM_LIMIT = 48 * 1024 * 1024


def _cparams(*sem):
    return pltpu.CompilerParams(dimension_semantics=sem, vmem_limit_bytes=VMEM_LIMIT)


PROJ_TM = 512
PROJ_TN = 512
PROJ_VMEM_LIMIT = 56 * 1024 * 1024


def _proj_kernel(x_ref, w_ref, wg_ref, glu_ref, q_ref, kvc_ref, kvs_ref, kvw_ref, g_ref,
                 ksk_ref, vst_ref, kwk_ref, vwt_ref, xb_ref):
    xb_ref[...] = x_ref[...].astype(BF16)

    def cols(j):
        return jnp.dot(xb_ref[...], w_ref[:, PROJ_TN * j:PROJ_TN * (j + 1)], preferred_element_type=F32)

    for j in range(4):
        acc = cols(j)
        glu_ref[:, 256 * j:256 * (j + 1)] = acc[:, :256] * jax.nn.sigmoid(acc[:, 256:])
    for j in range(2):
        q_ref[:, PROJ_TN * j:PROJ_TN * (j + 1)] = (cols(4 + j) * (HEAD_DIM ** -0.5 * LOG2E)).astype(BF16)
    kvc_ref[...] = cols(6)
    acc = cols(7)
    kvs_ref[...] = acc
    ksk_ref[...] = acc[:, :KV_WIDTH].astype(BF16)
    vst_ref[...] = acc[:, KV_WIDTH:].T.astype(BF16)
    acc = cols(8)
    kvw_ref[...] = acc
    kwk_ref[...] = acc[:, :KV_WIDTH].astype(BF16)
    vwt_ref[...] = acc[:, KV_WIDTH:].T.astype(BF16)
    g_ref[...] = jax.nn.sigmoid(jnp.dot(xb_ref[...], wg_ref[...], preferred_element_type=F32))


def _proj(x, w_in):
    n = x.shape[0]
    o1 = 2 * CONV_CH
    o2 = o1 + Q_WIDTH
    o3 = o2 + 6 * KV_WIDTH
    wa, wb = w_in[:, :CONV_CH], w_in[:, CONV_CH:o1]
    glu_cols = [jnp.concatenate([wa[:, 256 * s:256 * (s + 1)], wb[:, 256 * s:256 * (s + 1)]], axis=1)
                for s in range(4)]
    w_cat = jnp.concatenate(glu_cols + [w_in[:, o1:o3]], axis=1).astype(BF16)
    wg = w_in[:, o3:].reshape(D_MODEL, N_HEADS, 3).transpose(0, 2, 1).reshape(D_MODEL, 3 * N_HEADS)
    wg = jnp.pad(wg, ((0, 0), (0, LANES - 3 * N_HEADS))).astype(BF16)
    rows = lambda width: pl.BlockSpec((PROJ_TM, width), lambda i: (i, 0))
    cols_t = pl.BlockSpec((KV_WIDTH, PROJ_TM), lambda i: (0, i))
    return pl.pallas_call(
        _proj_kernel,
        grid=(n // PROJ_TM,),
        in_specs=[
            rows(D_MODEL),
            pl.BlockSpec((D_MODEL, 9 * PROJ_TN), lambda i: (0, 0), pipeline_mode=pl.Buffered(1)),
            pl.BlockSpec((D_MODEL, LANES), lambda i: (0, 0)),
        ],
        out_specs=[rows(CONV_CH), rows(Q_WIDTH), rows(2 * KV_WIDTH), rows(2 * KV_WIDTH), rows(2 * KV_WIDTH),
                   rows(LANES), rows(KV_WIDTH), cols_t, rows(KV_WIDTH), cols_t],
        out_shape=[
            jax.ShapeDtypeStruct((n, CONV_CH), F32),
            jax.ShapeDtypeStruct((n, Q_WIDTH), BF16),
            jax.ShapeDtypeStruct((n, 2 * KV_WIDTH), F32),
            jax.ShapeDtypeStruct((n, 2 * KV_WIDTH), F32),
            jax.ShapeDtypeStruct((n, 2 * KV_WIDTH), F32),
            jax.ShapeDtypeStruct((n, LANES), F32),
            jax.ShapeDtypeStruct((n, KV_WIDTH), BF16),
            jax.ShapeDtypeStruct((KV_WIDTH, n), BF16),
            jax.ShapeDtypeStruct((n, KV_WIDTH), BF16),
            jax.ShapeDtypeStruct((KV_WIDTH, n), BF16),
        ],
        scratch_shapes=[pltpu.VMEM((PROJ_TM, D_MODEL), BF16)],
        compiler_params=pltpu.CompilerParams(dimension_semantics=("parallel",), vmem_limit_bytes=PROJ_VMEM_LIMIT),
        name="proj",
    )(x, w_cat, wg)


CONV_T = 256
CONV_HALO = 32
CONV_CC = 256


def _ln_silu(y, g, beta):
    mu = jnp.mean(y, axis=-1, keepdims=True)
    yc = y - mu
    var = jnp.mean(yc * yc, axis=-1, keepdims=True)
    z = yc * lax.rsqrt(var + LN_EPS) * g + beta
    return z * jax.nn.sigmoid(z)


def _conv_prompt_kernel(cur_ref, prev_ref, w_ref, b_ref, g_ref, beta_ref, o_ref, xp_ref, y_ref, z_ref):
    i = pl.program_id(0)
    xp_ref[CONV_HALO:, :] = cur_ref[...]

    @pl.when(i == 0)
    def _():
        xp_ref[0:CONV_HALO, :] = jnp.zeros((CONV_HALO, CONV_CH), F32)

    @pl.when(i > 0)
    def _():
        xp_ref[0:CONV_HALO, :] = prev_ref[...]

    off = CONV_HALO - (CONV_WIDTH - 1)
    for c in range(CONV_CH // CONV_CC):
        cols = slice(c * CONV_CC, (c + 1) * CONV_CC)
        for b in range(8):
            n = CONV_T if b == 0 else CONV_T + 8
            z = None
            for k in range(CONV_WIDTH):
                if (off + k) % 8 != b:
                    continue
                a8 = off + k - b
                term = w_ref[k:k + 1, cols] * xp_ref[a8:a8 + n, cols]
                z = term if z is None else z + term
            z_ref[b, 0:n] = z
        acc = z_ref[0, 0:CONV_T]
        for b in range(1, 8):
            acc = acc + z_ref[b, b:b + CONV_T]
        y_ref[:, cols] = acc
    y = y_ref[...] + b_ref[...]
    o_ref[...] = _ln_silu(y, g_ref[...], beta_ref[...]).astype(BF16)


def _conv_prompt(a, conv_w, conv_b, ln_g, ln_b):
    wp = jnp.pad(conv_w, ((0, 1), (0, 0)))
    row = lambda v: v.reshape(1, CONV_CH)
    hb = CONV_T // CONV_HALO
    return pl.pallas_call(
        _conv_prompt_kernel,
        grid=(N_PROMPT // CONV_T,),
        in_specs=[
            pl.BlockSpec((CONV_T, CONV_CH), lambda i: (i, 0)),
            pl.BlockSpec((CONV_HALO, CONV_CH), lambda i: (jnp.maximum(i * hb - 1, 0), 0)),
            pl.BlockSpec((CONV_WIDTH + 1, CONV_CH), lambda i: (0, 0)),
            pl.BlockSpec((1, CONV_CH), lambda i: (0, 0)),
            pl.BlockSpec((1, CONV_CH), lambda i: (0, 0)),
            pl.BlockSpec((1, CONV_CH), lambda i: (0, 0)),
        ],
        out_specs=pl.BlockSpec((CONV_T, CONV_CH), lambda i: (i, 0)),
        out_shape=jax.ShapeDtypeStruct((N_PROMPT, CONV_CH), BF16),
        scratch_shapes=[pltpu.VMEM((CONV_T + CONV_HALO, CONV_CH), F32), pltpu.VMEM((CONV_T, CONV_CH), F32),
                        pltpu.VMEM((8, CONV_T + 8, CONV_CC), F32)],
        compiler_params=_cparams("parallel"),
        name="conv_prompt",
    )(a, a, wp, row(conv_b), row(ln_g), row(ln_b))


def _conv_sample_kernel(xt_ref, w_ref, b_ref, g_ref, beta_ref, o_ref):
    for t in range(DEC_SEQ):
        acc = jnp.zeros((DEC_BATCH, CONV_CH), F32)
        for k in range(CONV_WIDTH):
            acc = acc + w_ref[k:k + 1, :] * xt_ref[t + k]
        y = acc + b_ref[...]
        o_ref[t] = _ln_silu(y, g_ref[...], beta_ref[...]).astype(BF16)


def _conv_sample(xt, conv_w, conv_b, ln_g, ln_b):
    wp = jnp.pad(conv_w, ((0, 1), (0, 0)))
    row = lambda v: v.reshape(1, CONV_CH)
    npos = CONV_WIDTH - 1 + DEC_SEQ
    return pl.pallas_call(
        _conv_sample_kernel,
        grid=(1,),
        in_specs=[
            pl.BlockSpec((npos, DEC_BATCH, CONV_CH), lambda i: (0, 0, 0)),
            pl.BlockSpec((CONV_WIDTH + 1, CONV_CH), lambda i: (0, 0)),
            pl.BlockSpec((1, CONV_CH), lambda i: (0, 0)),
            pl.BlockSpec((1, CONV_CH), lambda i: (0, 0)),
            pl.BlockSpec((1, CONV_CH), lambda i: (0, 0)),
        ],
        out_specs=pl.BlockSpec((DEC_SEQ, DEC_BATCH, CONV_CH), lambda i: (0, 0, 0)),
        out_shape=jax.ShapeDtypeStruct((DEC_SEQ, DEC_BATCH, CONV_CH), BF16),
        compiler_params=_cparams("arbitrary"),
        name="conv_sample",
    )(xt, wp, row(conv_b), row(ln_g), row(ln_b))


HALF = HEAD_DIM
NEG_BIG = -1e30
CHUNKS_PER_PAGE = PAGE_SIZE // COMP_STRIDE
CHUNK_LANES = COMP_STRIDE * 2 * KV_WIDTH


def _alibi_slope(h):
    return float(2.0 ** (-8.0 * (h + 1) / N_HEADS) * LOG2E)


def _gelu_tanh(x):
    c = np.float32(np.sqrt(2.0 / np.pi))
    return 0.5 * x * (1.0 + jnp.tanh(c * (x + 0.044715 * (x * x * x))))


def _dot_nt(a, b, precision=None):
    return lax.dot_general(a, b, (((1,), (1,)), ((), ())), preferred_element_type=F32, precision=precision)


def _half_mask(shape, half):
    lane = lax.broadcasted_iota(jnp.int32, shape, len(shape) - 1)
    return (lane >= HALF) if half else (lane < HALF)


def _swap_halves(x):
    return pltpu.roll(x, HALF, axis=x.ndim - 1)


def _compress_weights(pos, w1, w2):
    eye2 = jnp.eye(2, dtype=F32)
    base = w1.reshape(2, COMP_STRIDE, HEAD_DIM, CMP_HIDDEN)
    w1p = jnp.einsum('rsde,hg->rshdge', base, eye2).reshape(2, COMP_STRIDE * 2 * HEAD_DIM, 2 * CMP_HIDDEN)
    pe = jnp.broadcast_to(pos.reshape(2, COMP_STRIDE, 1, HEAD_DIM), (2, COMP_STRIDE, 2, HEAD_DIM))
    pe = pe.reshape(2, 1, COMP_STRIDE * 2 * HEAD_DIM)
    w2p = jnp.einsum('ed,hg->hegd', w2, eye2).reshape(2 * CMP_HIDDEN, 2 * HEAD_DIM)
    return w1p.astype(BF16), pe, w2p.astype(BF16)


def _compress_pair(x, pe_ref, w1_ref, w2_ref):
    n = x.shape[0]
    a0 = jnp.dot((x + pe_ref[0]).astype(BF16), w1_ref[0], preferred_element_type=F32)
    a1 = jnp.dot((x + pe_ref[1]).astype(BF16), w1_ref[1], preferred_element_type=F32)
    hid = a0 + pltpu.roll(a1, n - 1, axis=0)
    return jnp.dot(_gelu_tanh(hid).astype(BF16), w2_ref[...], preferred_element_type=F32)


def _overlap_map(nc, ns, nc_pad, ns_pad):
    i = np.arange(nc_pad)[None, :]
    j = np.arange(ns_pad)[:, None]
    c_start, s_start = i * COMP_STRIDE, j * SEL_BLOCK
    m = (c_start <= s_start + SEL_BLOCK - 1) & (c_start + COMP_BLOCK - 1 >= s_start) & (i < nc) & (j < ns)
    return jnp.asarray(m.astype(np.float32))


def _select_blocks(imp_t, q_pos, n_iter):
    nb = imp_t.shape[0]
    blk = lax.broadcasted_iota(jnp.int32, imp_t.shape, 0)
    blk_f = blk.astype(F32)
    cur = q_pos >> 6
    valid = blk * SEL_BLOCK <= q_pos
    forced = (blk == 0) | (blk == cur) | (blk == cur - 1)
    score = jnp.where(valid, jnp.where(forced, 3e38, imp_t), -1.0)
    for _ in range(n_iter):
        mx = jnp.max(score, axis=0, keepdims=True)
        first = jnp.min(jnp.where(score == mx, blk_f, float(nb)), axis=0, keepdims=True)
        score = jnp.where(blk_f == first, -2.0, score)
    return jnp.where(valid & (score == -2.0), 1.0, 0.0)


def _select_blocks_by_rank(imp_t, q_pos, n_sel, n_blocks):
    nb = imp_t.shape[0]
    rows = -(-n_blocks // 8) * 8
    blk = lax.broadcasted_iota(jnp.int32, (rows, imp_t.shape[1]), 0)
    cur = q_pos >> 6
    valid = blk * SEL_BLOCK <= q_pos
    forced = (blk == 0) | (blk == cur) | (blk == cur - 1)
    score = jnp.where(valid, jnp.where(forced, 3e38, imp_t[:rows]), -1.0)
    beaten_by = jnp.zeros(score.shape, F32)
    for i in range(n_blocks):
        s_i = score[i:i + 1, :]
        beats = (s_i > score) | ((s_i == score) & (blk > i))
        beaten_by = beaten_by + jnp.where(beats, 1.0, 0.0)
    sel = jnp.where((beaten_by < n_sel) & (score >= 0.0), 1.0, 0.0)
    return jnp.concatenate([sel, jnp.zeros((nb - rows, imp_t.shape[1]), F32)], axis=0)


P_CHUNKS = SEQ // COMP_STRIDE
P_NC = (SEQ - COMP_BLOCK) // COMP_STRIDE + 1
P_NS = SEQ // SEL_BLOCK


def _compress_prompt_kernel(*refs):
    c_refs = refs[:COMP_STRIDE]
    pe_ref, w1_ref, w2_ref, o_ref = refs[COMP_STRIDE:]
    x = jnp.concatenate([r[...] for r in c_refs], axis=1)
    o_ref[...] = _compress_pair(x, pe_ref, w1_ref, w2_ref).astype(BF16)


def _compress_prompt(kvc, cw):
    pos_k, w1_k, w2_k, pos_v, w1_v, w2_v = cw
    wk, wv = _compress_weights(pos_k, w1_k, w2_k), _compress_weights(pos_v, w1_v, w2_v)
    w1p, pe, w2p = (jnp.stack([a, b]) for a, b in zip(wk, wv))
    chunks = kvc.reshape(N_TOK // COMP_STRIDE, CHUNK_LANES)
    c_specs = [pl.BlockSpec((P_CHUNKS, LANES), lambda g, s=s: (0, s * 4 + g)) for s in range(COMP_STRIDE)]
    return pl.pallas_call(
        _compress_prompt_kernel,
        grid=(4,),
        in_specs=c_specs + [
            pl.BlockSpec((None, 2, 1, 2048), lambda g: (g // 2, 0, 0, 0)),
            pl.BlockSpec((None, 2, 2048, 256), lambda g: (g // 2, 0, 0, 0)),
            pl.BlockSpec((None, 256, LANES), lambda g: (g // 2, 0, 0)),
        ],
        out_specs=pl.BlockSpec((P_CHUNKS, LANES), lambda g: (0, g)),
        out_shape=jax.ShapeDtypeStruct((P_CHUNKS, 4 * LANES), BF16),
        compiler_params=_cparams("parallel"),
        name="compress_prompt",
    )(*([chunks] * COMP_STRIDE), pe, w1p, w2p)


QB = 256


def _stack_gqa(q_ref, k):
    qa = q_ref[:, 256 * k:256 * k + LANES]
    qb = q_ref[:, 256 * k + LANES:256 * (k + 1)]
    qas = _swap_halves(qa.astype(F32)).astype(BF16)
    qbs = _swap_halves(qb.astype(F32)).astype(BF16)
    order = [qa, qas, qb, qbs] if k % 2 == 0 else [qas, qa, qbs, qb]
    return jnp.concatenate(order, axis=0)


def _pair_up(o0, o1, half):
    return o0 + _swap_halves(o1) if half == 0 else _swap_halves(o0) + o1


def _cmp_prompt_kernel(q_ref, kcv_ref, smap_ref, oc_ref, drop_ref):
    i = pl.program_id(0)
    q0 = i * QB
    q_pos_col = q0 + lax.broadcasted_iota(jnp.int32, (QB, 1), 0)
    c_end = lax.broadcasted_iota(jnp.int32, (1, P_CHUNKS), 1) * COMP_STRIDE + (COMP_BLOCK - 1)
    d_c = q_pos_col - c_end
    ok = d_c >= 0
    d_cf = d_c.astype(F32)
    q_pos_row = q0 + lax.broadcasted_iota(jnp.int32, (1, QB), 1)
    for k in range(N_KV):
        pair, half = k // 2, k % 2
        keep = _half_mask((P_CHUNKS, LANES), half)
        kc = jnp.where(keep, kcv_ref[:, LANES * pair:LANES * (pair + 1)], 0)
        vc = jnp.where(keep, kcv_ref[:, 2 * LANES + LANES * pair:2 * LANES + LANES * (pair + 1)], 0)
        s4 = _dot_nt(_stack_gqa(q_ref, k), kc)
        psum = jnp.zeros((QB, P_CHUNKS), F32)
        outs = []
        for g in range(GQA):
            s = s4[g * QB:(g + 1) * QB] - _alibi_slope(4 * k + g) * d_cf
            s = jnp.where(ok, s, NEG_BIG)
            m = jnp.max(s, axis=-1, keepdims=True)
            e = jnp.where(ok, jnp.exp2(s - m), 0.0)
            den = jnp.sum(e, axis=-1, keepdims=True)
            p = e / jnp.where(den > 0, den, 1.0)
            psum = psum + p
            outs.append(jnp.dot(p.astype(BF16), vc, preferred_element_type=F32))
        oc_ref[:, 256 * k:256 * k + LANES] = _pair_up(outs[0], outs[1], half)
        oc_ref[:, 256 * k + LANES:256 * (k + 1)] = _pair_up(outs[2], outs[3], half)
        smap = smap_ref[...]
        p_hi = psum.astype(BF16)
        rest = psum - p_hi.astype(F32)
        p_mid = rest.astype(BF16)
        p_lo = (rest - p_mid.astype(F32)).astype(BF16)
        imp_t = _dot_nt(smap, p_lo) + _dot_nt(smap, p_mid) + _dot_nt(smap, p_hi)
        sel_t = _select_blocks(imp_t, q_pos_row, N_SEL)
        drop_ref[:, LANES * k:LANES * (k + 1)] = (1.0 - sel_t.T).astype(BF16)


def _cmp_prompt(q, kcv):
    smap_t = _overlap_map(P_NC, P_NS, P_CHUNKS, P_NS).astype(BF16)
    return pl.pallas_call(
        _cmp_prompt_kernel,
        grid=(N_PROMPT // QB,),
        in_specs=[
            pl.BlockSpec((QB, Q_WIDTH), lambda i: (i, 0)),
            pl.BlockSpec((P_CHUNKS, 4 * LANES), lambda i: (0, 0)),
            pl.BlockSpec((P_NS, P_CHUNKS), lambda i: (0, 0)),
        ],
        out_specs=[
            pl.BlockSpec((QB, Q_WIDTH), lambda i: (i, 0)),
            pl.BlockSpec((QB, N_KV * P_NS), lambda i: (i, 0)),
        ],
        out_shape=[
            jax.ShapeDtypeStruct((N_PROMPT, Q_WIDTH), F32),
            jax.ShapeDtypeStruct((N_PROMPT, N_KV * P_NS), BF16),
        ],
        compiler_params=_cparams("parallel"),
        name="nsa_cmp_prompt",
    )(q, kcv, smap_t)


TKT = 512
POS_PIECES = 3


def _bf16_pieces(x, n):
    out = []
    for _ in range(n):
        p = float(np.float32(x).astype(jnp.bfloat16))
        out.append(p)
        x = x - p
    return out


def _position_lanes():
    pos = np.arange(SEQ)
    out = np.zeros((2, SEQ, LANES), np.float32)
    for half in range(2):
        base = HALF * (1 - half)
        for j in range(POS_PIECES):
            out[half, :, base + j] = (pos >> 6) * SEL_BLOCK
            out[half, :, base + POS_PIECES + j] = pos & (SEL_BLOCK - 1)
    return jnp.asarray(out, BF16)


def _block_drop_rows():
    m = np.where(np.arange(P_NS)[None, :] == (np.arange(SEQ)[:, None] // SEL_BLOCK), NEG_BIG, 0.0)
    return jnp.asarray(m.astype(np.float32), BF16)


def _selw_kernel(q_ref, g_ref, oc_ref, drop_ref, ksk_ref, vst_ref, kwk_ref, vwt_ref, pos_ref, e_ref, o_ref,
                 qa_ref, m_ref, acc_ref, os_ref):
    i = pl.program_id(0)
    q0 = pl.multiple_of(i * QB, QB)
    lane1 = lax.broadcasted_iota(jnp.int32, (1, LANES), 1)

    for k in range(N_KV):
        half = k % 2
        base = HALF * (1 - half)
        qa = q_ref[:, 256 * k:256 * k + LANES]
        qb = q_ref[:, 256 * k + LANES:256 * (k + 1)]
        qas = _swap_halves(qa.astype(F32)).astype(BF16)
        qbs = _swap_halves(qb.astype(F32)).astype(BF16)
        order = [qa, qas, qb, qbs] if half == 0 else [qas, qa, qbs, qb]
        own = _half_mask((QB, LANES), half)
        parts = []
        for g in range(GQA):
            sv = jnp.zeros((1, LANES), F32)
            pieces = _bf16_pieces(_alibi_slope(4 * k + g), POS_PIECES)
            for j, piece in enumerate(pieces + pieces):
                sv = jnp.where(lane1 == base + j, piece, sv)
            parts.append(jnp.where(own, order[g], sv.astype(BF16)))
        drop = drop_ref[:, LANES * k:LANES * (k + 1)]
        qa_ref[k] = jnp.concatenate([jnp.concatenate(parts, axis=0), jnp.concatenate([drop] * GQA, axis=0)], axis=1)

    key_row = lax.broadcasted_iota(jnp.int32, (TKT, 1), 0)
    q_lane = lax.broadcasted_iota(jnp.int32, (1, GQA * QB), 1) & (QB - 1)
    vrow = lax.broadcasted_iota(jnp.int32, (LANES, 1), 0)

    def tile(kk_ref, vt_ref, row0, selected, mode, first):
        if mode is not None:
            d = (q0 + q_lane) - (row0 + key_row)
            ok = (d >= 0) if mode == "causal" else ((d >= 0) & (d <= WINDOW))
        for k in range(N_KV):
            pair, half = k // 2, k % 2
            own = _half_mask((TKT, LANES), half)
            ka = jnp.where(own, kk_ref[pl.ds(row0, TKT), LANES * pair:LANES * (pair + 1)],
                           pos_ref[half, pl.ds(row0, TKT), :])
            if selected:
                st = _dot_nt(jnp.concatenate([ka, e_ref[pl.ds(row0, TKT), :]], axis=1), qa_ref[k])
            else:
                st = _dot_nt(ka, qa_ref[k, :, 0:LANES])
            if mode is not None:
                st = jnp.where(ok, st, NEG_BIG)
            m_tile = jnp.max(st, axis=0, keepdims=True)
            own_rows = (vrow >= HALF) if half else (vrow < HALF)
            ones_row = jnp.where(vrow == HALF * (1 - half), 1.0, 0.0).astype(BF16)
            va = jnp.where(own_rows, vt_ref[LANES * pair:LANES * (pair + 1), pl.ds(row0, TKT)], ones_row)
            if first:
                m_new = m_tile
            else:
                m_old = m_ref[k]
                m_new = jnp.maximum(m_old, m_tile)
            p = jnp.exp2(st - m_new).astype(BF16)
            pv = jnp.dot(va, p, preferred_element_type=F32)
            if first:
                acc_ref[k] = pv
            else:
                acc_ref[k] = jnp.exp2(m_old - m_new) * acc_ref[k] + pv
            m_ref[k] = m_new

    def normalised(k):
        half = k % 2
        den_row = HALF * (1 - half)
        return acc_ref[k] / acc_ref[k, den_row:den_row + 1, :]

    n_below = lax.div(i, jnp.int32(TKT // QB))
    diag0 = pl.multiple_of(n_below * TKT, TKT)
    tile(ksk_ref, vst_ref, diag0, True, "causal", True)

    def body(t, carry):
        tile(ksk_ref, vst_ref, pl.multiple_of(t * TKT, TKT), True, None, False)
        return carry

    lax.fori_loop(0, n_below, body, 0)
    for k in range(N_KV):
        os_ref[k] = normalised(k)

    tile(kwk_ref, vwt_ref, diag0, False, "window", True)

    @pl.when(n_below >= 1)
    def _():
        tile(kwk_ref, vwt_ref, pl.multiple_of(diag0 - TKT, TKT), False, "window", False)

    g_t = g_ref[...].T
    low = _half_mask((QB, LANES), 0)
    for k in range(N_KV):
        half = k % 2
        o_w = normalised(k)
        contrib = []
        for g in range(GQA):
            h = 4 * k + g
            cols = slice(g * QB, (g + 1) * QB)
            c_t = (os_ref[k, :, cols] * g_t[N_HEADS + h:N_HEADS + h + 1, :]
                   + o_w[:, cols] * g_t[2 * N_HEADS + h:2 * N_HEADS + h + 1, :])
            contrib.append(c_t.T)
        for pr in range(2):
            h0 = 4 * k + 2 * pr
            c0, c1 = contrib[2 * pr], contrib[2 * pr + 1]
            both = jnp.where(low, c0, _swap_halves(c1)) if half == 0 else jnp.where(low, _swap_halves(c0), c1)
            gate_c = jnp.where(low, g_ref[:, h0:h0 + 1], g_ref[:, h0 + 1:h0 + 2])
            cols = slice(256 * k + LANES * pr, 256 * k + LANES * (pr + 1))
            o_ref[:, cols] = (gate_c * oc_ref[:, cols] + both).astype(BF16)


def _selw_prompt(q, gates, o_c, drop, ks_k, vs_t, kw_k, vw_t):
    resident = lambda shape: pl.BlockSpec(shape, lambda i: (0,) * len(shape), pipeline_mode=pl.Buffered(1))
    return pl.pallas_call(
        _selw_kernel,
        grid=(N_PROMPT // QB,),
        in_specs=[
            pl.BlockSpec((QB, Q_WIDTH), lambda i: (i, 0)),
            pl.BlockSpec((QB, LANES), lambda i: (i, 0)),
            pl.BlockSpec((QB, Q_WIDTH), lambda i: (i, 0)),
            pl.BlockSpec((QB, N_KV * P_NS), lambda i: (i, 0)),
            resident((N_PROMPT, KV_WIDTH)),
            resident((KV_WIDTH, N_PROMPT)),
            resident((N_PROMPT, KV_WIDTH)),
            resident((KV_WIDTH, N_PROMPT)),
            resident((2, SEQ, LANES)),
            resident((SEQ, LANES)),
        ],
        out_specs=pl.BlockSpec((QB, Q_WIDTH), lambda i: (i, 0)),
        out_shape=jax.ShapeDtypeStruct((N_PROMPT, Q_WIDTH), BF16),
        scratch_shapes=[
            pltpu.VMEM((N_KV, GQA * QB, 2 * LANES), BF16),
            pltpu.VMEM((N_KV, 1, GQA * QB), F32),
            pltpu.VMEM((N_KV, LANES, GQA * QB), F32),
            pltpu.VMEM((N_KV, LANES, GQA * QB), F32),
        ],
        compiler_params=_cparams("parallel"),
        name="nsa_selw_prompt",
    )(q, gates, o_c, drop, ks_k, vs_t, kw_k, vw_t, _position_lanes(), _block_drop_rows())


N_PAGES = PAST_LEN // PAGE_SIZE
S_CHUNKS = PAST_LEN // COMP_STRIDE
S_NC = (PAST_LEN + DEC_SEQ - COMP_BLOCK) // COMP_STRIDE + 1
S_NS = -(-(PAST_LEN + DEC_SEQ) // SEL_BLOCK)
S_KEYS = 17 * LANES
WIN_BUF = min(WINDOW, PAST_LEN)
S_WKEYS = 5 * LANES
ROWS16 = GQA * DEC_SEQ
HEADS8 = 2 * N_KV


def _row_softmax(s, ok):
    s = jnp.where(ok, s, NEG_BIG)
    m = jnp.max(s, axis=-1, keepdims=True)
    e = jnp.where(ok, jnp.exp2(s - m), 0.0)
    den = jnp.sum(e, axis=-1, keepdims=True)
    return e / jnp.where(den > 0, den, 1.0)


def _nsa_sample_kernel(pt_ref, *refs):
    cp = refs[:N_PAGES]
    sp = refs[N_PAGES:2 * N_PAGES]
    (win_ref, news_ref, neww_ref, q_ref, g_ref, pe_ref, w1_ref, w2_ref, smap_ref, expand_ref,
     o_ref, ks_ref, kw_ref) = refs[2 * N_PAGES:]
    del pt_ref

    for c in range(HEADS8):
        kv, h = c // N_KV, c % N_KV
        for p in range(N_PAGES):
            ks_ref[c, :, PAGE_SIZE * p:PAGE_SIZE * (p + 1)] = sp[p][kv, h].astype(BF16)
        ks_ref[c, :, PAST_LEN:] = news_ref[c]
        kw_ref[c, :, 0:WIN_BUF] = win_ref[kv, h].astype(BF16)
        kw_ref[c, :, WIN_BUF:] = neww_ref[c]

    kcv = []
    for kv in range(2):
        x = jnp.concatenate([cp[p][N_KV * kv + h] for h in range(N_KV) for p in range(N_PAGES)], axis=0)
        a0 = jnp.dot((x + pe_ref[kv, 0]).astype(BF16), w1_ref[kv, 0], preferred_element_type=F32)
        a1 = jnp.dot((x + pe_ref[kv, 1]).astype(BF16), w1_ref[kv, 1], preferred_element_type=F32)
        hid = a0 + pltpu.roll(a1, N_KV * S_CHUNKS - 1, axis=0)
        out = jnp.dot(_gelu_tanh(hid).astype(BF16), w2_ref[kv], preferred_element_type=F32).astype(BF16)
        kcv += [out[S_CHUNKS * h:S_CHUNKS * (h + 1)] for h in range(N_KV)]

    row = lax.broadcasted_iota(jnp.int32, (ROWS16, 1), 0)
    q_pos = PAST_LEN + (row & (DEC_SEQ - 1))
    grp = row >> 2

    def slopes(k):
        s = jnp.full((ROWS16, 1), _alibi_slope(4 * k), F32)
        for g in range(1, GQA):
            s = jnp.where(grp == g, _alibi_slope(4 * k + g), s)
        return s

    c_end = lax.broadcasted_iota(jnp.int32, (1, S_CHUNKS), 1) * COMP_STRIDE + (COMP_BLOCK - 1)
    d_c = q_pos - c_end
    o_cmp, psums = [], []
    for k in range(N_KV):
        s = _dot_nt(q_ref[k], kcv[k]) - slopes(k) * d_c.astype(F32)
        p = _row_softmax(s, d_c >= 0)
        o_cmp.append(jnp.dot(p.astype(BF16), kcv[N_KV + k], preferred_element_type=F32))
        ps = p
        for g in range(1, GQA):
            ps = ps + pltpu.roll(p, DEC_SEQ * g, axis=0)
        psums.append(ps)
    psum_all = jnp.concatenate(psums + [jnp.zeros((LANES - N_KV * ROWS16, S_CHUNKS), F32)], axis=0)
    imp_t = _dot_nt(smap_ref[...], psum_all, precision=lax.Precision.HIGHEST)
    col = lax.broadcasted_iota(jnp.int32, (1, LANES), 1)
    sel_t = _select_blocks_by_rank(imp_t, PAST_LEN + (col & (DEC_SEQ - 1)), N_SEL, S_NS)
    sel = sel_t.T.astype(BF16)

    key_s = lax.broadcasted_iota(jnp.int32, (1, S_KEYS), 1)
    d_s = q_pos - key_s
    idx_w = lax.broadcasted_iota(jnp.int32, (1, S_WKEYS), 1)
    d_w = q_pos - (PAST_LEN - WIN_BUF + idx_w)
    ok_w = (d_w >= 0) & (d_w <= WINDOW)
    for k in range(N_KV):
        chosen = jnp.dot(sel[ROWS16 * k:ROWS16 * (k + 1)], expand_ref[...], preferred_element_type=F32)
        s = jnp.dot(q_ref[k], ks_ref[k], preferred_element_type=F32) - slopes(k) * d_s.astype(F32)
        p = _row_softmax(s, (chosen > 0.5) & (d_s >= 0))
        o_sel = _dot_nt(p.astype(BF16), ks_ref[N_KV + k])
        s = jnp.dot(q_ref[k], kw_ref[k], preferred_element_type=F32) - slopes(k) * d_w.astype(F32)
        p = _row_softmax(s, ok_w)
        o_win = _dot_nt(p.astype(BF16), kw_ref[N_KV + k])
        o_ref[k] = (g_ref[k][:, 0:1] * o_cmp[k] + g_ref[k][:, 1:2] * o_sel + g_ref[k][:, 2:3] * o_win)


def _nsa_sample(page_table, cache_c, cache_s, win_prev, q_s, gates_s, kvs_new, kvw_new, cw):
    pos_k, w1_k, w2_k, pos_v, w1_v, w2_v = cw
    flat = COMP_STRIDE * HEAD_DIM
    n_phys = cache_c.shape[1]
    w1 = jnp.stack([w1_k, w1_v]).reshape(2, 2, flat, CMP_HIDDEN).astype(BF16)
    pe = jnp.stack([pos_k, pos_v]).reshape(2, 2, 1, flat)
    w2 = jnp.stack([w2_k, w2_v]).astype(BF16)
    chunks = cache_c.reshape(n_phys, CHUNKS_PER_PAGE, COMP_STRIDE, HEADS8, HEAD_DIM)
    chunks = chunks.transpose(0, 3, 1, 2, 4).reshape(n_phys, HEADS8, CHUNKS_PER_PAGE, flat)
    slc_t = cache_s.transpose(0, 1, 3, 4, 5, 2)
    win_t = win_prev.transpose(0, 1, 3, 4, 5, 2)
    qh = q_s.reshape(DEC_SEQ, DEC_BATCH, N_KV, GQA, HEAD_DIM).transpose(1, 2, 3, 0, 4)
    q16 = qh.reshape(DEC_BATCH, N_KV, ROWS16, HEAD_DIM)
    gh = gates_s[:, :3 * N_HEADS].reshape(DEC_SEQ, DEC_BATCH, 3, N_KV, GQA).transpose(1, 3, 4, 0, 2)
    g16 = jnp.pad(gh.reshape(DEC_BATCH, N_KV, ROWS16, 3), ((0, 0), (0, 0), (0, 0), (0, LANES - 3)))

    def new_rows_t(a, width):
        a = a.reshape(DEC_SEQ, DEC_BATCH, HEADS8, HEAD_DIM).transpose(1, 2, 3, 0).astype(BF16)
        return jnp.pad(a, ((0, 0), (0, 0), (0, 0), (0, width - DEC_SEQ)))

    smap_t = _overlap_map(S_NC, S_NS, S_CHUNKS, LANES)
    expand = jnp.asarray((np.arange(LANES)[:, None] == (np.arange(S_KEYS)[None, :] // SEL_BLOCK)), BF16)
    pt = page_table.reshape(-1).astype(jnp.int32)
    c_specs = [pl.BlockSpec((None, HEADS8, CHUNKS_PER_PAGE, flat), lambda b, pt, p=p: (pt[b * N_PAGES + p], 0, 0, 0))
               for p in range(N_PAGES)]
    s_specs = [pl.BlockSpec((None, None, 2, N_KV, HEAD_DIM, PAGE_SIZE),
                            lambda b, pt, p=p: (0, pt[b * N_PAGES + p], 0, 0, 0, 0)) for p in range(N_PAGES)]
    per_b4 = lambda shape: pl.BlockSpec((None,) + shape, lambda b, pt: (b, 0, 0, 0))
    const = lambda shape: pl.BlockSpec(shape, lambda b, pt: (0,) * len(shape))
    o16 = pl.pallas_call(
        _nsa_sample_kernel,
        grid_spec=pltpu.PrefetchScalarGridSpec(
            num_scalar_prefetch=1,
            grid=(DEC_BATCH,),
            in_specs=c_specs + s_specs + [
                pl.BlockSpec((None, None, 2, N_KV, HEAD_DIM, WIN_BUF), lambda b, pt: (0, b, 0, 0, 0, 0)),
                per_b4((HEADS8, HEAD_DIM, S_KEYS - PAST_LEN)),
                per_b4((HEADS8, HEAD_DIM, S_WKEYS - WIN_BUF)),
                per_b4((N_KV, ROWS16, HEAD_DIM)),
                per_b4((N_KV, ROWS16, LANES)),
                const((2, 2, 1, flat)),
                const((2, 2, flat, CMP_HIDDEN)),
                const((2, CMP_HIDDEN, HEAD_DIM)),
                const((LANES, S_CHUNKS)),
                const((LANES, S_KEYS)),
            ],
            out_specs=per_b4((N_KV, ROWS16, HEAD_DIM)),
            scratch_shapes=[pltpu.VMEM((HEADS8, HEAD_DIM, S_KEYS), BF16), pltpu.VMEM((HEADS8, HEAD_DIM, S_WKEYS), BF16)],
        ),
        out_shape=jax.ShapeDtypeStruct((DEC_BATCH, N_KV, ROWS16, HEAD_DIM), F32),
        compiler_params=_cparams("parallel"),
        name="nsa_sample",
    )(pt, *([chunks] * N_PAGES), *([slc_t] * N_PAGES), win_t,
      new_rows_t(kvs_new, S_KEYS - PAST_LEN), new_rows_t(kvw_new, S_WKEYS - WIN_BUF),
      q16, g16, pe, w1, w2, smap_t, expand)
    o = o16.reshape(DEC_BATCH, N_KV, GQA, DEC_SEQ, HEAD_DIM)
    return o.transpose(3, 0, 1, 2, 4).reshape(N_SAMPLE, Q_WIDTH).astype(BF16)


OUT_TM = 512
ROUTE_E0, ROUTE_E1, ROUTE_W0, ROUTE_W1, ROUTE_R0, ROUTE_R1 = range(6)


def _layer_norm(x, g, b):
    mu = jnp.mean(x, axis=-1, keepdims=True)
    xc = x - mu
    var = jnp.mean(xc * xc, axis=-1, keepdims=True)
    return xc * lax.rsqrt(var + LN_EPS) * g + b


def _first_max(vals, lane_f):
    mx = jnp.max(vals, axis=-1, keepdims=True)
    idx = jnp.min(jnp.where(vals == mx, lane_f, float(LANES)), axis=-1, keepdims=True)
    return mx, idx


def _out_route_kernel(conv_ref, attn_ref, x_ref, wo_ref, g1_ref, b1_ref, wr_ref, br_ref,
                      h_ref, route_ref, counts_ref, run_ref):
    i = pl.program_id(0)

    @pl.when(i == 0)
    def _():
        run_ref[...] = jnp.zeros_like(run_ref)

    mix = jnp.dot(conv_ref[...], wo_ref[0:CONV_CH, :], preferred_element_type=F32)
    mix = mix + jnp.dot(attn_ref[...], wo_ref[CONV_CH:, :], preferred_element_type=F32)
    h = _layer_norm(ALPHA * x_ref[...] + mix, g1_ref[...], b1_ref[...])
    h_ref[...] = h

    logit = jnp.dot(h, wr_ref[...], preferred_element_type=F32, precision=lax.Precision.HIGHEST) + br_ref[...]
    lane = lax.broadcasted_iota(jnp.int32, logit.shape, 1)
    lane_f = lane.astype(F32)
    is_grp = (lane >= N_EXPERTS) & (lane < N_EXPERTS + N_GROUPS)
    g_logit = jnp.where(is_grp, logit, NEG_BIG)
    g_top, g_lane = _first_max(g_logit, lane_f)
    lse = g_top + jnp.log(jnp.sum(jnp.where(is_grp, jnp.exp(g_logit - g_top), 0.0), axis=-1, keepdims=True))
    g_p = jnp.exp(g_top - lse)
    g_idx = g_lane.astype(jnp.int32) - N_EXPERTS
    in_grp = (lane >> 3) == g_idx
    e_logit = jnp.where(in_grp, logit, NEG_BIG)
    e0, i0 = _first_max(e_logit, lane_f)
    e1, i1 = _first_max(jnp.where(lane_f == i0, NEG_BIG, e_logit), lane_f)
    z1 = jnp.exp(e1 - e0)
    w0 = g_p / (1.0 + z1)
    w1 = g_p * z1 / (1.0 + z1)

    hit = jnp.where((lane_f == i0) | (lane_f == i1), 1.0, 0.0)
    r = lax.broadcasted_iota(jnp.int32, (OUT_TM, OUT_TM), 0)
    c = lax.broadcasted_iota(jnp.int32, (OUT_TM, OUT_TM), 1)
    lower = jnp.where(c < r, 1.0, 0.0).astype(BF16)
    before = jnp.dot(lower, hit.astype(BF16), preferred_element_type=F32) + run_ref[...]
    r0 = jnp.sum(jnp.where(lane_f == i0, before, 0.0), axis=-1, keepdims=True)
    r1 = jnp.sum(jnp.where(lane_f == i1, before, 0.0), axis=-1, keepdims=True)
    run_ref[...] = run_ref[...] + jnp.sum(hit, axis=0, keepdims=True)
    counts_ref[...] = jnp.broadcast_to(run_ref[...], counts_ref.shape)

    rec = jnp.zeros(logit.shape, F32)
    for col_id, v in ((ROUTE_E0, i0), (ROUTE_E1, i1), (ROUTE_W0, w0), (ROUTE_W1, w1), (ROUTE_R0, r0), (ROUTE_R1, r1)):
        rec = jnp.where(lane == col_id, v, rec)
    route_ref[...] = rec


def _out_route(conv, attn, x, w_out, ln_g, ln_b, w_grp, b_grp, w_exp, b_exp):
    n = x.shape[0]
    wr = jnp.pad(jnp.concatenate([w_exp, w_grp], axis=1), ((0, 0), (0, LANES - N_EXPERTS - N_GROUPS)))
    br = jnp.pad(jnp.concatenate([b_exp, b_grp]), (0, LANES - N_EXPERTS - N_GROUPS)).reshape(1, LANES)
    const = lambda shape: pl.BlockSpec(shape, lambda i: (0, 0))
    return pl.pallas_call(
        _out_route_kernel,
        grid=(n // OUT_TM,),
        in_specs=[
            pl.BlockSpec((OUT_TM, CONV_CH), lambda i: (i, 0)),
            pl.BlockSpec((OUT_TM, Q_WIDTH), lambda i: (i, 0)),
            pl.BlockSpec((OUT_TM, D_MODEL), lambda i: (i, 0)),
            pl.BlockSpec((D_MODEL, D_MODEL), lambda i: (0, 0), pipeline_mode=pl.Buffered(1)),
            const((1, D_MODEL)),
            const((1, D_MODEL)),
            const((D_MODEL, LANES)),
            const((1, LANES)),
        ],
        out_specs=[
            pl.BlockSpec((OUT_TM, D_MODEL), lambda i: (i, 0)),
            pl.BlockSpec((OUT_TM, LANES), lambda i: (i, 0)),
            const((8, LANES)),
        ],
        out_shape=[
            jax.ShapeDtypeStruct((n, D_MODEL), F32),
            jax.ShapeDtypeStruct((n, LANES), F32),
            jax.ShapeDtypeStruct((8, LANES), F32),
        ],
        scratch_shapes=[pltpu.VMEM((1, LANES), F32)],
        compiler_params=_cparams("arbitrary"),
        name="out_ln1_route",
    )(conv, attn, x, w_out.astype(BF16), ln_g.reshape(1, D_MODEL), ln_b.reshape(1, D_MODEL), wr, br)


MOE_BLK = 128
N_ASSIGN = N_TOK * TOP_K
MOE_BLOCKS = -(-(N_ASSIGN + N_EXPERTS * (MOE_BLK - 1)) // MOE_BLK)
MOE_ROWS = MOE_BLOCKS * MOE_BLK
CMB_TM = 128


def _gather_rows(src_hbm, dst_ref, sem, idx_ref, base, n_rows):
    def row_copy(r, priority):
        pltpu.make_async_copy(src_hbm.at[pl.ds(idx_ref[base + r], 1)], dst_ref.at[pl.ds(r, 1)], sem).start(priority)

    def body(j, carry):
        row_copy(2 * j, 0)
        row_copy(2 * j + 1, 1)
        return carry
    lax.fori_loop(0, n_rows // 2, body, 0)
    if isinstance(n_rows, int):
        assert n_rows % 2 == 0
    else:
        @pl.when((n_rows & 1) == 1)
        def _():
            row_copy(n_rows - 1, 0)


def _wait_rows(src_hbm, dst_ref, sem):
    pltpu.make_async_copy(src_hbm.at[pl.ds(0, dst_ref.shape[0])], dst_ref, sem).wait()


def _wait_row_count(src_hbm, dst_ref, sem, n_rows):
    p = MOE_BLK
    while p >= 1:
        @pl.when((n_rows & p) != 0)
        def _(p=p):
            pltpu.make_async_copy(src_hbm.at[pl.ds(0, p)], dst_ref.at[pl.ds(0, p)], sem).wait()
        p //= 2


def _moe_kernel(blk_e_ref, row_tok_ref, blk_rows_ref, h_hbm, wg_ref, wu_ref, wd_ref, y_ref,
                xbuf, sems, wg_b, wu_b, wd_b):
    i = pl.program_id(0)
    slot = i % 2
    rows = blk_rows_ref[i]
    nxt = jnp.minimum(i + 1, MOE_BLOCKS - 1)
    rows_next = jnp.where(i + 1 < MOE_BLOCKS, blk_rows_ref[nxt], 0)

    @pl.when(i == 0)
    def _():
        xbuf[...] = jnp.zeros_like(xbuf)
        _gather_rows(h_hbm, xbuf.at[0], sems.at[0], row_tok_ref, 0, rows)

    @pl.when(rows_next > 0)
    def _():
        _gather_rows(h_hbm, xbuf.at[1 - slot], sems.at[1 - slot], row_tok_ref, (i + 1) * MOE_BLK, rows_next)

    @pl.when((i == 0) | (blk_e_ref[i] != blk_e_ref[jnp.maximum(i - 1, 0)]))
    def _():
        wg_b[...] = wg_ref[...].astype(BF16)
        wu_b[...] = wu_ref[...].astype(BF16)
        wd_b[...] = wd_ref[...].astype(BF16)

    _wait_row_count(h_hbm, xbuf.at[slot], sems.at[slot], rows)

    @pl.when(rows > 0)
    def _():
        x = xbuf[slot].astype(BF16)
        hg = jnp.dot(x, wg_b[...], preferred_element_type=F32)
        hu = jnp.dot(x, wu_b[...], preferred_element_type=F32)
        act = (hg * jax.nn.sigmoid(hg) * hu).astype(BF16)
        y_ref[...] = jnp.dot(act, wd_b[...], preferred_element_type=F32)

    @pl.when(rows == 0)
    def _():
        y_ref[...] = jnp.zeros_like(y_ref)


def _moe(h, blk_e, row_tok, blk_rows, w_gate, w_up, w_down):
    wspec = lambda shape: pl.BlockSpec((None,) + shape, lambda i, be, rt, br: (be[i], 0, 0))
    return pl.pallas_call(
        _moe_kernel,
        grid_spec=pltpu.PrefetchScalarGridSpec(
            num_scalar_prefetch=3,
            grid=(MOE_BLOCKS,),
            in_specs=[
                pl.BlockSpec(memory_space=pl.ANY),
                wspec((D_MODEL, MOE_D_FF)),
                wspec((D_MODEL, MOE_D_FF)),
                wspec((MOE_D_FF, D_MODEL)),
            ],
            out_specs=pl.BlockSpec((MOE_BLK, D_MODEL), lambda i, be, rt, br: (i, 0)),
            scratch_shapes=[
                pltpu.VMEM((2, MOE_BLK, D_MODEL), F32),
                pltpu.SemaphoreType.DMA((2,)),
                pltpu.VMEM((D_MODEL, MOE_D_FF), BF16),
                pltpu.VMEM((D_MODEL, MOE_D_FF), BF16),
                pltpu.VMEM((MOE_D_FF, D_MODEL), BF16),
            ],
        ),
        out_shape=jax.ShapeDtypeStruct((MOE_ROWS, D_MODEL), F32),
        compiler_params=_cparams("arbitrary"),
        name="moe_ffn",
    )(blk_e, row_tok, blk_rows, h, w_gate, w_up, w_down)


def _combine_kernel(dest_ref, y_hbm, h_ref, route_ref, g2_ref, b2_ref, o_ref, ybuf, sems):
    i = pl.program_id(0)
    n_steps = pl.num_programs(0)
    slot = i % 2

    @pl.when(i == 0)
    def _():
        _gather_rows(y_hbm, ybuf.at[0], sems.at[0], dest_ref, 0, TOP_K * CMB_TM)

    @pl.when(i + 1 < n_steps)
    def _():
        _gather_rows(y_hbm, ybuf.at[1 - slot], sems.at[1 - slot], dest_ref, (i + 1) * TOP_K * CMB_TM, TOP_K * CMB_TM)

    _wait_rows(y_hbm, ybuf.at[slot], sems.at[slot])
    w0 = route_ref[:, ROUTE_W0:ROUTE_W0 + 1]
    w1 = route_ref[:, ROUTE_W1:ROUTE_W1 + 1]
    moe = ybuf[slot, 0:CMB_TM] * w0 + ybuf[slot, CMB_TM:2 * CMB_TM] * w1
    o_ref[...] = _layer_norm(ALPHA * h_ref[...] + moe, g2_ref[...], b2_ref[...])


def _combine(dest, y_rows, h, route, ln_g, ln_b):
    n = h.shape[0]
    return pl.pallas_call(
        _combine_kernel,
        grid_spec=pltpu.PrefetchScalarGridSpec(
            num_scalar_prefetch=1,
            grid=(n // CMB_TM,),
            in_specs=[
                pl.BlockSpec(memory_space=pl.ANY),
                pl.BlockSpec((CMB_TM, D_MODEL), lambda i, d: (i, 0)),
                pl.BlockSpec((CMB_TM, LANES), lambda i, d: (i, 0)),
                pl.BlockSpec((1, D_MODEL), lambda i, d: (0, 0)),
                pl.BlockSpec((1, D_MODEL), lambda i, d: (0, 0)),
            ],
            out_specs=pl.BlockSpec((CMB_TM, D_MODEL), lambda i, d: (i, 0)),
            scratch_shapes=[pltpu.VMEM((2, TOP_K * CMB_TM, D_MODEL), F32), pltpu.SemaphoreType.DMA((2,))],
        ),
        out_shape=jax.ShapeDtypeStruct((n, D_MODEL), F32),
        compiler_params=_cparams("arbitrary"),
        name="moe_combine_ln2",
    )(dest, y_rows, h, route, ln_g.reshape(1, D_MODEL), ln_b.reshape(1, D_MODEL))


def _dispatch_indices(route, counts):
    eid = route[:, ROUTE_E0:ROUTE_E1 + 1].astype(jnp.int32)
    rank = route[:, ROUTE_R0:ROUTE_R1 + 1].astype(jnp.int32)
    cnt = counts[0, :N_EXPERTS].astype(jnp.int32)
    padded = (cnt + MOE_BLK - 1) // MOE_BLK * MOE_BLK
    pad_end = jnp.cumsum(padded)
    pad_start = pad_end - padded
    dest = pad_start[eid] + rank
    tok = jnp.broadcast_to(jnp.arange(N_TOK, dtype=jnp.int32)[:, None], dest.shape)
    row_tok = jnp.zeros((MOE_ROWS,), jnp.int32).at[dest.reshape(-1)].set(tok.reshape(-1))
    blk_start = jnp.arange(MOE_BLOCKS, dtype=jnp.int32) * MOE_BLK
    blk_e = jnp.minimum(jnp.sum(pad_end[None, :] <= blk_start[:, None], axis=1), N_EXPERTS - 1).astype(jnp.int32)
    blk_rows = jnp.clip(cnt[blk_e] - (blk_start - pad_start[blk_e]), 0, MOE_BLK).astype(jnp.int32)
    dest_tiles = dest.reshape(N_TOK // CMB_TM, CMB_TM, TOP_K).transpose(0, 2, 1).reshape(-1)
    return blk_e, row_tok, blk_rows, dest_tiles


def kernel(x_prompt, x_sample, cache_cmp_kv, cache_slc_kv, state_win_kv, state_conv, page_table, w_in, conv_w, conv_b, conv_ln_g, conv_ln_b, cmp_pos_k, cmp_w1_k, cmp_w2_k, cmp_pos_v, cmp_w1_v, cmp_w2_v, w_out, ln1_g, ln1_b, w_grp, b_grp, w_exp, b_exp, w_gate, w_up, w_down, ln2_g, ln2_b):
    xs_t = x_sample.transpose(1, 0, 2).reshape(N_SAMPLE, D_MODEL)
    x = jnp.concatenate([x_prompt.reshape(N_PROMPT, D_MODEL), xs_t], axis=0)
    glu, q, kvc, kvs, kvw, gates, ks_k, vs_t, kw_k, vw_t = _proj(x, w_in[0])
    conv_p = _conv_prompt(glu, conv_w[0], conv_b[0], conv_ln_g[0], conv_ln_b[0])
    a_s = glu[N_PROMPT:].reshape(DEC_SEQ, DEC_BATCH, CONV_CH)
    xt = jnp.concatenate([state_conv[0].transpose(1, 0, 2), a_s], axis=0)
    conv_s = _conv_sample(xt, conv_w[0], conv_b[0], conv_ln_g[0], conv_ln_b[0])
    cw = (cmp_pos_k[0], cmp_w1_k[0], cmp_w2_k[0], cmp_pos_v[0], cmp_w1_v[0], cmp_w2_v[0])
    kcv_p = _compress_prompt(kvc, cw)
    o_c, drop = _cmp_prompt(q, kcv_p)
    attn_p = _selw_prompt(q, gates, o_c, drop, ks_k, vs_t, kw_k, vw_t)
    attn_s = _nsa_sample(page_table, cache_cmp_kv, cache_slc_kv, state_win_kv, q[N_PROMPT:],
                         gates[N_PROMPT:], kvs[N_PROMPT:], kvw[N_PROMPT:], cw)
    conv = jnp.concatenate([conv_p, conv_s.reshape(N_SAMPLE, CONV_CH)], axis=0)
    attn = jnp.concatenate([attn_p, attn_s], axis=0)
    h1, route, counts = _out_route(conv, attn, x, w_out[0], ln1_g[0], ln1_b[0], w_grp[0], b_grp[0],
                                   w_exp[0], b_exp[0])
    blk_e, row_tok, blk_rows, dest_tiles = _dispatch_indices(route, counts)
    y_rows = _moe(h1, blk_e, row_tok, blk_rows, w_gate[0], w_up[0], w_down[0])
    y = _combine(dest_tiles, y_rows, h1, route, ln2_g[0], ln2_b[0])

    def batch_major(a):
        return a[N_PROMPT:].reshape(DEC_SEQ, DEC_BATCH, -1).transpose(1, 0, 2)

    kv_shape = (2, N_KV, HEAD_DIM)
    win_len = min(WINDOW, SEQ)
    y_prompt = y[:N_PROMPT].reshape(1, SEQ, D_MODEL)
    y_sample = batch_major(y)
    p_cmp = kvc[:N_PROMPT].reshape(1, 1, SEQ, *kv_shape)
    p_slc = kvs[:N_PROMPT].reshape(1, 1, SEQ, *kv_shape)
    p_win = kvw[N_PROMPT - win_len:N_PROMPT].reshape(1, 1, win_len, *kv_shape)
    p_conv = glu[N_PROMPT - (CONV_WIDTH - 1):N_PROMPT].reshape(1, 1, CONV_WIDTH - 1, CONV_CH)
    s_cmp = batch_major(kvc).reshape(1, DEC_BATCH, DEC_SEQ, *kv_shape)
    s_slc = batch_major(kvs).reshape(1, DEC_BATCH, DEC_SEQ, *kv_shape)
    s_win = jnp.concatenate([state_win_kv[0][:, DEC_SEQ:],
                             batch_major(kvw).reshape(DEC_BATCH, DEC_SEQ, *kv_shape)], axis=1)[None]
    s_conv = jnp.concatenate([state_conv[0][:, DEC_SEQ:], batch_major(glu)], axis=1)[None]
    return (y_prompt, y_sample, p_cmp, p_slc, p_win, p_conv, s_cmp, s_slc, s_win, s_conv)
```
